```python
import jax
import jax.numpy as jnp
from jax import lax
import numpy as np


D_MODEL = 2048
BATCH = 2
SEQ = 16384
DEPTH = 2

GRID_W = 64
CTX_LEN = 256
EPS = 1e-6
HEAD_DIM = 128
ROPE_BASE = 10000.0
RET_HEADS = D_MODEL // (2 * HEAD_DIM)
RET_WIDTH = RET_HEADS * HEAD_DIM
RET_CHUNK = 128
NA_HEADS = D_MODEL // (2 * HEAD_DIM)
NA_WIDTH = NA_HEADS * HEAD_DIM
NA_ROWS = 8
NA_COLS = 16
EVEN_IN = 4 * RET_WIDTH + 3 * NA_WIDTH
SSD_INNER = 2 * D_MODEL
SSD_HEAD_DIM = 64
SSD_HEADS = SSD_INNER // SSD_HEAD_DIM
SSD_GROUPS = 8
SSD_STATE = 128
SSD_CONV = 5
SSD_CHUNK = 128
SSD_BC = SSD_GROUPS * SSD_STATE
SSD_CONV_DIM = SSD_INNER + 2 * SSD_BC
ODD_IN = SSD_INNER + SSD_CONV_DIM + SSD_HEADS
N_EXPERTS = 32
N_GROUPS = 8
EXPERTS_PER_GROUP = N_EXPERTS // N_GROUPS
TOP_K = 2
EXPERT_HIDDEN = 512
MOE_BLOCK = 256
N_EVEN = (DEPTH + 1) // 2
N_ODD = DEPTH // 2

kernel_name = 'hybrid_retention_na_ssd_moe_dit'


def rmsnorm(x, w):
    xf = x.astype(jnp.float32)
    y = xf * lax.rsqrt(jnp.mean(xf * xf, axis=-1, keepdims=True) + EPS)
    return (y * w.astype(jnp.float32)).astype(x.dtype)


def head_rms(o):
    of = o.astype(jnp.float32)
    return of * lax.rsqrt(jnp.mean(of * of, axis=-1, keepdims=True) + EPS)


def modulate(h, shift, scale):
    return h * (1 + scale) + shift


def split_heads(a, n_heads):
    b, l, _ = a.shape
    return a.reshape(b, l, n_heads, -1).transpose(0, 2, 1, 3)


def merge_heads(a):
    b, h, l, d = a.shape
    return a.transpose(0, 2, 1, 3).reshape(b, l, h * d)


def axial_rope(t):
    L, d = t.shape[2], t.shape[3]
    pos = jnp.arange(L)
    rows = (pos // GRID_W).astype(jnp.float32)
    cols = (pos % GRID_W).astype(jnp.float32)
    half = d // 2
    inv = ROPE_BASE ** (-jnp.arange(0, half, 2, dtype=jnp.float32) / half)
    ang = jnp.concatenate([rows[:, None] * inv, cols[:, None] * inv], axis=-1)
    cos, sin = jnp.cos(ang), jnp.sin(ang)
    tf = t.astype(jnp.float32)
    t1, t2 = tf[..., 0::2], tf[..., 1::2]
    out = jnp.stack([t1 * cos - t2 * sin, t1 * sin + t2 * cos], axis=-1).reshape(t.shape)
    return out.astype(t.dtype)


def dense_attention(q, k, v):
    s = jnp.einsum('bhqd,bhkd->bhqk', q, k).astype(jnp.float32) * (q.shape[-1] ** -0.5)
    p = jax.nn.softmax(s, axis=-1).astype(v.dtype)
    return jnp.einsum('bhqk,bhkd->bhqd', p, v)


def retention_scan(q, k, v, log_gamma, state0):
    b_, h, L, d = q.shape
    n, Q = L // RET_CHUNK, RET_CHUNK

    def split(a):
        return jnp.moveaxis(a.reshape(b_, h, n, Q, a.shape[-1]), 2, 0)

    idx = jnp.arange(Q, dtype=jnp.float32)
    lg = log_gamma[:, None]
    rel = idx[:, None] - idx[None, :]
    inner_decay = jnp.where(rel >= 0, jnp.exp(lg[:, :, None] * jnp.maximum(rel, 0.0)), 0.0)
    q_decay = jnp.exp(lg * (idx + 1.0))[..., None]
    k_decay = jnp.exp(lg * (Q - 1.0 - idx))[..., None]
    chunk_decay = jnp.exp(log_gamma * Q)[:, None, None]

    def step(S, qkv):
        qc, kc, vc = qkv
        s = jnp.einsum('bhid,bhjd->bhij', qc, kc) * inner_decay
        o = jnp.einsum('bhij,bhjd->bhid', s, vc) + jnp.einsum('bhid,bhde->bhie', qc * q_decay, S)
        S = chunk_decay * S + jnp.einsum('bhjd,bhje->bhde', kc * k_decay, vc)
        return S, o

    S, o = lax.scan(step, state0, (split(q), split(k), split(v)))
    return jnp.moveaxis(o, 0, 2).reshape(b_, h, L, -1), S


def bidir_retention(q_l, k_l, v_l, q_c, k_c, v_c, log_gamma):
    b_, h, _, d = q_l.shape
    zero = jnp.zeros((b_, h, d, d), jnp.float32)
    q_l, k_l, v_l, q_c, k_c, v_c = [a.astype(jnp.float32) for a in (q_l, k_l, v_l, q_c, k_c, v_c)]

    def rev(a):
        return jnp.flip(a, axis=2)

    o_cf, s_f = retention_scan(q_c, k_c, v_c, log_gamma[0], zero)
    o_cb, s_b = retention_scan(rev(q_c), rev(k_c), rev(v_c), log_gamma[1], zero)
    o_lf, _ = retention_scan(q_l, k_l, v_l, log_gamma[0], s_f)
    o_lb, _ = retention_scan(rev(q_l), rev(k_l), rev(v_l), log_gamma[1], s_b)
    return o_lf + rev(o_lb), o_cf + rev(o_cb)


def neighbourhood_attention(q, k, v, k_ctx, v_ctx, rpb):
    b_, h, L, d = q.shape
    rows = L // GRID_W
    wr, wc = min(NA_ROWS, rows), NA_COLS
    scale = d ** -0.5
    qg = q.reshape(b_, h, rows, GRID_W, d)
    kg = k.reshape(b_, h, rows, GRID_W, d)
    vg = v.reshape(b_, h, rows, GRID_W, d)
    cols = np.arange(GRID_W)
    c0 = np.clip(cols - wc // 2, 0, GRID_W - wc)
    col_idx = c0[:, None] + np.arange(wc)[None, :]
    col_off = col_idx - cols[:, None] + (NA_COLS - 1)
    rpb_c = rpb[:, :, col_off].astype(jnp.float32)

    def row_block(r):
        r0 = jnp.clip(r - wr // 2, 0, rows - wr)
        qr = lax.dynamic_index_in_dim(qg, r, axis=2, keepdims=False)
        kb = lax.dynamic_slice_in_dim(kg, r0, wr, axis=2)[:, :, :, col_idx]
        vb = lax.dynamic_slice_in_dim(vg, r0, wr, axis=2)[:, :, :, col_idx]
        row_off = r0 + jnp.arange(wr) - r + (NA_ROWS - 1)
        bias = jnp.take(rpb_c, row_off, axis=1).transpose(0, 2, 1, 3)
        s_loc = jnp.einsum('bhqd,bhrqcd->bhqrc', qr, kb).astype(jnp.float32) * scale + bias
        s_ctx = jnp.einsum('bhqd,bhkd->bhqk', qr, k_ctx).astype(jnp.float32) * scale
        s = jnp.concatenate([s_loc.reshape(b_, h, GRID_W, wr * wc), s_ctx], axis=-1)
        p = jax.nn.softmax(s, axis=-1).astype(v.dtype)
        p_loc = p[..., :wr * wc].reshape(b_, h, GRID_W, wr, wc)
        return (jnp.einsum('bhqrc,bhrqcd->bhqd', p_loc, vb)
                + jnp.einsum('bhqk,bhkd->bhqd', p[..., wr * wc:], v_ctx))

    o = lax.map(row_block, jnp.arange(rows))
    return jnp.moveaxis(o, 0, 2).reshape(b_, h, L, d)


def even_mixer(h_lat, h_ctx, w_in, w_out, ret_logit, rpb, with_ctx_out):
    cuts = [RET_WIDTH * (n + 1) for n in range(4)] + [4 * RET_WIDTH + NA_WIDTH * (n + 1) for n in range(2)]

    def project(h):
        qr, kr, vr, gr, qn, kn, vn = jnp.split(h @ w_in, cuts, axis=-1)
        return (split_heads(qr, RET_HEADS), split_heads(kr, RET_HEADS) * (HEAD_DIM ** -0.5),
                split_heads(vr, RET_HEADS), gr,
                split_heads(qn, NA_HEADS), split_heads(kn, NA_HEADS), split_heads(vn, NA_HEADS))

    lq, lk, lv, lg, lqn, lkn, lvn = project(h_lat)
    cq, ck, cv, cg, cqn, ckn, cvn = project(h_ctx)
    log_gamma = jax.nn.log_sigmoid(ret_logit.astype(jnp.float32))
    ret_l, ret_c = bidir_retention(axial_rope(lq), axial_rope(lk), lv, cq, ck, cv, log_gamma)

    def retention_out(o, g):
        return merge_heads(head_rms(o).astype(g.dtype)) * jax.nn.silu(g)

    na_l = neighbourhood_attention(lqn, lkn, lvn, ckn, cvn, rpb)
    out_lat = jnp.concatenate([retention_out(ret_l, lg), merge_heads(na_l)], axis=-1) @ w_out
    if not with_ctx_out:
        return out_lat, None
    na_c = dense_attention(cqn, ckn, cvn)
    out_ctx = jnp.concatenate([retention_out(ret_c, cg), merge_heads(na_c)], axis=-1) @ w_out
    return out_lat, out_ctx


def depthwise_conv(u, w, b):
    k = w.shape[0]
    y = lax.conv_general_dilated(u, w[:, None, :].astype(u.dtype), window_strides=(1,),
                                 padding=[(k // 2, k // 2)], dimension_numbers=('NWC', 'WIO', 'NWC'),
                                 feature_group_count=u.shape[-1])
    return y + b


def ssd_scan(x, dt, A, Bm, Cm, state0):
    b_, L, h, p = x.shape
    g, n_st = Bm.shape[2], Bm.shape[3]
    hg = h // g
    n, Q = L // SSD_CHUNK, SSD_CHUNK

    def split(a):
        return jnp.moveaxis(a.reshape((b_, n, Q) + a.shape[2:]), 1, 0)

    xs = (split(x.reshape(b_, L, g, hg, p)), split(dt.reshape(b_, L, g, hg)), split(Bm), split(Cm))
    A_g = A.reshape(g, hg)
    lower = jnp.tril(jnp.ones((Q, Q), bool))[None, :, :, None, None]

    def step(S, inp):
        xc, dtc, Bc, Cc = inp
        acum = jnp.cumsum(dtc * A_g, axis=1)
        seg = acum[:, :, None] - acum[:, None, :]
        Lm = jnp.exp(jnp.where(lower, seg, -jnp.inf))
        cb = jnp.einsum('bign,bjgn->bijg', Cc, Bc)
        w = cb[..., None] * Lm * dtc[:, None]
        y = jnp.einsum('bijgh,bjghp->bighp', w, xc)
        y = y + jnp.einsum('bign,bghpn->bighp', Cc, S) * jnp.exp(acum)[..., None]
        xw = xc * (jnp.exp(acum[:, -1:] - acum) * dtc)[..., None]
        S = jnp.exp(acum[:, -1])[..., None, None] * S + jnp.einsum('bjgn,bjghp->bghpn', Bc, xw)
        return S, y

    S, y = lax.scan(step, state0, xs)
    return jnp.moveaxis(y, 0, 1).reshape(b_, L, h, p), S


def odd_mixer(h_lat, h_ctx, w_in, conv_w, conv_b, dt_bias, a_log, d_skip, norm_w, w_out, with_ctx_out):
    cuts = [SSD_INNER, SSD_INNER + SSD_CONV_DIM]

    def project(h):
        b_, l, _ = h.shape
        z, xbc, dt = jnp.split(h @ w_in, cuts, axis=-1)
        xbc = jax.nn.silu(depthwise_conv(xbc, conv_w, conv_b))
        xs, bm, cm = jnp.split(xbc, [SSD_INNER, SSD_INNER + SSD_BC], axis=-1)
        return (z, xs.reshape(b_, l, SSD_HEADS, SSD_HEAD_DIM).astype(jnp.float32),
                bm.reshape(b_, l, SSD_GROUPS, SSD_STATE).astype(jnp.float32),
                cm.reshape(b_, l, SSD_GROUPS, SSD_STATE).astype(jnp.float32), dt.astype(jnp.float32))

    A = -jnp.exp(a_log.astype(jnp.float32))
    dtb = dt_bias.astype(jnp.float32)
    d_sum = (d_skip[0] + d_skip[1]).astype(jnp.float32)[:, None]

    def rev(a):
        return jnp.flip(a, axis=1)

    def both_directions(xs, bm, cm, dt, s_f, s_b):
        y_f, s_f = ssd_scan(xs, jax.nn.softplus(dt + dtb[0]), A[0], bm, cm, s_f)
        y_b, s_b = ssd_scan(rev(xs), rev(jax.nn.softplus(dt + dtb[1])), A[1], rev(bm), rev(cm), s_b)
        return y_f + rev(y_b) + d_sum * xs, s_f, s_b

    cz, cxs, cbm, ccm, cdt = project(h_ctx)
    zero = jnp.zeros((cxs.shape[0], SSD_GROUPS, SSD_HEADS // SSD_GROUPS, SSD_HEAD_DIM, SSD_STATE), jnp.float32)
    y_c, s_f, s_b = both_directions(cxs, cbm, ccm, cdt, zero, zero)
    lz, lxs, lbm, lcm, ldt = project(h_lat)
    y_l, _, _ = both_directions(lxs, lbm, lcm, ldt, s_f, s_b)

    def finish(y, z):
        b_, l = z.shape[0], z.shape[1]
        y = y.reshape(b_, l, SSD_INNER).astype(z.dtype) * jax.nn.silu(z)
        return rmsnorm(y, norm_w) @ w_out

    out_lat = finish(y_l, lz)
    if not with_ctx_out:
        return out_lat, None
    return out_lat, finish(y_c, cz)


def moe(h, w_router, router_bias, w_gate, w_up, w_down):
    T, D = h.shape
    scores = jax.nn.sigmoid((h @ w_router).astype(jnp.float32))
    biased = (scores + router_bias.astype(jnp.float32)).reshape(T, N_GROUPS, EXPERTS_PER_GROUP)
    group_score = lax.top_k(biased, 2)[0].sum(-1)
    g_sel = jnp.argmax(group_score, axis=-1)
    in_group = jnp.take_along_axis(biased, g_sel[:, None, None], axis=1)[:, 0]
    _, top_local = lax.top_k(in_group, TOP_K)
    expert_idx = g_sel[:, None] * EXPERTS_PER_GROUP + top_local
    gate = jnp.take_along_axis(scores, expert_idx, axis=-1)
    gate = gate / jnp.sum(gate, axis=-1, keepdims=True)

    n_assign = T * TOP_K
    flat_e = expert_idx.reshape(-1)
    flat_tok = jnp.arange(n_assign, dtype=jnp.int32) // TOP_K
    order = jnp.argsort(flat_e)
    sorted_e = flat_e[order]
    counts = jnp.bincount(flat_e, length=N_EXPERTS)
    padded = (counts + MOE_BLOCK - 1) // MOE_BLOCK * MOE_BLOCK
    pad_end = jnp.cumsum(padded)
    pad_start = pad_end - padded
    start = jnp.cumsum(counts) - counts
    dest = pad_start[sorted_e] + jnp.arange(n_assign) - start[sorted_e]
    n_blocks = -(-(n_assign + N_EXPERTS * (MOE_BLOCK - 1)) // MOE_BLOCK)
    n_slots = n_blocks * MOE_BLOCK
    slot_tok = jnp.full((n_slots,), T, jnp.int32).at[dest].set(flat_tok[order])
    slot_gate = jnp.zeros((n_slots,), jnp.float32).at[dest].set(gate.reshape(-1)[order])
    block_start = jnp.arange(n_blocks) * MOE_BLOCK
    block_expert = jnp.minimum(jnp.searchsorted(pad_end, block_start, side='right'), N_EXPERTS - 1)
    h_pad = jnp.concatenate([h, jnp.zeros((1, D), h.dtype)], axis=0)

    def run_block(blk):
        s0, e = blk
        tok = lax.dynamic_slice_in_dim(slot_tok, s0, MOE_BLOCK)
        gw = lax.dynamic_slice_in_dim(slot_gate, s0, MOE_BLOCK)
        xb = h_pad[tok]
        y = (jax.nn.silu(xb @ w_gate[e]) * (xb @ w_up[e])) @ w_down[e]
        return y * gw[:, None].astype(y.dtype)

    ys = lax.map(run_block, (block_start, block_expert))
    out = jnp.zeros((T + 1, D), h.dtype).at[slot_tok].add(ys.reshape(n_slots, D))
    return out[:T]


def setup_inputs(seed: int = 0) -> dict:
    key = jax.random.key(seed)
    keys = iter(jax.random.split(key, 32))

    def nrm(shape, scale):
        return jax.random.normal(next(keys), shape, jnp.float32) * scale

    def uni(shape, lo, hi):
        return jax.random.uniform(next(keys), shape, jnp.float32, lo, hi)

    d = D_MODEL
    ret_base = jnp.asarray(np.log(2.0 ** (5 + np.arange(RET_HEADS)) - 1.0), jnp.float32)
    dt0 = jnp.exp(uni((N_ODD, 2, SSD_HEADS), float(np.log(1e-3)), float(np.log(1e-1))))
    return {
        'x': nrm((BATCH, SEQ, d), 1.0),
        'c': nrm((BATCH, d), 1.0),
        'ctx': nrm((BATCH, CTX_LEN, d), 1.0),
        'c_ctx': nrm((d,), 1.0),
        'w_mod': nrm((DEPTH, d, 6 * d), 0.5 * d ** -0.5),
        'b_mod': nrm((DEPTH, 6 * d), 0.02),
        'norm_mix': 1.0 + nrm((DEPTH, d), 0.05),
        'norm_ffn': 1.0 + nrm((DEPTH, d), 0.05),
        'norm_final': 1.0 + nrm((d,), 0.05),
        'w_in_even': nrm((N_EVEN, d, EVEN_IN), d ** -0.5),
        'w_out_even': nrm((N_EVEN, RET_WIDTH + NA_WIDTH, d), (RET_WIDTH + NA_WIDTH) ** -0.5),
        'ret_decay': ret_base + nrm((N_EVEN, 2, RET_HEADS), 0.1),
        'na_rpb': nrm((N_EVEN, NA_HEADS, 2 * NA_ROWS - 1, 2 * NA_COLS - 1), 0.1),
        'w_in_odd': nrm((N_ODD, d, ODD_IN), d ** -0.5),
        'conv_w': nrm((N_ODD, SSD_CONV, SSD_CONV_DIM), SSD_CONV ** -0.5),
        'conv_b': nrm((N_ODD, SSD_CONV_DIM), 0.02),
        'dt_bias': dt0 + jnp.log(-jnp.expm1(-dt0)),
        'a_log': jnp.log(uni((N_ODD, 2, SSD_HEADS), 1.0, 16.0)),
        'd_skip': 1.0 + nrm((N_ODD, 2, SSD_HEADS), 0.1),
        'ssd_norm': 1.0 + nrm((N_ODD, SSD_INNER), 0.05),
        'w_out_odd': nrm((N_ODD, SSD_INNER, d), SSD_INNER ** -0.5),
        'w_router': nrm((d, N_EXPERTS), d ** -0.5),
        'router_bias': nrm((N_EXPERTS,), 0.01),
        'w_gate': nrm((DEPTH, N_EXPERTS, d, EXPERT_HIDDEN), d ** -0.5),
        'w_up': nrm((DEPTH, N_EXPERTS, d, EXPERT_HIDDEN), d ** -0.5),
        'w_down': nrm((DEPTH, N_EXPERTS, EXPERT_HIDDEN, d), EXPERT_HIDDEN ** -0.5),
    }


def reference(x, c, ctx, c_ctx, w_mod, b_mod, norm_mix, norm_ffn, norm_final, w_in_even, w_out_even,
              ret_decay, na_rpb, w_in_odd, conv_w, conv_b, dt_bias, a_log, d_skip, ssd_norm, w_out_odd,
              w_router, router_bias, w_gate, w_up, w_down):
    b_, l, d = x.shape
    mod_lat = jnp.einsum('bd,ide->ibe', jax.nn.silu(c), w_mod) + b_mod[:, None, :]
    mod_ctx = jnp.einsum('d,ide->ie', jax.nn.silu(c_ctx), w_mod) + b_mod
    lat, cx = x, ctx
    n_ctx_tok = cx.shape[0] * cx.shape[1]
    for i in range(DEPTH):
        last = i == DEPTH - 1
        sh1, sc1, g1, sh2, sc2, g2 = jnp.split(mod_lat[i][:, None, :], 6, axis=-1)
        csh1, csc1, cg1, csh2, csc2, cg2 = jnp.split(mod_ctx[i], 6, axis=-1)
        h_l = modulate(rmsnorm(lat, norm_mix[i]), sh1, sc1)
        h_c = modulate(rmsnorm(cx, norm_mix[i]), csh1, csc1)
        j = i // 2
        if i % 2 == 0:
            o_l, o_c = even_mixer(h_l, h_c, w_in_even[j], w_out_even[j], ret_decay[j], na_rpb[j], not last)
        else:
            o_l, o_c = odd_mixer(h_l, h_c, w_in_odd[j], conv_w[j], conv_b[j], dt_bias[j], a_log[j],
                                 d_skip[j], ssd_norm[j], w_out_odd[j], not last)
        lat = lat + g1 * o_l
        h_l = modulate(rmsnorm(lat, norm_ffn[i]), sh2, sc2).reshape(-1, d)
        if last:
            y_l = moe(h_l, w_router, router_bias, w_gate[i], w_up[i], w_down[i])
            lat = lat + g2 * y_l.reshape(b_, l, d)
        else:
            cx = cx + cg1 * o_c
            h_c = modulate(rmsnorm(cx, norm_ffn[i]), csh2, csc2).reshape(-1, d)
            y = moe(jnp.concatenate([h_c, h_l], axis=0), w_router, router_bias, w_gate[i], w_up[i], w_down[i])
            cx = cx + cg2 * y[:n_ctx_tok].reshape(cx.shape)
            lat = lat + g2 * y[n_ctx_tok:].reshape(b_, l, d)
    return rmsnorm(lat, norm_final)
```

```python
import functools
import math

import numpy as np
import jax
import jax.numpy as jnp
from jax import lax
from jax.experimental import pallas as pl
from jax.experimental.pallas import tpu as pltpu

GRID_W = 64
EPS = 1e-6
HEAD_DIM = 128
ROPE_BASE = 10000.0
RET_CHUNK = 128
NA_ROWS = 8
NA_COLS = 16
SSD_HEAD_DIM = 64
SSD_GROUPS = 8
SSD_STATE = 128
SSD_CONV = 5
SSD_CHUNK = 128
N_EXPERTS = 32
N_GROUPS = 8
EXPERTS_PER_GROUP = N_EXPERTS // N_GROUPS
TOP_K = 2
MOE_BLOCK = 256

LANES = 128
SUBLANES = 8
VMEM_LIMIT_BYTES = 56 * 1024 * 1024

NEG_INF = -1e30
F32 = jnp.float32
BF16 = jnp.bfloat16


def _cparams(*sem):
    return pltpu.CompilerParams(dimension_semantics=sem, vmem_limit_bytes=VMEM_LIMIT_BYTES)


def _silu(x):
    return x * (1.0 / (1.0 + jnp.exp(-x)))


def _mod_kernel(c_ref, w_ref, b_ref, o_ref):
    c = _silu(c_ref[...]).astype(BF16)
    o_ref[0] = jnp.dot(c, w_ref[0].astype(BF16), preferred_element_type=F32) + b_ref[0]


def _modulation(cvec, w_mod, b_mod):
    depth, d, n = w_mod.shape
    tn = 1024
    return pl.pallas_call(
        _mod_kernel,
        out_shape=jax.ShapeDtypeStruct((depth, SUBLANES, n), F32),
        grid=(depth, n // tn),
        in_specs=[
            pl.BlockSpec((SUBLANES, d), lambda i, j: (0, 0)),
            pl.BlockSpec((1, d, tn), lambda i, j: (i, 0, j)),
            pl.BlockSpec((1, 1, tn), lambda i, j: (i, 0, j)),
        ],
        out_specs=pl.BlockSpec((1, SUBLANES, tn), lambda i, j: (i, 0, j)),
        compiler_params=_cparams("parallel", "parallel"),
        name="adaln_modulation",
    )(cvec, w_mod, b_mod.reshape(depth, 1, n))


def _rms_modulate(x, nw, sh, sc):
    r = lax.rsqrt(jnp.mean(x * x, axis=-1, keepdims=True) + EPS)
    return (x * r * nw) * (1.0 + sc) + sh


def _rope(t, cos, sin):
    return t * cos + pltpu.roll(t, HEAD_DIM // 2, axis=1) * sin


def _inproj_kernel(*refs, combine, even, has_dt, tn):
    it = iter(refs)
    x_ref = next(it)
    if combine:
        y0_ref, y1_ref, g2_ref = next(it), next(it), next(it)
    nw_ref, sh_ref, sc_ref, w_ref = next(it), next(it), next(it), next(it)
    if even:
        cos_ref, sin_ref = next(it), next(it)
    if has_dt:
        wdt_ref = next(it)
    o_ref = next(it)
    if combine:
        lat_ref = next(it)
    if has_dt:
        dt_ref = next(it)
    h_scr = next(it)

    j = pl.program_id(1)

    @pl.when(j == 0)
    def _():
        x = x_ref[...]
        if combine:
            x = x + g2_ref[0] * (y0_ref[...] + y1_ref[...])
            lat_ref[...] = x
        h = _rms_modulate(x, nw_ref[...], sh_ref[0], sc_ref[0]).astype(BF16)
        h_scr[...] = h
        if has_dt:
            dt_ref[...] = jnp.dot(h, wdt_ref[...], preferred_element_type=F32)

    acc = jnp.dot(h_scr[...], w_ref[...], preferred_element_type=F32)

    if not even:
        o_ref[...] = acc.astype(o_ref.dtype)
        return

    heads_per_tile = tn // HEAD_DIM
    tiles_per_kilo = 1024 // tn
    seg = j // tiles_per_kilo
    scale = HEAD_DIM ** -0.5

    @pl.when(seg <= 1)
    def _():
        mul = jnp.where(seg == 1, scale, 1.0).astype(F32)
        cos, sin = cos_ref[...], sin_ref[...]
        for hh in range(heads_per_tile):
            sl = slice(hh * HEAD_DIM, (hh + 1) * HEAD_DIM)
            o_ref[:, sl] = (_rope(acc[:, sl], cos, sin) * mul).astype(o_ref.dtype)

    @pl.when(seg == 4)
    def _():
        o_ref[...] = (acc * scale).astype(o_ref.dtype)

    @pl.when((seg == 2) | (seg == 3) | (seg >= 5))
    def _():
        o_ref[...] = acc.astype(o_ref.dtype)


def _seg_spec(d, seg_rows, tm):
    return pl.BlockSpec((1, 1, d), lambda i, j: ((i * tm) // seg_rows, 0, 0))


def _inproj(x, nw, sh, sc, w, *, seg_rows, tm, tn, even=False, cos=None, sin=None,
            wdt=None, combine=None):
    t, d = x.shape
    n = w.shape[1]
    tm = min(tm, t)
    assert t % tm == 0 and n % tn == 0 and seg_rows % tm == 0
    args, in_specs = [x], [pl.BlockSpec((tm, d), lambda i, j: (i, 0))]
    if combine is not None:
        ys2, row_off, g2 = combine
        assert row_off % tm == 0
        ob = row_off // tm
        args += [ys2, ys2, g2]
        in_specs += [pl.BlockSpec((tm, d), lambda i, j: (i + ob, 0)),
                     pl.BlockSpec((tm, d), lambda i, j: (i + ob, 1)),
                     _seg_spec(d, seg_rows, tm)]
    args += [nw.reshape(1, d), sh, sc, w]
    in_specs += [pl.BlockSpec((1, d), lambda i, j: (0, 0)), _seg_spec(d, seg_rows, tm),
                 _seg_spec(d, seg_rows, tm), pl.BlockSpec((d, tn), lambda i, j: (0, j))]
    if even:
        lblk = cos.shape[0] // tm
        args += [cos, sin]
        in_specs += [pl.BlockSpec((tm, HEAD_DIM), lambda i, j: (i % lblk, 0))] * 2
    if wdt is not None:
        args.append(wdt)
        in_specs.append(pl.BlockSpec(wdt.shape, lambda i, j: (0, 0)))
    out_shape = [jax.ShapeDtypeStruct((t, n), BF16)]
    out_specs = [pl.BlockSpec((tm, tn), lambda i, j: (i, j))]
    if combine is not None:
        out_shape.append(jax.ShapeDtypeStruct((t, d), F32))
        out_specs.append(pl.BlockSpec((tm, d), lambda i, j: (i, 0)))
    if wdt is not None:
        out_shape.append(jax.ShapeDtypeStruct((t, wdt.shape[1]), F32))
        out_specs.append(pl.BlockSpec((tm, wdt.shape[1]), lambda i, j: (i, 0)))
    kern = functools.partial(_inproj_kernel, combine=combine is not None, even=even,
                             has_dt=wdt is not None, tn=tn)
    return pl.pallas_call(
        kern,
        out_shape=out_shape,
        grid=(t // tm, n // tn),
        in_specs=in_specs,
        out_specs=out_specs,
        scratch_shapes=[pltpu.VMEM((tm, d), BF16)],
        compiler_params=_cparams("parallel", "arbitrary"),
        name="inproj_even" if even else "inproj_odd",
    )(*args)


_TN = (((0,), (0,)), ((), ()))
_NT = (((1,), (1,)), ((), ()))


def _retention_kernel(q_ref, k_ref, v_ref, g_ref, tab_ref, sf0_ref, sb0_ref,
                      o_ref, sf_ref, sb_ref, sb_store, s_run, *, nchunk, nblk):
    p = pl.program_id(2)
    s = pl.program_id(3)
    q = RET_CHUNK
    dcomb = tab_ref[0, 0]

    @pl.when(p == 0)
    def _():
        @pl.when(s == 0)
        def _():
            s_run[...] = sb0_ref[0, 0]

        blk = nblk - 1 - s
        kb, cb = tab_ref[0, 4], tab_ref[0, 6]

        def body(ci, carry):
            c = nchunk - 1 - ci
            rows = pl.ds(pl.multiple_of(c * q, q), q)
            st = s_run[...]
            sb_store[blk * nchunk + c] = st.astype(BF16)
            kd = (k_ref[rows, :].astype(F32) * kb).astype(BF16)
            upd = lax.dot_general(kd, v_ref[rows, :], _TN, preferred_element_type=F32)
            s_run[...] = cb * st + upd
            return carry

        lax.fori_loop(0, nchunk, body, 0)

        @pl.when(s == nblk - 1)
        def _():
            sb_ref[0, 0] = s_run[...]

    @pl.when(p == 1)
    def _():
        @pl.when(s == 0)
        def _():
            s_run[...] = sf0_ref[0, 0]

        qf, qb, kf, cf = tab_ref[0, 1], tab_ref[0, 2], tab_ref[0, 3], tab_ref[0, 5]

        def body(c, carry):
            rows = pl.ds(pl.multiple_of(c * q, q), q)
            qc, kc, vc = q_ref[rows, :], k_ref[rows, :], v_ref[rows, :]
            qc32, kc32 = qc.astype(F32), kc.astype(F32)
            st = s_run[...]
            sc = lax.dot_general(qc, kc, _NT, preferred_element_type=F32) * dcomb
            o = jnp.dot(sc.astype(BF16), vc, preferred_element_type=F32)
            qcat = jnp.concatenate([(qc32 * qf).astype(BF16), (qc32 * qb).astype(BF16)], axis=1)
            scat = jnp.concatenate([st.astype(BF16), sb_store[s * nchunk + c]], axis=0)
            o = o + jnp.dot(qcat, scat, preferred_element_type=F32)
            upd = lax.dot_general((kc32 * kf).astype(BF16), vc, _TN, preferred_element_type=F32)
            s_run[...] = cf * st + upd
            o = o * lax.rsqrt(jnp.mean(o * o, axis=-1, keepdims=True) + EPS)
            o_ref[rows, :] = (o * _silu(g_ref[rows, :].astype(F32))).astype(o_ref.dtype)
            return carry

        lax.fori_loop(0, nchunk, body, 0)

        @pl.when(s == nblk - 1)
        def _():
            sf_ref[0, 0] = s_run[...]


def _retention_tables(ret_logit):
    q = RET_CHUNK
    lg = jax.nn.log_sigmoid(ret_logit.astype(F32))
    lf, lb = lg[0][:, None, None], lg[1][:, None, None]
    i = jnp.arange(q, dtype=F32)[None, :, None]
    j = jnp.arange(q, dtype=F32)[None, None, :]
    ones = jnp.ones((1, 1, q), F32)
    dcomb = (jnp.where(i >= j, jnp.exp(lf * jnp.maximum(i - j, 0.0)), 0.0)
             + jnp.where(j >= i, jnp.exp(lb * jnp.maximum(j - i, 0.0)), 0.0))
    qf = jnp.exp(lf * (i + 1.0)) * ones
    qb = jnp.exp(lb * (q - i)) * ones
    kf = jnp.exp(lf * (q - 1.0 - i)) * ones
    kb = jnp.exp(lb * i) * ones
    cf = jnp.exp(lf * q) * jnp.ones((1, q, q), F32)
    cb = jnp.exp(lb * q) * jnp.ones((1, q, q), F32)
    return jnp.stack([dcomb, qf, qb, kf, kb, cf, cb], axis=1)


def _retention(proj, tabs, sf0, sb0, *, batch, seq, rblk):
    heads = tabs.shape[0]
    rblk = min(rblk, seq)
    nblk = seq // rblk
    nchunk = rblk // RET_CHUNK
    hd = HEAD_DIM

    def qmap(off):
        return lambda b, h, p, s: (b * nblk + p * s, off + h)

    def kvmap(off):
        return lambda b, h, p, s: (b * nblk + s + (1 - p) * (nblk - 1 - 2 * s), off + h)

    st_spec = pl.BlockSpec((1, 1, hd, hd), lambda b, h, p, s: (b, h, 0, 0))
    kern = functools.partial(_retention_kernel, nchunk=nchunk, nblk=nblk)
    return pl.pallas_call(
        kern,
        out_shape=[jax.ShapeDtypeStruct((batch * seq, heads * hd), BF16),
                   jax.ShapeDtypeStruct((batch, heads, hd, hd), F32),
                   jax.ShapeDtypeStruct((batch, heads, hd, hd), F32)],
        grid=(batch, heads, 2, nblk),
        in_specs=[pl.BlockSpec((rblk, hd), qmap(0)),
                  pl.BlockSpec((rblk, hd), kvmap(heads)),
                  pl.BlockSpec((rblk, hd), kvmap(2 * heads)),
                  pl.BlockSpec((rblk, hd), qmap(3 * heads)),
                  pl.BlockSpec((1, 7, RET_CHUNK, RET_CHUNK), lambda b, h, p, s: (h, 0, 0, 0)),
                  st_spec, st_spec],
        out_specs=[pl.BlockSpec((rblk, hd), qmap(0)), st_spec, st_spec],
        scratch_shapes=[pltpu.VMEM((seq // RET_CHUNK, hd, hd), BF16), pltpu.VMEM((hd, hd), F32)],
        compiler_params=_cparams("parallel", "parallel", "arbitrary", "arbitrary"),
        name="retention",
    )(proj, proj, proj, proj, tabs, sf0, sb0)


NA_QROWS = 8
NA_WROWS = 16
NA_KBLK = 4


def _na_kernel(q_ref, k0, k1, k2, k3, v0, v1, v2, v3, kc_ref, vc_ref, bias_ref, o_ref):
    q = q_ref[...]
    kb = NA_KBLK * GRID_W
    scores = []
    for i, kr in enumerate((k0, k1, k2, k3)):
        sc = lax.dot_general(q, kr[...], _NT, preferred_element_type=F32)
        scores.append(sc + bias_ref[0, 0, :, i * kb:(i + 1) * kb])
    scores.append(lax.dot_general(q, kc_ref[...], _NT, preferred_element_type=F32))
    m = scores[0].max(axis=-1, keepdims=True)
    for sc in scores[1:]:
        m = jnp.maximum(m, sc.max(axis=-1, keepdims=True))
    acc = jnp.zeros((q.shape[0], HEAD_DIM), F32)
    l = jnp.zeros((q.shape[0], 1), F32)
    for sc, vr in zip(scores, (v0, v1, v2, v3, vc_ref)):
        pr = jnp.exp(sc - m)
        l = l + pr.sum(axis=-1, keepdims=True)
        acc = acc + jnp.dot(pr.astype(BF16), vr[...], preferred_element_type=F32)
    o_ref[...] = (acc / l).astype(o_ref.dtype)


def _na_bias_tables(rpb, rows):
    nrb = rows // NA_QROWS
    qr = np.arange(NA_QROWS)[:, None, None, None]
    qc = np.arange(GRID_W)[None, :, None, None]
    kr = np.arange(NA_WROWS)[None, None, :, None]
    kc = np.arange(GRID_W)[None, None, None, :]
    tabs = []
    for rb in (0, 1, nrb - 1):
        w0 = int(np.clip(rb * NA_QROWS - NA_ROWS // 2, 0, rows - NA_WROWS))
        r = rb * NA_QROWS + qr
        krow = w0 + kr
        r0 = np.clip(r - NA_ROWS // 2, 0, rows - NA_ROWS)
        c0 = np.clip(qc - NA_COLS // 2, 0, GRID_W - NA_COLS)
        valid = (krow >= r0) & (krow < r0 + NA_ROWS) & (kc >= c0) & (kc < c0 + NA_COLS)
        ri = np.clip(krow - r + NA_ROWS - 1, 0, 2 * NA_ROWS - 2) + 0 * kc + 0 * qc
        ci = np.clip(kc - qc + NA_COLS - 1, 0, 2 * NA_COLS - 2) + 0 * kr + 0 * qr
        valid = np.broadcast_to(valid, ri.shape).reshape(NA_QROWS * GRID_W, NA_WROWS * GRID_W)
        ri = ri.reshape(valid.shape)
        ci = ci.reshape(valid.shape)
        b = rpb.astype(F32)[:, ri, ci]
        tabs.append(jnp.where(valid[None], b, NEG_INF))
    return jnp.stack(tabs, axis=1)


def _neighbourhood_attention(proj, proj_ctx, bias, *, batch, seq, ctx_len):
    heads = bias.shape[0]
    hd = HEAD_DIM
    qblk = NA_QROWS * GRID_W
    kblk = NA_KBLK * GRID_W
    nrb = seq // qblk
    nkb = seq // kblk
    assert seq // GRID_W >= NA_WROWS + NA_QROWS
    qoff, koff, voff = 4 * heads, 5 * heads, 6 * heads

    def kmap(off, i):
        def f(b, h, r):
            u0 = jnp.clip(2 * r - 1, 0, nkb - NA_WROWS // NA_KBLK)
            return (b * nkb + u0 + i, off + h)
        return f

    def bias_map(b, h, r):
        return (h, (r > 0).astype(jnp.int32) + (r == nrb - 1).astype(jnp.int32), 0, 0)

    kspecs = [pl.BlockSpec((kblk, hd), kmap(koff, i)) for i in range(4)]
    vspecs = [pl.BlockSpec((kblk, hd), kmap(voff, i)) for i in range(4)]
    return pl.pallas_call(
        _na_kernel,
        out_shape=jax.ShapeDtypeStruct((batch * seq, heads * hd), BF16),
        grid=(batch, heads, nrb),
        in_specs=[pl.BlockSpec((qblk, hd), lambda b, h, r: (b * nrb + r, qoff + h))] + kspecs + vspecs + [
            pl.BlockSpec((ctx_len, hd), lambda b, h, r: (b, koff + h)),
            pl.BlockSpec((ctx_len, hd), lambda b, h, r: (b, voff + h)),
            pl.BlockSpec((1, 1, qblk, NA_WROWS * GRID_W), bias_map)],
        out_specs=pl.BlockSpec((qblk, hd), lambda b, h, r: (b * nrb + r, h)),
        compiler_params=_cparams("parallel", "parallel", "arbitrary"),
        name="neighbourhood_attention",
    )(*([proj] * 9), proj_ctx, proj_ctx, bias)


def _ctx_attn_kernel(q_ref, k_ref, v_ref, o_ref):
    sc = lax.dot_general(q_ref[...], k_ref[...], _NT, preferred_element_type=F32)
    pr = jnp.exp(sc - sc.max(axis=-1, keepdims=True))
    acc = jnp.dot(pr.astype(BF16), v_ref[...], preferred_element_type=F32)
    o_ref[...] = (acc / pr.sum(axis=-1, keepdims=True)).astype(o_ref.dtype)


def _ctx_attention(proj_ctx, *, batch, ctx_len, heads):
    hd = HEAD_DIM
    return pl.pallas_call(
        _ctx_attn_kernel,
        out_shape=jax.ShapeDtypeStruct((batch * ctx_len, heads * hd), BF16),
        grid=(batch, heads),
        in_specs=[pl.BlockSpec((ctx_len, hd), lambda b, h: (b, 4 * heads + h)),
                  pl.BlockSpec((ctx_len, hd), lambda b, h: (b, 5 * heads + h)),
                  pl.BlockSpec((ctx_len, hd), lambda b, h: (b, 6 * heads + h))],
        out_specs=pl.BlockSpec((ctx_len, hd), lambda b, h: (b, h)),
        compiler_params=_cparams("parallel", "parallel"),
        name="context_attention",
    )(proj_ctx, proj_ctx, proj_ctx)


def _split_bf16(x):
    hi = x.astype(BF16)
    return hi, (x - hi.astype(F32)).astype(BF16)


def _outproj_kernel(*refs, odd):
    it = iter(refs)
    if odd:
        y_ref, z_ref, nws_ref = next(it), next(it), next(it)
    else:
        mr_ref, mn_ref = next(it), next(it)
    w_ref, lat_ref, g1_ref, nwf_ref, sh_ref, sc_ref = (next(it) for _ in range(6))
    wrh_ref, wrl_ref = next(it), next(it)
    lat2_ref, h_ref, lg_ref = next(it), next(it), next(it)

    if odd:
        u = y_ref[...].astype(F32) * _silu(z_ref[...].astype(F32))
        u = u * lax.rsqrt(jnp.mean(u * u, axis=-1, keepdims=True) + EPS) * nws_ref[...]
        o = jnp.dot(u.astype(BF16), w_ref[...], preferred_element_type=F32)
    else:
        half = mr_ref.shape[1]
        o = (jnp.dot(mr_ref[...], w_ref[0:half, :], preferred_element_type=F32)
             + jnp.dot(mn_ref[...], w_ref[half:2 * half, :], preferred_element_type=F32))
    lat2 = lat_ref[...] + g1_ref[0] * o
    lat2_ref[...] = lat2
    h = _rms_modulate(lat2, nwf_ref[...], sh_ref[0], sc_ref[0])
    h_ref[...] = h
    hh, hl = _split_bf16(h)
    wrh = wrh_ref[...]
    lg_ref[...] = (jnp.dot(hh, wrh, preferred_element_type=F32)
                   + jnp.dot(hl, wrh, preferred_element_type=F32)
                   + jnp.dot(hh, wrl_ref[...], preferred_element_type=F32))


def _outproj(mix, w, lat, g1, nwf, sh, sc, wrh, wrl, *, seg_rows, tm, ssd=None):
    t, d = lat.shape
    tm = min(tm, t)
    assert t % tm == 0 and seg_rows % tm == 0
    odd = ssd is not None
    row = lambda i: (i, 0)
    seg = pl.BlockSpec((1, 1, d), lambda i: ((i * tm) // seg_rows, 0, 0))
    const = lambda shape: pl.BlockSpec(shape, lambda i: (0,) * len(shape))
    if odd:
        y, zx, nws = ssd
        kin = y.shape[1]
        args = [y, zx, nws.reshape(1, kin)]
        in_specs = [pl.BlockSpec((tm, kin), row), pl.BlockSpec((tm, kin), row), const((1, kin))]
    else:
        mr, mn = mix
        args = [mr, mn]
        in_specs = [pl.BlockSpec((tm, mr.shape[1]), row), pl.BlockSpec((tm, mn.shape[1]), row)]
    args += [w, lat, g1, nwf.reshape(1, d), sh, sc, wrh, wrl]
    in_specs += [pl.BlockSpec(w.shape, lambda i: (0, 0), pipeline_mode=pl.Buffered(1)),
                 pl.BlockSpec((tm, d), row), seg, const((1, d)), seg, seg,
                 const(wrh.shape), const(wrl.shape)]
    nl = wrh.shape[1]
    return pl.pallas_call(
        functools.partial(_outproj_kernel, odd=odd),
        out_shape=[jax.ShapeDtypeStruct((t, d), F32),
                   jax.ShapeDtypeStruct((t, d), F32),
                   jax.ShapeDtypeStruct((t, nl), F32)],
        grid=(t // tm,),
        in_specs=in_specs,
        out_specs=[pl.BlockSpec((tm, d), row), pl.BlockSpec((tm, d), row), pl.BlockSpec((tm, nl), row)],
        compiler_params=_cparams("parallel"),
        name="outproj_odd" if odd else "outproj_even",
    )(*args)


def _route(logits, router_bias):
    t = logits.shape[0]
    scores = jax.nn.sigmoid(logits[:, :N_EXPERTS])
    biased = (scores + router_bias.astype(F32)).reshape(t, N_GROUPS, EXPERTS_PER_GROUP)
    group_score = lax.top_k(biased, 2)[0].sum(-1)
    g_sel = jnp.argmax(group_score, axis=-1)
    in_group = jnp.take_along_axis(biased, g_sel[:, None, None], axis=1)[:, 0]
    _, top_local = lax.top_k(in_group, TOP_K)
    expert_idx = g_sel[:, None] * EXPERTS_PER_GROUP + top_local
    gate = jnp.take_along_axis(scores, expert_idx, axis=-1)
    gate = gate / jnp.sum(gate, axis=-1, keepdims=True)
    return expert_idx.astype(jnp.int32), gate


def _dispatch_plan(expert_idx, gate):
    t = expert_idx.shape[0]
    n_assign = t * TOP_K
    flat_e = expert_idx.reshape(-1)
    order = jnp.argsort(flat_e)
    sorted_e = flat_e[order]
    counts = jnp.bincount(flat_e, length=N_EXPERTS)
    padded = (counts + MOE_BLOCK - 1) // MOE_BLOCK * MOE_BLOCK
    pad_end = jnp.cumsum(padded)
    pad_start = pad_end - padded
    start = jnp.cumsum(counts) - counts
    dest = (pad_start[sorted_e] + jnp.arange(n_assign) - start[sorted_e]).astype(jnp.int32)
    n_blocks = -(-(n_assign + N_EXPERTS * (MOE_BLOCK - 1)) // MOE_BLOCK)
    n_slots = n_blocks * MOE_BLOCK
    slot_tok = jnp.zeros((n_slots,), jnp.int32).at[dest].set((order // TOP_K).astype(jnp.int32))
    slot_gate = jnp.zeros((n_slots,), F32).at[dest].set(gate.reshape(-1)[order])
    block_start = jnp.arange(n_blocks) * MOE_BLOCK
    block_expert = jnp.minimum(jnp.searchsorted(pad_end, block_start, side='right'),
                               N_EXPERTS - 1).astype(jnp.int32)
    pos = jnp.zeros((n_assign,), jnp.int32).at[order].set(dest)
    n_used = (pad_end[-1] // MOE_BLOCK).astype(jnp.int32).reshape(1)
    return slot_tok, slot_gate, block_expert, pos, n_used, n_blocks


def _row_gather(idx_ref, n, srcs, dst, sem):
    def row_copy(src, t, r):
        return pltpu.make_async_copy(src.at[pl.ds(t, 1), :], dst.at[pl.ds(r, 1), :], sem)

    def issue(r, c):
        t = idx_ref[0, 0, r]
        if len(srcs) == 1:
            row_copy(srcs[0], t, r).start()
        else:
            first_rows = srcs[0].shape[0]

            @pl.when(t < first_rows)
            def _():
                row_copy(srcs[0], t, r).start()

            @pl.when(t >= first_rows)
            def _():
                row_copy(srcs[1], t - first_rows, r).start()
        return c

    def wait(r, c):
        row_copy(srcs[-1], 0, r).wait()
        return c

    lax.fori_loop(0, n, issue, 0)
    lax.fori_loop(0, n, wait, 0)


def _moe_gather_kernel(*refs):
    tok_ref, srcs, (o_ref, buf, sem) = refs[0], refs[1:-3], refs[-3:]
    _row_gather(tok_ref, MOE_BLOCK, srcs, buf, sem)
    o_ref[...] = buf[...].astype(o_ref.dtype)


def _moe_gather(hs, slot_tok, n_blocks):
    d = hs[0].shape[1]
    return pl.pallas_call(
        _moe_gather_kernel,
        out_shape=jax.ShapeDtypeStruct((n_blocks * MOE_BLOCK, d), BF16),
        grid=(n_blocks,),
        in_specs=[pl.BlockSpec((1, 1, MOE_BLOCK), lambda i: (i, 0, 0), memory_space=pltpu.SMEM)]
        + [pl.BlockSpec(memory_space=pl.ANY)] * len(hs),
        out_specs=pl.BlockSpec((MOE_BLOCK, d), lambda i: (i, 0)),
        scratch_shapes=[pltpu.VMEM((MOE_BLOCK, d), F32), pltpu.SemaphoreType.DMA(())],
        compiler_params=_cparams("arbitrary"),
        name="moe_gather",
    )(slot_tok.reshape(n_blocks, 1, MOE_BLOCK), *hs)


def _expert_kernel(be_ref, nu_ref, x_ref, wg_ref, wu_ref, wd_ref, gate_ref, o_ref, wg_s, wu_s, wd_s):
    i = pl.program_id(0)
    used = i < nu_ref[0]
    fresh = (i == 0) | (be_ref[i] != be_ref[jnp.maximum(i - 1, 0)])

    @pl.when(used & fresh)
    def _():
        wg_s[...] = wg_ref[0].astype(BF16)
        wu_s[...] = wu_ref[0].astype(BF16)
        wd_s[...] = wd_ref[0].astype(BF16)

    @pl.when(used)
    def _():
        x = x_ref[...]
        a = jnp.dot(x, wg_s[...], preferred_element_type=F32)
        u = jnp.dot(x, wu_s[...], preferred_element_type=F32)
        hid = (_silu(a) * u).astype(BF16)
        y = jnp.dot(hid, wd_s[...], preferred_element_type=F32)
        o_ref[...] = y * gate_ref[...]

    @pl.when(jnp.logical_not(used))
    def _():
        o_ref[...] = jnp.zeros_like(o_ref)


def _expert_ffn(xs, slot_gate, block_expert, n_used, w_gate, w_up, w_down):
    n_slots, d = xs.shape
    n_blocks = n_slots // MOE_BLOCK
    hid = w_gate.shape[2]
    wmap = lambda i, be, nu: (be[i], 0, 0)
    return pl.pallas_call(
        _expert_kernel,
        out_shape=jax.ShapeDtypeStruct((n_slots, d), F32),
        grid_spec=pltpu.PrefetchScalarGridSpec(
            num_scalar_prefetch=2,
            grid=(n_blocks,),
            in_specs=[pl.BlockSpec((MOE_BLOCK, d), lambda i, be, nu: (i, 0)),
                      pl.BlockSpec((1, d, hid), wmap),
                      pl.BlockSpec((1, d, hid), wmap),
                      pl.BlockSpec((1, hid, d), wmap),
                      pl.BlockSpec((MOE_BLOCK, 1), lambda i, be, nu: (i, 0))],
            out_specs=pl.BlockSpec((MOE_BLOCK, d), lambda i, be, nu: (i, 0)),
            scratch_shapes=[pltpu.VMEM((d, hid), BF16), pltpu.VMEM((d, hid), BF16),
                            pltpu.VMEM((hid, d), BF16)]),
        compiler_params=_cparams("arbitrary"),
        name="moe_experts",
    )(block_expert, n_used, xs, w_gate, w_up, w_down, slot_gate.reshape(n_slots, 1))


def _combine_kernel(pos_ref, ys_hbm, lat_ref, g2_ref, nw_ref, o_ref, buf, sem, *, final, tc):
    _row_gather(pos_ref, TOP_K * tc, (ys_hbm,), buf, sem)
    x = lat_ref[...] + g2_ref[0] * (buf[0:tc, :] + buf[tc:2 * tc, :])
    if final:
        x = x * lax.rsqrt(jnp.mean(x * x, axis=-1, keepdims=True) + EPS) * nw_ref[...]
    o_ref[...] = x


def _moe_combine(ys, pos, lat, g2, nw, *, seg_rows, final):
    t, d = lat.shape
    tc = 128
    nt = t // tc
    kern = functools.partial(_combine_kernel, final=final, tc=tc)
    return pl.pallas_call(
        kern,
        out_shape=jax.ShapeDtypeStruct((t, d), F32),
        grid=(nt,),
        in_specs=[pl.BlockSpec((1, 1, TOP_K * tc), lambda i: (i, 0, 0), memory_space=pltpu.SMEM),
                  pl.BlockSpec(memory_space=pl.ANY),
                  pl.BlockSpec((tc, d), lambda i: (i, 0)),
                  pl.BlockSpec((1, 1, d), lambda i: ((i * tc) // seg_rows, 0, 0)),
                  pl.BlockSpec((1, d), lambda i: (0, 0))],
        out_specs=pl.BlockSpec((tc, d), lambda i: (i, 0)),
        scratch_shapes=[pltpu.VMEM((TOP_K * tc, d), F32), pltpu.SemaphoreType.DMA(())],
        compiler_params=_cparams("arbitrary"),
        name="moe_combine",
    )(pos.reshape(nt, tc, TOP_K).transpose(0, 2, 1).reshape(nt, 1, TOP_K * tc), ys, lat, g2,
      nw.reshape(1, d))


CONV_HALO = 16


def _conv_kernel(prev_ref, cur_ref, next_ref, w_ref, b_ref, o_ref, *, seq, rblk):
    i = pl.program_id(0)
    first = (i * rblk) % seq == 0
    last = ((i + 1) * rblk) % seq == 0
    half = CONV_HALO // 2
    pv = jnp.where(first, 0.0, prev_ref[...].astype(F32))[half:, :]
    nx = jnp.where(last, 0.0, next_ref[...].astype(F32))[:half, :]
    xm = jnp.concatenate([pv, cur_ref[...].astype(F32), nx], axis=0)
    n = rblk + 2 * half
    acc = jnp.zeros((rblk, xm.shape[1]), F32) + b_ref[...]
    for k in range(SSD_CONV):
        sh = (SSD_CONV // 2 - k) % n
        xs = xm if sh == 0 else pltpu.roll(xm, sh, axis=0)
        acc = acc + w_ref[k:k + 1, :] * xs[half:half + rblk, :]
    o_ref[...] = _silu(acc).astype(o_ref.dtype)


def _conv_silu(zx, conv_w, conv_b, *, seq, col_off):
    t = zx.shape[0]
    c = conv_w.shape[1]
    rblk, tn = 256, 512
    assert seq % rblk == 0 and c % tn == 0 and col_off % tn == 0
    cb = col_off // tn
    hb = rblk // CONV_HALO
    nhb = t // CONV_HALO
    kern = functools.partial(_conv_kernel, seq=seq, rblk=rblk)
    wpad = jnp.zeros((SUBLANES, c), F32).at[:SSD_CONV].set(conv_w.astype(F32))
    return pl.pallas_call(
        kern,
        out_shape=jax.ShapeDtypeStruct((t, c), BF16),
        grid=(t // rblk, c // tn),
        in_specs=[pl.BlockSpec((CONV_HALO, tn), lambda i, j: (jnp.maximum(i * hb - 1, 0), cb + j)),
                  pl.BlockSpec((rblk, tn), lambda i, j: (i, cb + j)),
                  pl.BlockSpec((CONV_HALO, tn), lambda i, j: (jnp.minimum((i + 1) * hb, nhb - 1), cb + j)),
                  pl.BlockSpec((SUBLANES, tn), lambda i, j: (0, j)),
                  pl.BlockSpec((1, tn), lambda i, j: (0, j))],
        out_specs=pl.BlockSpec((rblk, tn), lambda i, j: (i, j)),
        compiler_params=_cparams("parallel", "parallel"),
        name="ssd_conv_silu",
    )(zx, zx, zx, wpad, conv_b.astype(F32).reshape(1, c))


def _softplus(x):
    return jnp.maximum(x, 0.0) + jnp.log1p(jnp.exp(-jnp.abs(x)))


def _ssd_kernel(*refs, reverse, second, nchunk, nblk, groups):
    it = iter(refs)
    x_ref, b_ref, c_ref, dt_ref, prm_ref, s0_ref = (next(it) for _ in range(6))
    if second:
        yprev_ref, dsk_ref = next(it), next(it)
    y_ref, sfin_ref, s_scr = next(it), next(it), next(it)
    q = SSD_CHUNK
    nst = SSD_STATE
    gw = x_ref.shape[1] // groups
    pw = 2 * SSD_HEAD_DIM
    s = pl.program_id(1)

    @pl.when(s == 0)
    def _():
        s_scr[...] = s0_ref[0]

    ii = lax.broadcasted_iota(jnp.int32, (q, q), 0)
    jj = lax.broadcasted_iota(jnp.int32, (q, q), 1)
    tri = (jj >= ii) if reverse else (ii >= jj)
    tri_bf = tri.astype(BF16)
    lo_half = lax.broadcasted_iota(jnp.int32, (q, pw), 1) < SSD_HEAD_DIM
    a_neg = prm_ref[0:1, :]
    dtb = prm_ref[1:2, :]

    def chunk(ci, carry):
        c = (nchunk - 1 - ci) if reverse else ci
        rows = pl.ds(pl.multiple_of(c * q, q), q)
        dt = _softplus(dt_ref[rows, :] + dtb)
        da = dt * a_neg
        hi = da.astype(BF16)
        r1 = da - hi.astype(F32)
        mid = r1.astype(BF16)
        lo = (r1 - mid.astype(F32)).astype(BF16)
        acum = (jnp.dot(tri_bf, hi, preferred_element_type=F32)
                + jnp.dot(tri_bf, mid, preferred_element_type=F32)
                + jnp.dot(tri_bf, lo, preferred_element_type=F32))
        acum_t = acum.T
        dt_t = dt.T
        total = acum[0:1, :] if reverse else acum[q - 1:q, :]
        ea = jnp.exp(acum)
        fw = jnp.exp(total - acum) * dt
        etot = jnp.exp(total)
        for g in range(groups):
            bg = b_ref[rows, g * nst:(g + 1) * nst]
            cg = c_ref[rows, g * nst:(g + 1) * nst]
            cb = lax.dot_general(cg, bg, _NT, preferred_element_type=F32)
            sg = s_scr[g]
            yint = jnp.dot(cg, sg.astype(BF16), preferred_element_type=F32)
            ys, xws, decs = [], [], []
            for pr in range(gw // pw):
                h1 = g * (gw // SSD_HEAD_DIM) + 2 * pr
                h2 = h1 + 1
                ws = []
                for h in (h1, h2):
                    sgm = acum[:, h:h + 1] - acum_t[h:h + 1, :]
                    lm = jnp.where(tri, jnp.exp(sgm), 0.0)
                    ws.append((cb * lm * dt_t[h:h + 1, :]).astype(BF16))
                wp = jnp.concatenate(ws, axis=1)
                xp = x_ref[rows, g * gw + pr * pw:g * gw + (pr + 1) * pw].astype(F32)
                rhs = jnp.concatenate([jnp.where(lo_half, xp, 0.0), jnp.where(lo_half, 0.0, xp)],
                                      axis=0).astype(BF16)
                yp = jnp.dot(wp, rhs, preferred_element_type=F32)
                yp = yp + yint[:, pr * pw:(pr + 1) * pw] * jnp.where(lo_half, ea[:, h1:h1 + 1], ea[:, h2:h2 + 1])
                xws.append((xp * jnp.where(lo_half, fw[:, h1:h1 + 1], fw[:, h2:h2 + 1])).astype(BF16))
                decs.append(jnp.where(lo_half[0:1, :], etot[:, h1:h1 + 1], etot[:, h2:h2 + 1]))
                if second:
                    cols = slice(g * gw + pr * pw, g * gw + (pr + 1) * pw)
                    yp = yp + yprev_ref[rows, cols] + dsk_ref[:, cols] * xp
                ys.append(yp)
            xw = jnp.concatenate(xws, axis=1)
            upd = lax.dot_general(bg, xw, _TN, preferred_element_type=F32)
            s_scr[g] = jnp.concatenate(decs, axis=1) * sg + upd
            y_ref[rows, g * gw:(g + 1) * gw] = jnp.concatenate(ys, axis=1).astype(y_ref.dtype)
        return carry

    lax.fori_loop(0, nchunk, chunk, 0)

    @pl.when(s == nblk - 1)
    def _():
        sfin_ref[0] = s_scr[...]


def _ssd_pass(act, dt, prm, s0, *, batch, seq, reverse, yprev=None, dskip=None):
    groups, nst = SSD_GROUPS, SSD_STATE
    inner = act.shape[1] - 2 * groups * nst
    rblk = min(256, seq)
    nblk = seq // rblk
    second = yprev is not None
    bc = groups * nst
    rmap = (lambda b, s: (b * nblk + nblk - 1 - s, 0)) if reverse else (lambda b, s: (b * nblk + s, 0))
    cmap = lambda k: ((lambda b, s: (b * nblk + nblk - 1 - s, k)) if reverse
                      else (lambda b, s: (b * nblk + s, k)))
    st_spec = pl.BlockSpec((1, groups, nst, inner // groups), lambda b, s: (b, 0, 0, 0))
    args = [act, act, act, dt, prm, s0]
    in_specs = [pl.BlockSpec((rblk, inner), rmap),
                pl.BlockSpec((rblk, bc), cmap(inner // bc)),
                pl.BlockSpec((rblk, bc), cmap(inner // bc + 1)),
                pl.BlockSpec((rblk, dt.shape[1]), rmap),
                pl.BlockSpec(prm.shape, lambda b, s: (0, 0)),
                st_spec]
    if second:
        args += [yprev, dskip]
        in_specs += [pl.BlockSpec((rblk, inner), rmap), pl.BlockSpec((1, inner), lambda b, s: (0, 0))]
    kern = functools.partial(_ssd_kernel, reverse=reverse, second=second, nchunk=rblk // SSD_CHUNK,
                             nblk=nblk, groups=groups)
    return pl.pallas_call(
        kern,
        out_shape=[jax.ShapeDtypeStruct((batch * seq, inner), BF16 if second else F32),
                   jax.ShapeDtypeStruct((batch, groups, nst, inner // groups), F32)],
        grid=(batch, nblk),
        in_specs=in_specs,
        out_specs=[pl.BlockSpec((rblk, inner), rmap), st_spec],
        scratch_shapes=[pltpu.VMEM((groups, nst, inner // groups), F32)],
        compiler_params=_cparams("parallel", "arbitrary"),
        name="ssd_scan_bwd" if reverse else "ssd_scan_fwd",
    )(*args)


def _rope_tables(seq):
    pos = jnp.arange(seq)
    rows = (pos // GRID_W).astype(F32)
    cols = (pos % GRID_W).astype(F32)
    half = HEAD_DIM // 2
    inv = ROPE_BASE ** (-jnp.arange(0, half, 2, dtype=F32) / half)
    ang = jnp.concatenate([rows[:, None] * inv, cols[:, None] * inv], axis=-1)
    cos, sin = jnp.cos(ang), jnp.sin(ang)
    return jnp.concatenate([cos, cos], axis=-1), jnp.concatenate([-sin, sin], axis=-1)


def _deinterleave_perm(n_cols, n_rot_cols):
    within = np.concatenate([np.arange(0, HEAD_DIM, 2), np.arange(1, HEAD_DIM, 2)])
    perm = np.arange(n_cols)
    for h in range(n_rot_cols // HEAD_DIM):
        perm[h * HEAD_DIM:(h + 1) * HEAD_DIM] = h * HEAD_DIM + within
    return perm


def _moe_layer(hs, logits, router_bias, w_gate, w_up, w_down):
    expert_idx, gate = _route(logits, router_bias)
    slot_tok, slot_gate, block_expert, pos, n_used, n_blocks = _dispatch_plan(expert_idx, gate)
    xs = _moe_gather(hs, slot_tok, n_blocks)
    ys = _expert_ffn(xs, slot_gate, block_expert, n_used, w_gate, w_up, w_down)
    return ys, pos.reshape(-1, TOP_K)


def kernel(x, c, ctx, c_ctx, w_mod, b_mod, norm_mix, norm_ffn, norm_final, w_in_even, w_out_even,
           ret_decay, na_rpb, w_in_odd, conv_w, conv_b, dt_bias, a_log, d_skip, ssd_norm, w_out_odd,
           w_router, router_bias, w_gate, w_up, w_down):
    batch, seq, d = x.shape
    ctx_len = ctx.shape[1]
    depth = w_mod.shape[0]
    assert batch + 1 <= SUBLANES
    tl, tc = batch * seq, batch * ctx_len

    cvec = jnp.zeros((SUBLANES, d), F32).at[0].set(c_ctx).at[1:1 + batch].set(c)
    mod = _modulation(cvec, w_mod, b_mod)

    def mod_vecs(i, k):
        m = mod[i, :, k * d:(k + 1) * d]
        return m[1:1 + batch].reshape(batch, 1, d), m[0:1].reshape(1, 1, d)

    wr = jnp.zeros((d, LANES), F32).at[:, :N_EXPERTS].set(w_router.astype(F32))
    wrh, wrl = _split_bf16(wr)

    lat = x.reshape(tl, d)
    cx = ctx.reshape(tc, d)
    out = None
    for i in range(depth):
        last = i == depth - 1
        j = i // 2
        (sh1, csh1), (sc1, csc1), (g1, cg1) = mod_vecs(i, 0), mod_vecs(i, 1), mod_vecs(i, 2)
        (sh2, csh2), (sc2, csc2), (g2, cg2) = mod_vecs(i, 3), mod_vecs(i, 4), mod_vecs(i, 5)
        if i % 2 == 0:
            ret_heads = w_out_even.shape[1] // (2 * HEAD_DIM)
            rot_cols = 2 * ret_heads * HEAD_DIM
            w_in = w_in_even[j][:, _deinterleave_perm(w_in_even.shape[2], rot_cols)].astype(BF16)
            cos, sin = _rope_tables(seq)
            ones, zeros = jnp.ones((tc, HEAD_DIM), F32), jnp.zeros((tc, HEAD_DIM), F32)
            proj_l, = _inproj(lat, norm_mix[i], sh1, sc1, w_in, seg_rows=seq, tm=1024, tn=512,
                              even=True, cos=cos, sin=sin)
            proj_c, = _inproj(cx, norm_mix[i], csh1, csc1, w_in, seg_rows=tc, tm=tc, tn=512,
                              even=True, cos=ones, sin=zeros)
            tabs = _retention_tables(ret_decay[j])
            zero = jnp.zeros((batch, ret_heads, HEAD_DIM, HEAD_DIM), F32)
            ret_c, s_f, s_b = _retention(proj_c, tabs, zero, zero, batch=batch, seq=ctx_len, rblk=ctx_len)
            ret_l, _, _ = _retention(proj_l, tabs, s_f, s_b, batch=batch, seq=seq, rblk=1024)
            bias = _na_bias_tables(na_rpb[j], seq // GRID_W)
            na_l = _neighbourhood_attention(proj_l, proj_c, bias, batch=batch, seq=seq, ctx_len=ctx_len)
            w_out = w_out_even[j].astype(BF16)
            mix_l, ssd_l, mix_c, ssd_c = (ret_l, na_l), None, None, None
            if not last:
                mix_c = (ret_c, _ctx_attention(proj_c, batch=batch, ctx_len=ctx_len, heads=ret_heads))
            tm_out = 512
        else:
            inner = w_out_odd.shape[1]
            n_main = inner + conv_w.shape[2]
            w_in = w_in_odd[j]
            w_main = w_in[:, :n_main].astype(BF16)
            heads = w_in.shape[1] - n_main
            w_dt = jnp.zeros((d, LANES), F32).at[:, :heads].set(w_in[:, n_main:]).astype(BF16)
            zx_l, dt_l = _inproj(lat, norm_mix[i], sh1, sc1, w_main, seg_rows=seq, tm=1024, tn=512, wdt=w_dt)
            zx_c, dt_c = _inproj(cx, norm_mix[i], csh1, csc1, w_main, seg_rows=tc, tm=tc, tn=512, wdt=w_dt)
            act_l = _conv_silu(zx_l, conv_w[j], conv_b[j], seq=seq, col_off=inner)
            act_c = _conv_silu(zx_c, conv_w[j], conv_b[j], seq=ctx_len, col_off=inner)
            a_neg = -jnp.exp(a_log[j].astype(F32))

            def prm(dr):
                return (jnp.zeros((SUBLANES, LANES), F32).at[0, :heads].set(a_neg[dr])
                        .at[1, :heads].set(dt_bias[j, dr].astype(F32)))

            dsk = jnp.repeat((d_skip[j, 0] + d_skip[j, 1]).astype(F32), inner // heads)[None, :]
            zero = jnp.zeros((batch, SSD_GROUPS, SSD_STATE, inner // SSD_GROUPS), F32)
            yf_c, s_f = _ssd_pass(act_c, dt_c, prm(0), zero, batch=batch, seq=ctx_len, reverse=False)
            y_c, s_b = _ssd_pass(act_c, dt_c, prm(1), zero, batch=batch, seq=ctx_len, reverse=True,
                                 yprev=yf_c, dskip=dsk)
            yf_l, _ = _ssd_pass(act_l, dt_l, prm(0), s_f, batch=batch, seq=seq, reverse=False)
            y_l, _ = _ssd_pass(act_l, dt_l, prm(1), s_b, batch=batch, seq=seq, reverse=True,
                               yprev=yf_l, dskip=dsk)
            w_out = w_out_odd[j].astype(BF16)
            mix_l, ssd_l = None, (y_l, zx_l, ssd_norm[j].astype(F32))
            mix_c, ssd_c = None, (None if last else (y_c, zx_c, ssd_norm[j].astype(F32)))
            tm_out = 256

        nwf = norm_ffn[i].astype(F32)
        lat2, h_l, lg_l = _outproj(mix_l, w_out, lat, g1, nwf, sh2, sc2, wrh, wrl, seg_rows=seq,
                                   tm=tm_out, ssd=ssd_l)
        if last:
            ys, pos = _moe_layer([h_l], lg_l, router_bias, w_gate[i], w_up[i], w_down[i])
            out = _moe_combine(ys, pos, lat2, g2, norm_final.astype(F32), seg_rows=seq, final=True)
        else:
            cx2, h_c, lg_c = _outproj(mix_c, w_out, cx, cg1, nwf, csh2, csc2, wrh, wrl, seg_rows=tc,
                                      tm=tm_out, ssd=ssd_c)
            ys, pos = _moe_layer([h_c, h_l], jnp.concatenate([lg_c, lg_l], axis=0), router_bias,
                                 w_gate[i], w_up[i], w_down[i])
            ones = jnp.ones((d,), F32)
            cx = _moe_combine(ys, pos[:tc], cx2, cg2, ones, seg_rows=tc, final=False)
            lat = _moe_combine(ys, pos[tc:], lat2, g2, ones, seg_rows=seq, final=False)
    return out.reshape(batch, seq, d)
```

```python
import functools
import math

import numpy as np
import jax
import jax.numpy as jnp
from jax import lax
from jax.experimental import pallas as pl
from jax.experimental.pallas import tpu as pltpu

GRID_W = 64
EPS = 1e-6
HEAD_DIM = 128
ROPE_BASE = 10000.0
RET_CHUNK = 128
NA_ROWS = 8
NA_COLS = 16
SSD_HEAD_DIM = 64
SSD_GROUPS = 8
SSD_STATE = 128
SSD_CONV = 5
SSD_CHUNK = 128
N_EXPERTS = 32
N_GROUPS = 8
EXPERTS_PER_GROUP = N_EXPERTS // N_GROUPS
TOP_K = 2
MOE_BLOCK = 256

LANES = 128
SUBLANES = 8
VMEM_LIMIT_BYTES = 56 * 1024 * 1024

NEG_INF = -1e30
F32 = jnp.float32
BF16 = jnp.bfloat16


def _cparams(*sem):
    return pltpu.CompilerParams(dimension_semantics=sem, vmem_limit_bytes=VMEM_LIMIT_BYTES)


def _silu(x):
    return x * (1.0 / (1.0 + jnp.exp(-x)))


def _mod_kernel(c_ref, w_ref, b_ref, o_ref):
    c = _silu(c_ref[...]).astype(BF16)
    o_ref[0] = jnp.dot(c, w_ref[0].astype(BF16), preferred_element_type=F32) + b_ref[0]


def _modulation(cvec, w_mod, b_mod):
    depth, d, n = w_mod.shape
    tn = 1024
    return pl.pallas_call(
        _mod_kernel,
        out_shape=jax.ShapeDtypeStruct((depth, SUBLANES, n), F32),
        grid=(depth, n // tn),
        in_specs=[
            pl.BlockSpec((SUBLANES, d), lambda i, j: (0, 0)),
            pl.BlockSpec((1, d, tn), lambda i, j: (i, 0, j)),
            pl.BlockSpec((1, 1, tn), lambda i, j: (i, 0, j)),
        ],
        out_specs=pl.BlockSpec((1, SUBLANES, tn), lambda i, j: (i, 0, j)),
        compiler_params=_cparams("parallel", "parallel"),
        name="adaln_modulation",
    )(cvec, w_mod, b_mod.reshape(depth, 1, n))


def _rms_modulate(x, nw, sh, sc):
    r = lax.rsqrt(jnp.mean(x * x, axis=-1, keepdims=True) + EPS)
    return (x * r * nw) * (1.0 + sc) + sh


def _rope(t, cos, sin):
    return t * cos + pltpu.roll(t, HEAD_DIM // 2, axis=1) * sin


def _inproj_kernel(*refs, even, has_dt, tn):
    it = iter(refs)
    x_ref, nw_ref, sh_ref, sc_ref, w_ref = (next(it) for _ in range(5))
    if even:
        cos_ref, sin_ref = next(it), next(it)
    if has_dt:
        wdt_ref = next(it)
    o_ref = next(it)
    if has_dt:
        dt_ref = next(it)
    h_scr = next(it)

    j = pl.program_id(1)

    @pl.when(j == 0)
    def _():
        h = _rms_modulate(x_ref[...], nw_ref[...], sh_ref[0], sc_ref[0]).astype(BF16)
        h_scr[...] = h
        if has_dt:
            dt_ref[...] = jnp.dot(h, wdt_ref[...], preferred_element_type=F32)

    acc = jnp.dot(h_scr[...], w_ref[...], preferred_element_type=F32)

    if not even:
        o_ref[...] = acc.astype(o_ref.dtype)
        return

    heads_per_tile = tn // HEAD_DIM
    tiles_per_kilo = 1024 // tn
    seg = j // tiles_per_kilo
    scale = HEAD_DIM ** -0.5

    @pl.when(seg <= 1)
    def _():
        mul = jnp.where(seg == 1, scale, 1.0).astype(F32)
        cos, sin = cos_ref[...], sin_ref[...]
        for hh in range(heads_per_tile):
            sl = slice(hh * HEAD_DIM, (hh + 1) * HEAD_DIM)
            o_ref[:, sl] = (_rope(acc[:, sl], cos, sin) * mul).astype(o_ref.dtype)

    @pl.when(seg == 4)
    def _():
        o_ref[...] = (acc * scale).astype(o_ref.dtype)

    @pl.when((seg == 2) | (seg == 3) | (seg >= 5))
    def _():
        o_ref[...] = acc.astype(o_ref.dtype)


def _seg_spec(d, seg_rows, tm):
    return pl.BlockSpec((1, 1, d), lambda i, j: ((i * tm) // seg_rows, 0, 0))


def _inproj(x, nw, sh, sc, w, *, seg_rows, tm, tn, even=False, cos=None, sin=None, wdt=None):
    t, d = x.shape
    n = w.shape[1]
    tm = min(tm, t)
    assert t % tm == 0 and n % tn == 0 and seg_rows % tm == 0
    args, in_specs = [x], [pl.BlockSpec((tm, d), lambda i, j: (i, 0))]
    args += [nw.reshape(1, d), sh, sc, w]
    in_specs += [pl.BlockSpec((1, d), lambda i, j: (0, 0)), _seg_spec(d, seg_rows, tm),
                 _seg_spec(d, seg_rows, tm), pl.BlockSpec((d, tn), lambda i, j: (0, j))]
    if even:
        lblk = cos.shape[0] // tm
        args += [cos, sin]
        in_specs += [pl.BlockSpec((tm, HEAD_DIM), lambda i, j: (i % lblk, 0))] * 2
    if wdt is not None:
        args.append(wdt)
        in_specs.append(pl.BlockSpec(wdt.shape, lambda i, j: (0, 0)))
    out_shape = [jax.ShapeDtypeStruct((t, n), BF16)]
    out_specs = [pl.BlockSpec((tm, tn), lambda i, j: (i, j))]
    if wdt is not None:
        out_shape.append(jax.ShapeDtypeStruct((t, wdt.shape[1]), F32))
        out_specs.append(pl.BlockSpec((tm, wdt.shape[1]), lambda i, j: (i, 0)))
    kern = functools.partial(_inproj_kernel, even=even, has_dt=wdt is not None, tn=tn)
    return pl.pallas_call(
        kern,
        out_shape=out_shape,
        grid=(t // tm, n // tn),
        in_specs=in_specs,
        out_specs=out_specs,
        scratch_shapes=[pltpu.VMEM((tm, d), BF16)],
        compiler_params=_cparams("parallel", "arbitrary"),
        name="inproj_even" if even else "inproj_odd",
    )(*args)


_TN = (((0,), (0,)), ((), ()))
_NT = (((1,), (1,)), ((), ()))


def _retention_kernel(q_ref, k_ref, v_ref, g_ref, tab_ref, sf0_ref, sb0_ref,
                      o_ref, sf_ref, sb_ref, sb_store, s_run, *, nchunk, nblk):
    p = pl.program_id(2)
    s = pl.program_id(3)
    q = RET_CHUNK
    dcomb = tab_ref[0, 0]

    @pl.when(p == 0)
    def _():
        @pl.when(s == 0)
        def _():
            s_run[...] = sb0_ref[0, 0]

        blk = nblk - 1 - s
        kb, cb = tab_ref[0, 4], tab_ref[0, 6]

        def body(ci, carry):
            c = nchunk - 1 - ci
            rows = pl.ds(pl.multiple_of(c * q, q), q)
            st = s_run[...]
            sb_store[blk * nchunk + c] = st.astype(BF16)
            kd = (k_ref[rows, :].astype(F32) * kb).astype(BF16)
            upd = lax.dot_general(kd, v_ref[rows, :], _TN, preferred_element_type=F32)
            s_run[...] = cb * st + upd
            return carry

        lax.fori_loop(0, nchunk, body, 0)

        @pl.when(s == nblk - 1)
        def _():
            sb_ref[0, 0] = s_run[...]

    @pl.when(p == 1)
    def _():
        @pl.when(s == 0)
        def _():
            s_run[...] = sf0_ref[0, 0]

        qf, qb, kf, cf = tab_ref[0, 1], tab_ref[0, 2], tab_ref[0, 3], tab_ref[0, 5]

        def body(c, carry):
            rows = pl.ds(pl.multiple_of(c * q, q), q)
            qc, kc, vc = q_ref[rows, :], k_ref[rows, :], v_ref[rows, :]
            qc32, kc32 = qc.astype(F32), kc.astype(F32)
            st = s_run[...]
            sc = lax.dot_general(qc, kc, _NT, preferred_element_type=F32) * dcomb
            o = jnp.dot(sc.astype(BF16), vc, preferred_element_type=F32)
            qcat = jnp.concatenate([(qc32 * qf).astype(BF16), (qc32 * qb).astype(BF16)], axis=1)
            scat = jnp.concatenate([st.astype(BF16), sb_store[s * nchunk + c]], axis=0)
            o = o + jnp.dot(qcat, scat, preferred_element_type=F32)
            upd = lax.dot_general((kc32 * kf).astype(BF16), vc, _TN, preferred_element_type=F32)
            s_run[...] = cf * st + upd
            o = o * lax.rsqrt(jnp.mean(o * o, axis=-1, keepdims=True) + EPS)
            o_ref[rows, :] = (o * _silu(g_ref[rows, :].astype(F32))).astype(o_ref.dtype)
            return carry

        lax.fori_loop(0, nchunk, body, 0)

        @pl.when(s == nblk - 1)
        def _():
            sf_ref[0, 0] = s_run[...]


def _retention_tables(ret_logit):
    q = RET_CHUNK
    lg = jax.nn.log_sigmoid(ret_logit.astype(F32))
    lf, lb = lg[0][:, None, None], lg[1][:, None, None]
    i = jnp.arange(q, dtype=F32)[None, :, None]
    j = jnp.arange(q, dtype=F32)[None, None, :]
    ones = jnp.ones((1, 1, q), F32)
    dcomb = (jnp.where(i >= j, jnp.exp(lf * jnp.maximum(i - j, 0.0)), 0.0)
             + jnp.where(j >= i, jnp.exp(lb * jnp.maximum(j - i, 0.0)), 0.0))
    qf = jnp.exp(lf * (i + 1.0)) * ones
    qb = jnp.exp(lb * (q - i)) * ones
    kf = jnp.exp(lf * (q - 1.0 - i)) * ones
    kb = jnp.exp(lb * i) * ones
    cf = jnp.exp(lf * q) * jnp.ones((1, q, q), F32)
    cb = jnp.exp(lb * q) * jnp.ones((1, q, q), F32)
    return jnp.stack([dcomb, qf, qb, kf, kb, cf, cb], axis=1)


def _retention(proj, tabs, sf0, sb0, *, batch, seq, rblk):
    heads = tabs.shape[0]
    rblk = min(rblk, seq)
    nblk = seq // rblk
    nchunk = rblk // RET_CHUNK
    hd = HEAD_DIM

    def qmap(off):
        return lambda b, h, p, s: (b * nblk + p * s, off + h)

    def kvmap(off):
        return lambda b, h, p, s: (b * nblk + s + (1 - p) * (nblk - 1 - 2 * s), off + h)

    st_spec = pl.BlockSpec((1, 1, hd, hd), lambda b, h, p, s: (b, h, 0, 0))
    kern = functools.partial(_retention_kernel, nchunk=nchunk, nblk=nblk)
    return pl.pallas_call(
        kern,
        out_shape=[jax.ShapeDtypeStruct((batch * seq, heads * hd), BF16),
                   jax.ShapeDtypeStruct((batch, heads, hd, hd), F32),
                   jax.ShapeDtypeStruct((batch, heads, hd, hd), F32)],
        grid=(batch, heads, 2, nblk),
        in_specs=[pl.BlockSpec((rblk, hd), qmap(0)),
                  pl.BlockSpec((rblk, hd), kvmap(heads)),
                  pl.BlockSpec((rblk, hd), kvmap(2 * heads)),
                  pl.BlockSpec((rblk, hd), qmap(3 * heads)),
                  pl.BlockSpec((1, 7, RET_CHUNK, RET_CHUNK), lambda b, h, p, s: (h, 0, 0, 0)),
                  st_spec, st_spec],
        out_specs=[pl.BlockSpec((rblk, hd), qmap(0)), st_spec, st_spec],
        scratch_shapes=[pltpu.VMEM((seq // RET_CHUNK, hd, hd), BF16), pltpu.VMEM((hd, hd), F32)],
        compiler_params=_cparams("parallel", "parallel", "arbitrary", "arbitrary"),
        name="retention",
    )(proj, proj, proj, proj, tabs, sf0, sb0)


NA_QROWS = 8
NA_WROWS = 16
NA_KBLK = 4


def _na_kernel(q_ref, k0, k1, k2, k3, v0, v1, v2, v3, kc_ref, vc_ref, bias_ref, o_ref):
    q = q_ref[...]
    kb = NA_KBLK * GRID_W
    scores = []
    for i, kr in enumerate((k0, k1, k2, k3)):
        sc = lax.dot_general(q, kr[...], _NT, preferred_element_type=F32)
        scores.append(sc + bias_ref[0, 0, :, i * kb:(i + 1) * kb])
    scores.append(lax.dot_general(q, kc_ref[...], _NT, preferred_element_type=F32))
    m = scores[0].max(axis=-1, keepdims=True)
    for sc in scores[1:]:
        m = jnp.maximum(m, sc.max(axis=-1, keepdims=True))
    acc = jnp.zeros((q.shape[0], HEAD_DIM), F32)
    l = jnp.zeros((q.shape[0], 1), F32)
    for sc, vr in zip(scores, (v0, v1, v2, v3, vc_ref)):
        pr = jnp.exp(sc - m)
        l = l + pr.sum(axis=-1, keepdims=True)
        acc = acc + jnp.dot(pr.astype(BF16), vr[...], preferred_element_type=F32)
    o_ref[...] = (acc / l).astype(o_ref.dtype)


def _na_bias_tables(rpb, rows):
    nrb = rows // NA_QROWS
    nr, nc = 2 * NA_ROWS - 1, 2 * NA_COLS - 1
    hi = lax.Precision.HIGHEST
    qc = np.arange(GRID_W)[:, None]
    kc = np.arange(GRID_W)[None, :]
    c0 = np.clip(qc - NA_COLS // 2, 0, GRID_W - NA_COLS)
    col_valid = (kc >= c0) & (kc < c0 + NA_COLS)
    sel_c = (np.arange(nc)[:, None, None] == (kc - qc + NA_COLS - 1)[None]).astype(np.float32)
    toep = jnp.einsum('haj,jqk->haqk', rpb.astype(F32), sel_c, precision=hi)
    qr = np.arange(NA_QROWS)[:, None]
    kr = np.arange(NA_WROWS)[None, :]
    tabs = []
    for rb in (0, 1, nrb - 1):
        w0 = int(np.clip(rb * NA_QROWS - NA_ROWS // 2, 0, rows - NA_WROWS))
        r = rb * NA_QROWS + qr
        krow = w0 + kr
        r0 = np.clip(r - NA_ROWS // 2, 0, rows - NA_ROWS)
        row_valid = (krow >= r0) & (krow < r0 + NA_ROWS)
        sel_r = ((np.arange(nr)[None, None, :] == (krow - r + NA_ROWS - 1)[:, :, None])
                 & row_valid[:, :, None]).astype(np.float32)
        b = jnp.einsum('rka,haqc->hrqkc', sel_r, toep, precision=hi)
        valid = row_valid[:, None, :, None] & col_valid[None, :, None, :]
        b = jnp.where(valid[None], b, NEG_INF)
        tabs.append(b.reshape(rpb.shape[0], NA_QROWS * GRID_W, NA_WROWS * GRID_W))
    return jnp.stack(tabs, axis=1)


def _neighbourhood_attention(proj, proj_ctx, bias, *, batch, seq, ctx_len):
    heads = bias.shape[0]
    hd = HEAD_DIM
    qblk = NA_QROWS * GRID_W
    kblk = NA_KBLK * GRID_W
    nrb = seq // qblk
    nkb = seq // kblk
    assert seq // GRID_W >= NA_WROWS + NA_QROWS
    qoff, koff, voff = 4 * heads, 5 * heads, 6 * heads

    def kmap(off, i):
        def f(b, h, r):
            u0 = jnp.clip(2 * r - 1, 0, nkb - NA_WROWS // NA_KBLK)
            return (b * nkb + u0 + i, off + h)
        return f

    def bias_map(b, h, r):
        return (h, (r > 0).astype(jnp.int32) + (r == nrb - 1).astype(jnp.int32), 0, 0)

    kspecs = [pl.BlockSpec((kblk, hd), kmap(koff, i)) for i in range(4)]
    vspecs = [pl.BlockSpec((kblk, hd), kmap(voff, i)) for i in range(4)]
    return pl.pallas_call(
        _na_kernel,
        out_shape=jax.ShapeDtypeStruct((batch * seq, heads * hd), BF16),
        grid=(batch, heads, nrb),
        in_specs=[pl.BlockSpec((qblk, hd), lambda b, h, r: (b * nrb + r, qoff + h))] + kspecs + vspecs + [
            pl.BlockSpec((ctx_len, hd), lambda b, h, r: (b, koff + h)),
            pl.BlockSpec((ctx_len, hd), lambda b, h, r: (b, voff + h)),
            pl.BlockSpec((1, 1, qblk, NA_WROWS * GRID_W), bias_map)],
        out_specs=pl.BlockSpec((qblk, hd), lambda b, h, r: (b * nrb + r, h)),
        compiler_params=_cparams("parallel", "parallel", "arbitrary"),
        name="neighbourhood_attention",
    )(*([proj] * 9), proj_ctx, proj_ctx, bias)


def _ctx_attn_kernel(q_ref, k_ref, v_ref, o_ref):
    sc = lax.dot_general(q_ref[...], k_ref[...], _NT, preferred_element_type=F32)
    pr = jnp.exp(sc - sc.max(axis=-1, keepdims=True))
    acc = jnp.dot(pr.astype(BF16), v_ref[...], preferred_element_type=F32)
    o_ref[...] = (acc / pr.sum(axis=-1, keepdims=True)).astype(o_ref.dtype)


def _ctx_attention(proj_ctx, *, batch, ctx_len, heads):
    hd = HEAD_DIM
    return pl.pallas_call(
        _ctx_attn_kernel,
        out_shape=jax.ShapeDtypeStruct((batch * ctx_len, heads * hd), BF16),
        grid=(batch, heads),
        in_specs=[pl.BlockSpec((ctx_len, hd), lambda b, h: (b, 4 * heads + h)),
                  pl.BlockSpec((ctx_len, hd), lambda b, h: (b, 5 * heads + h)),
                  pl.BlockSpec((ctx_len, hd), lambda b, h: (b, 6 * heads + h))],
        out_specs=pl.BlockSpec((ctx_len, hd), lambda b, h: (b, h)),
        compiler_params=_cparams("parallel", "parallel"),
        name="context_attention",
    )(proj_ctx, proj_ctx, proj_ctx)


def _split_bf16(x):
    hi = x.astype(BF16)
    return hi, (x - hi.astype(F32)).astype(BF16)


def _outproj_kernel(*refs, odd):
    it = iter(refs)
    if odd:
        y_ref, z_ref, nws_ref = next(it), next(it), next(it)
    else:
        mr_ref, mn_ref = next(it), next(it)
    w_ref, lat_ref, g1_ref, nwf_ref, sh_ref, sc_ref = (next(it) for _ in range(6))
    wrh_ref, wrl_ref = next(it), next(it)
    lat2_ref, h_ref, lg_ref = next(it), next(it), next(it)

    if odd:
        u = y_ref[...].astype(F32) * _silu(z_ref[...].astype(F32))
        u = u * lax.rsqrt(jnp.mean(u * u, axis=-1, keepdims=True) + EPS) * nws_ref[...]
        o = jnp.dot(u.astype(BF16), w_ref[...], preferred_element_type=F32)
    else:
        half = mr_ref.shape[1]
        o = (jnp.dot(mr_ref[...], w_ref[0:half, :], preferred_element_type=F32)
             + jnp.dot(mn_ref[...], w_ref[half:2 * half, :], preferred_element_type=F32))
    lat2 = lat_ref[...] + g1_ref[0] * o
    lat2_ref[...] = lat2
    h = _rms_modulate(lat2, nwf_ref[...], sh_ref[0], sc_ref[0])
    h_ref[...] = h
    hh, hl = _split_bf16(h)
    wrh = wrh_ref[...]
    lg_ref[...] = (jnp.dot(hh, wrh, preferred_element_type=F32)
                   + jnp.dot(hl, wrh, preferred_element_type=F32)
                   + jnp.dot(hh, wrl_ref[...], preferred_element_type=F32))


def _outproj(mix, w, lat, g1, nwf, sh, sc, wrh, wrl, *, seg_rows, tm, ssd=None):
    t, d = lat.shape
    tm = min(tm, t)
    assert t % tm == 0 and seg_rows % tm == 0
    odd = ssd is not None
    row = lambda i: (i, 0)
    seg = pl.BlockSpec((1, 1, d), lambda i: ((i * tm) // seg_rows, 0, 0))
    const = lambda shape: pl.BlockSpec(shape, lambda i: (0,) * len(shape))
    if odd:
        y, zx, nws = ssd
        kin = y.shape[1]
        args = [y, zx, nws.reshape(1, kin)]
        in_specs = [pl.BlockSpec((tm, kin), row), pl.BlockSpec((tm, kin), row), const((1, kin))]
    else:
        mr, mn = mix
        args = [mr, mn]
        in_specs = [pl.BlockSpec((tm, mr.shape[1]), row), pl.BlockSpec((tm, mn.shape[1]), row)]
    args += [w, lat, g1, nwf.reshape(1, d), sh, sc, wrh, wrl]
    in_specs += [pl.BlockSpec(w.shape, lambda i: (0, 0), pipeline_mode=pl.Buffered(1)),
                 pl.BlockSpec((tm, d), row), seg, const((1, d)), seg, seg,
                 const(wrh.shape), const(wrl.shape)]
    nl = wrh.shape[1]
    return pl.pallas_call(
        functools.partial(_outproj_kernel, odd=odd),
        out_shape=[jax.ShapeDtypeStruct((t, d), F32),
                   jax.ShapeDtypeStruct((t, d), F32),
                   jax.ShapeDtypeStruct((t, nl), F32)],
        grid=(t // tm,),
        in_specs=in_specs,
        out_specs=[pl.BlockSpec((tm, d), row), pl.BlockSpec((tm, d), row), pl.BlockSpec((tm, nl), row)],
        compiler_params=_cparams("parallel"),
        name="outproj_odd" if odd else "outproj_even",
    )(*args)


ROUTER_TILE = 256
META_E, META_RANK, META_GATE = 0, 2, 4


def _router_kernel(lg_ref, rb_ref, meta_ref, cnt_ref, run_scr):
    tm = lg_ref.shape[0]
    i = pl.program_id(0)

    @pl.when(i == 0)
    def _():
        run_scr[...] = jnp.zeros_like(run_scr)

    big = 1e9
    lane = lax.broadcasted_iota(jnp.int32, (tm, LANES), 1)
    lanef = lane.astype(F32)
    scores = 1.0 / (1.0 + jnp.exp(-lg_ref[...]))
    biased = jnp.where(lane < N_EXPERTS, scores + rb_ref[...], NEG_INF)
    member = lane % EXPERTS_PER_GROUP
    best = jnp.full((tm, LANES), NEG_INF, F32)
    for k in range(1, EXPERTS_PER_GROUP):
        best = jnp.maximum(best, jnp.where(member >= k, biased + pltpu.roll(biased, k, axis=1), NEG_INF))
    grp = (lane // EXPERTS_PER_GROUP).astype(F32)
    top = best.max(axis=-1, keepdims=True)
    g_sel = jnp.where(best == top, grp, big).min(axis=-1, keepdims=True)
    v = jnp.where(grp == g_sel, biased, NEG_INF)
    i1 = jnp.where(v == v.max(axis=-1, keepdims=True), lanef, big).min(axis=-1, keepdims=True)
    v = jnp.where(lanef == i1, NEG_INF, v)
    i2 = jnp.where(v == v.max(axis=-1, keepdims=True), lanef, big).min(axis=-1, keepdims=True)
    oh1, oh2 = lanef == i1, lanef == i2
    s1 = jnp.where(oh1, scores, 0.0).sum(axis=-1, keepdims=True)
    s2 = jnp.where(oh2, scores, 0.0).sum(axis=-1, keepdims=True)
    den = s1 + s2
    oh = oh1.astype(F32) + oh2.astype(F32)
    earlier = (lax.broadcasted_iota(jnp.int32, (tm, tm), 0)
               > lax.broadcasted_iota(jnp.int32, (tm, tm), 1)).astype(BF16)
    base = jnp.dot(earlier, oh.astype(BF16), preferred_element_type=F32) + run_scr[0:1, :]
    r1 = jnp.where(oh1, base, 0.0).sum(axis=-1, keepdims=True)
    r2 = jnp.where(oh2, base, 0.0).sum(axis=-1, keepdims=True)
    total = run_scr[0:1, :] + oh.sum(axis=0, keepdims=True)
    run_scr[0:1, :] = total
    cnt_ref[...] = jnp.broadcast_to(total, cnt_ref.shape)
    meta = jnp.zeros((tm, LANES), F32)
    for ln, val in ((META_E, i1), (META_E + 1, i2), (META_RANK, r1), (META_RANK + 1, r2),
                    (META_GATE, s1 / den), (META_GATE + 1, s2 / den)):
        meta = jnp.where(lane == ln, val, meta)
    meta_ref[...] = meta


def _router(logits, router_bias):
    t = logits.shape[0]
    tm = ROUTER_TILE
    rb = jnp.zeros((1, LANES), F32).at[0, :N_EXPERTS].set(router_bias.astype(F32))
    return pl.pallas_call(
        _router_kernel,
        out_shape=[jax.ShapeDtypeStruct((t, LANES), F32), jax.ShapeDtypeStruct((SUBLANES, LANES), F32)],
        grid=(t // tm,),
        in_specs=[pl.BlockSpec((tm, LANES), lambda i: (i, 0)), pl.BlockSpec((1, LANES), lambda i: (0, 0))],
        out_specs=[pl.BlockSpec((tm, LANES), lambda i: (i, 0)),
                   pl.BlockSpec((SUBLANES, LANES), lambda i: (0, 0))],
        scratch_shapes=[pltpu.VMEM((SUBLANES, LANES), F32)],
        compiler_params=_cparams("arbitrary"),
        name="moe_router",
    )(logits, rb)


def _moe_plan(meta, cnt):
    t = meta.shape[0]
    blk = MOE_BLOCK
    counts = cnt[0, :N_EXPERTS].astype(jnp.int32)
    ends = jnp.cumsum(counts)
    offsets = ends - counts
    e = meta[:, META_E:META_E + TOP_K].astype(jnp.int32)
    rank = meta[:, META_RANK:META_RANK + TOP_K].astype(jnp.int32)
    onehot = e[..., None] == jnp.arange(N_EXPERTS, dtype=jnp.int32)
    dest = jnp.sum(jnp.where(onehot, offsets, 0), axis=-1) + rank
    n_tiles = t * TOP_K // blk
    tile_start = jnp.arange(n_tiles, dtype=jnp.int32) * blk
    first_e = jnp.searchsorted(ends, tile_start, side='right').astype(jnp.int32)
    last_e = jnp.searchsorted(ends, tile_start + blk - 1, side='right').astype(jnp.int32)
    nsteps = last_e - first_e + 1
    step_end = jnp.cumsum(nsteps)
    step_start = step_end - nsteps
    s = jnp.arange(n_tiles + N_EXPERTS - 1, dtype=jnp.int32)
    valid = s < step_end[-1]
    tile_of = jnp.minimum(jnp.searchsorted(step_end, s, side='right'), n_tiles - 1).astype(jnp.int32)
    exp_of = jnp.where(valid, first_e[tile_of] + s - step_start[tile_of], last_e[-1])
    exp_of = jnp.clip(exp_of, 0, N_EXPERTS - 1).astype(jnp.int32)
    lo = jnp.clip(offsets[exp_of] - tile_of * blk, 0, blk)
    hi = jnp.clip(ends[exp_of] - tile_of * blk, 0, blk)
    lo = jnp.where(valid, lo, 0).astype(jnp.int32)
    hi = jnp.where(valid, hi, 0).astype(jnp.int32)
    first = (valid & (s == step_start[tile_of])).astype(jnp.int32)
    return dest.astype(jnp.int32), (tile_of, exp_of, lo, hi, first)


DMA_UNROLL = 8


def _dispatch_kernel(*refs, tm, src_tiles):
    dest_ref, srcs, xs_hbm, sem = refs[0], refs[1:-2], refs[-2], refs[-1]
    i = pl.program_id(0)
    nt = pl.num_programs(0)

    def issue_from(src, tile):
        def body(r, c):
            for k in range(TOP_K):
                pltpu.make_async_copy(src.at[pl.ds(tile * tm + r, 1), :],
                                      xs_hbm.at[pl.ds(dest_ref[0, 0, TOP_K * r + k], 1), :], sem).start()
            return c
        lax.fori_loop(0, tm, body, 0, unroll=DMA_UNROLL)

    first_tile = 0
    for src, ntile in zip(srcs, src_tiles):
        @pl.when((i >= first_tile) & (i < first_tile + ntile))
        def _(src=src, first_tile=first_tile):
            issue_from(src, i - first_tile)
        first_tile += ntile

    def wait_step():
        n = TOP_K * tm
        pltpu.make_async_copy(xs_hbm.at[pl.ds(0, n), :], xs_hbm.at[pl.ds(0, n), :], sem).wait()

    @pl.when(i > 0)
    def _():
        wait_step()

    @pl.when(i == nt - 1)
    def _():
        wait_step()


def _moe_dispatch(hs, dest):
    d = hs[0].shape[1]
    t = dest.shape[0]
    tm = 256
    nt = t // tm
    src_tiles = tuple(h.shape[0] // tm for h in hs)
    assert sum(src_tiles) == nt and all(h.shape[0] % tm == 0 for h in hs)
    kern = functools.partial(_dispatch_kernel, tm=tm, src_tiles=src_tiles)
    return pl.pallas_call(
        kern,
        out_shape=jax.ShapeDtypeStruct((TOP_K * t, d), F32),
        grid=(nt,),
        in_specs=[pl.BlockSpec((1, 1, TOP_K * tm), lambda i: (i, 0, 0), memory_space=pltpu.SMEM)]
        + [pl.BlockSpec(memory_space=pl.ANY)] * len(hs),
        out_specs=pl.BlockSpec(memory_space=pl.ANY),
        scratch_shapes=[pltpu.SemaphoreType.DMA(())],
        compiler_params=_cparams("arbitrary"),
        name="moe_dispatch",
    )(dest.reshape(nt, 1, TOP_K * tm), *hs)


def _expert_kernel(tile_ref, exp_ref, lo_ref, hi_ref, first_ref, x_ref, wg_ref, wu_ref, wd_ref,
                   o_ref, wg_s, wu_s, wd_s, cur_e):
    s = pl.program_id(0)
    lo, hi, e = lo_ref[s], hi_ref[s], exp_ref[s]

    @pl.when(s == 0)
    def _():
        cur_e[0] = -1

    @pl.when(hi > lo)
    def _():
        @pl.when(cur_e[0] != e)
        def _():
            wg_s[...] = wg_ref[0].astype(BF16)
            wu_s[...] = wu_ref[0].astype(BF16)
            wd_s[...] = wd_ref[0].astype(BF16)
            cur_e[0] = e

        x = x_ref[...].astype(BF16)
        a = jnp.dot(x, wg_s[...], preferred_element_type=F32)
        u = jnp.dot(x, wu_s[...], preferred_element_type=F32)
        hid = (_silu(a) * u).astype(BF16)
        y = jnp.dot(hid, wd_s[...], preferred_element_type=F32)
        row = lax.broadcasted_iota(jnp.int32, (x.shape[0], 1), 0)
        mine = (row >= lo) & (row < hi)

        @pl.when(first_ref[s] == 1)
        def _():
            o_ref[...] = jnp.where(mine, y, 0.0)

        @pl.when(first_ref[s] == 0)
        def _():
            o_ref[...] = jnp.where(mine, y, o_ref[...])


def _expert_ffn(xs, plan, w_gate, w_up, w_down):
    n_rows, d = xs.shape
    hid = w_gate.shape[2]
    n_steps = plan[0].shape[0]
    xmap = lambda s, tile, exp, lo, hi, first: (tile[s], 0)
    wmap = lambda s, tile, exp, lo, hi, first: (exp[s], 0, 0)
    return pl.pallas_call(
        _expert_kernel,
        out_shape=jax.ShapeDtypeStruct((n_rows, d), F32),
        grid_spec=pltpu.PrefetchScalarGridSpec(
            num_scalar_prefetch=5,
            grid=(n_steps,),
            in_specs=[pl.BlockSpec((MOE_BLOCK, d), xmap),
                      pl.BlockSpec((1, d, hid), wmap),
                      pl.BlockSpec((1, d, hid), wmap),
                      pl.BlockSpec((1, hid, d), wmap)],
            out_specs=pl.BlockSpec((MOE_BLOCK, d), xmap),
            scratch_shapes=[pltpu.VMEM((d, hid), BF16), pltpu.VMEM((d, hid), BF16),
                            pltpu.VMEM((hid, d), BF16), pltpu.SMEM((1,), jnp.int32)]),
        compiler_params=_cparams("arbitrary"),
        name="moe_experts",
    )(*plan, xs, w_gate, w_up, w_down)


def _combine_kernel(pos_ref, posn_ref, ys_hbm, meta_ref, lat_ref, g2_ref, nw_ref, o_ref, buf, sem,
                    *, final, tc):
    i = pl.program_id(0)
    nt = pl.num_programs(0)
    slot = i % 2
    n = TOP_K * tc

    def issue(idx_ref, sl):
        def body(r, c):
            pltpu.make_async_copy(ys_hbm.at[pl.ds(idx_ref[0, 0, r], 1), :],
                                  buf.at[sl, pl.ds(r, 1), :], sem.at[sl]).start()
            return c
        lax.fori_loop(0, n, body, 0, unroll=DMA_UNROLL)

    @pl.when(i == 0)
    def _():
        issue(pos_ref, 0)

    @pl.when(i + 1 < nt)
    def _():
        issue(posn_ref, 1 - slot)

    pltpu.make_async_copy(ys_hbm.at[pl.ds(0, n), :], buf.at[slot], sem.at[slot]).wait()
    gk = meta_ref[:, META_GATE:META_GATE + TOP_K]
    y = gk[:, 0:1] * buf[slot, 0:tc, :] + gk[:, 1:2] * buf[slot, tc:2 * tc, :]
    x = lat_ref[...] + g2_ref[0] * y
    if final:
        x = x * lax.rsqrt(jnp.mean(x * x, axis=-1, keepdims=True) + EPS) * nw_ref[...]
    o_ref[...] = x


def _moe_combine(ys, pos, meta, lat, g2, nw, *, seg_rows, final):
    t, d = lat.shape
    tc = 128
    nt = t // tc
    n = TOP_K * tc
    kern = functools.partial(_combine_kernel, final=final, tc=tc)
    posb = pos.reshape(nt, tc, TOP_K).transpose(0, 2, 1).reshape(nt, 1, n)
    return pl.pallas_call(
        kern,
        out_shape=jax.ShapeDtypeStruct((t, d), F32),
        grid=(nt,),
        in_specs=[pl.BlockSpec((1, 1, n), lambda i: (i, 0, 0), memory_space=pltpu.SMEM),
                  pl.BlockSpec((1, 1, n), lambda i: (jnp.minimum(i + 1, nt - 1), 0, 0),
                               memory_space=pltpu.SMEM),
                  pl.BlockSpec(memory_space=pl.ANY),
                  pl.BlockSpec((tc, LANES), lambda i: (i, 0)),
                  pl.BlockSpec((tc, d), lambda i: (i, 0)),
                  pl.BlockSpec((1, 1, d), lambda i: ((i * tc) // seg_rows, 0, 0)),
                  pl.BlockSpec((1, d), lambda i: (0, 0))],
        out_specs=pl.BlockSpec((tc, d), lambda i: (i, 0)),
        scratch_shapes=[pltpu.VMEM((2, n, d), F32), pltpu.SemaphoreType.DMA((2,))],
        compiler_params=_cparams("arbitrary"),
        name="moe_combine",
    )(posb, posb, ys, meta, lat, g2, nw.reshape(1, d))


CONV_HALO = 16


def _conv_kernel(prev_ref, cur_ref, next_ref, w_ref, b_ref, o_ref, *, seq, rblk):
    i = pl.program_id(0)
    first = (i * rblk) % seq == 0
    last = ((i + 1) * rblk) % seq == 0
    half = CONV_HALO // 2
    pv = jnp.where(first, 0.0, prev_ref[...].astype(F32))[half:, :]
    nx = jnp.where(last, 0.0, next_ref[...].astype(F32))[:half, :]
    xm = jnp.concatenate([pv, cur_ref[...].astype(F32), nx], axis=0)
    n = rblk + 2 * half
    acc = jnp.zeros((rblk, xm.shape[1]), F32) + b_ref[...]
    for k in range(SSD_CONV):
        sh = (SSD_CONV // 2 - k) % n
        xs = xm if sh == 0 else pltpu.roll(xm, sh, axis=0)
        acc = acc + w_ref[k:k + 1, :] * xs[half:half + rblk, :]
    o_ref[...] = _silu(acc).astype(o_ref.dtype)


def _conv_silu(zx, conv_w, conv_b, *, seq, col_off):
    t = zx.shape[0]
    c = conv_w.shape[1]
    rblk, tn = min(512, seq), 1024
    assert seq % rblk == 0 and c % tn == 0 and col_off % tn == 0
    cb = col_off // tn
    hb = rblk // CONV_HALO
    nhb = t // CONV_HALO
    kern = functools.partial(_conv_kernel, seq=seq, rblk=rblk)
    wpad = jnp.zeros((SUBLANES, c), F32).at[:SSD_CONV].set(conv_w.astype(F32))
    return pl.pallas_call(
        kern,
        out_shape=jax.ShapeDtypeStruct((t, c), BF16),
        grid=(t // rblk, c // tn),
        in_specs=[pl.BlockSpec((CONV_HALO, tn), lambda i, j: (jnp.maximum(i * hb - 1, 0), cb + j)),
                  pl.BlockSpec((rblk, tn), lambda i, j: (i, cb + j)),
                  pl.BlockSpec((CONV_HALO, tn), lambda i, j: (jnp.minimum((i + 1) * hb, nhb - 1), cb + j)),
                  pl.BlockSpec((SUBLANES, tn), lambda i, j: (0, j)),
                  pl.BlockSpec((1, tn), lambda i, j: (0, j))],
        out_specs=pl.BlockSpec((rblk, tn), lambda i, j: (i, j)),
        compiler_params=_cparams("parallel", "parallel"),
        name="ssd_conv_silu",
    )(zx, zx, zx, wpad, conv_b.astype(F32).reshape(1, c))


def _softplus(x):
    return jnp.maximum(x, 0.0) + jnp.log1p(jnp.exp(-jnp.abs(x)))


def _ssd_kernel(*refs, reverse, second, nchunk, nblk, groups):
    it = iter(refs)
    x_ref, b_ref, c_ref, dt_ref, prm_ref, s0_ref = (next(it) for _ in range(6))
    if second:
        yprev_ref, dsk_ref = next(it), next(it)
    y_ref, sfin_ref, s_scr = next(it), next(it), next(it)
    q = SSD_CHUNK
    nst = SSD_STATE
    gw = x_ref.shape[1] // groups
    pw = 2 * SSD_HEAD_DIM
    s = pl.program_id(1)

    @pl.when(s == 0)
    def _():
        s_scr[...] = s0_ref[0]

    ii = lax.broadcasted_iota(jnp.int32, (q, q), 0)
    jj = lax.broadcasted_iota(jnp.int32, (q, q), 1)
    tri = (jj >= ii) if reverse else (ii >= jj)
    tri_bf = tri.astype(BF16)
    lo_half = lax.broadcasted_iota(jnp.int32, (q, pw), 1) < SSD_HEAD_DIM
    a_neg = prm_ref[0:1, :]
    dtb = prm_ref[1:2, :]

    def chunk(ci, carry):
        c = (nchunk - 1 - ci) if reverse else ci
        rows = pl.ds(pl.multiple_of(c * q, q), q)
        dt = _softplus(dt_ref[rows, :] + dtb)
        da = dt * a_neg
        hi = da.astype(BF16)
        r1 = da - hi.astype(F32)
        mid = r1.astype(BF16)
        lo = (r1 - mid.astype(F32)).astype(BF16)
        acum = (jnp.dot(tri_bf, hi, preferred_element_type=F32)
                + jnp.dot(tri_bf, mid, preferred_element_type=F32)
                + jnp.dot(tri_bf, lo, preferred_element_type=F32))
        acum_t = acum.T
        dt_t = dt.T
        total = acum[0:1, :] if reverse else acum[q - 1:q, :]
        ea = jnp.exp(acum)
        fw = jnp.exp(total - acum) * dt
        etot = jnp.exp(total)
        for g in range(groups):
            bg = b_ref[rows, g * nst:(g + 1) * nst]
            cg = c_ref[rows, g * nst:(g + 1) * nst]
            cb = lax.dot_general(cg, bg, _NT, preferred_element_type=F32)
            sg = s_scr[g]
            yint = jnp.dot(cg, sg.astype(BF16), preferred_element_type=F32)
            ys, xws, decs = [], [], []
            for pr in range(gw // pw):
                h1 = g * (gw // SSD_HEAD_DIM) + 2 * pr
                h2 = h1 + 1
                ws = []
                for h in (h1, h2):
                    sgm = acum[:, h:h + 1] - acum_t[h:h + 1, :]
                    lm = jnp.where(tri, jnp.exp(sgm), 0.0)
                    ws.append((cb * lm * dt_t[h:h + 1, :]).astype(BF16))
                wp = jnp.concatenate(ws, axis=1)
                xp = x_ref[rows, g * gw + pr * pw:g * gw + (pr + 1) * pw].astype(F32)
                rhs = jnp.concatenate([jnp.where(lo_half, xp, 0.0), jnp.where(lo_half, 0.0, xp)],
                                      axis=0).astype(BF16)
                yp = jnp.dot(wp, rhs, preferred_element_type=F32)
                yp = yp + yint[:, pr * pw:(pr + 1) * pw] * jnp.where(lo_half, ea[:, h1:h1 + 1], ea[:, h2:h2 + 1])
                xws.append((xp * jnp.where(lo_half, fw[:, h1:h1 + 1], fw[:, h2:h2 + 1])).astype(BF16))
                decs.append(jnp.where(lo_half[0:1, :], etot[:, h1:h1 + 1], etot[:, h2:h2 + 1]))
                if second:
                    cols = slice(g * gw + pr * pw, g * gw + (pr + 1) * pw)
                    yp = yp + yprev_ref[rows, cols] + dsk_ref[:, cols] * xp
                ys.append(yp)
            xw = jnp.concatenate(xws, axis=1)
            upd = lax.dot_general(bg, xw, _TN, preferred_element_type=F32)
            s_scr[g] = jnp.concatenate(decs, axis=1) * sg + upd
            y_ref[rows, g * gw:(g + 1) * gw] = jnp.concatenate(ys, axis=1).astype(y_ref.dtype)
        return carry

    lax.fori_loop(0, nchunk, chunk, 0)

    @pl.when(s == nblk - 1)
    def _():
        sfin_ref[0] = s_scr[...]


def _ssd_pass(act, dt, prm, s0, *, batch, seq, reverse, yprev=None, dskip=None):
    groups, nst = SSD_GROUPS, SSD_STATE
    inner = act.shape[1] - 2 * groups * nst
    rblk = min(256, seq)
    nblk = seq // rblk
    second = yprev is not None
    bc = groups * nst
    rmap = (lambda b, s: (b * nblk + nblk - 1 - s, 0)) if reverse else (lambda b, s: (b * nblk + s, 0))
    cmap = lambda k: ((lambda b, s: (b * nblk + nblk - 1 - s, k)) if reverse
                      else (lambda b, s: (b * nblk + s, k)))
    st_spec = pl.BlockSpec((1, groups, nst, inner // groups), lambda b, s: (b, 0, 0, 0))
    args = [act, act, act, dt, prm, s0]
    in_specs = [pl.BlockSpec((rblk, inner), rmap),
                pl.BlockSpec((rblk, bc), cmap(inner // bc)),
                pl.BlockSpec((rblk, bc), cmap(inner // bc + 1)),
                pl.BlockSpec((rblk, dt.shape[1]), rmap),
                pl.BlockSpec(prm.shape, lambda b, s: (0, 0)),
                st_spec]
    if second:
        args += [yprev, dskip]
        in_specs += [pl.BlockSpec((rblk, inner), rmap), pl.BlockSpec((1, inner), lambda b, s: (0, 0))]
    kern = functools.partial(_ssd_kernel, reverse=reverse, second=second, nchunk=rblk // SSD_CHUNK,
                             nblk=nblk, groups=groups)
    return pl.pallas_call(
        kern,
        out_shape=[jax.ShapeDtypeStruct((batch * seq, inner), BF16 if second else F32),
                   jax.ShapeDtypeStruct((batch, groups, nst, inner // groups), F32)],
        grid=(batch, nblk),
        in_specs=in_specs,
        out_specs=[pl.BlockSpec((rblk, inner), rmap), st_spec],
        scratch_shapes=[pltpu.VMEM((groups, nst, inner // groups), F32)],
        compiler_params=_cparams("parallel", "arbitrary"),
        name="ssd_scan_bwd" if reverse else "ssd_scan_fwd",
    )(*args)


def _rope_tables(seq):
    pos = jnp.arange(seq)
    rows = (pos // GRID_W).astype(F32)
    cols = (pos % GRID_W).astype(F32)
    half = HEAD_DIM // 2
    inv = ROPE_BASE ** (-jnp.arange(0, half, 2, dtype=F32) / half)
    ang = jnp.concatenate([rows[:, None] * inv, cols[:, None] * inv], axis=-1)
    cos, sin = jnp.cos(ang), jnp.sin(ang)
    return jnp.concatenate([cos, cos], axis=-1), jnp.concatenate([-sin, sin], axis=-1)


def _deinterleave_perm(n_cols, n_rot_cols):
    within = np.concatenate([np.arange(0, HEAD_DIM, 2), np.arange(1, HEAD_DIM, 2)])
    perm = np.arange(n_cols)
    for h in range(n_rot_cols // HEAD_DIM):
        perm[h * HEAD_DIM:(h + 1) * HEAD_DIM] = h * HEAD_DIM + within
    return perm


def _moe_layer(hs, logits, router_bias, w_gate, w_up, w_down):
    meta, cnt = _router(logits, router_bias)
    dest, plan = _moe_plan(meta, cnt)
    xs = _moe_dispatch(hs, dest)
    ys = _expert_ffn(xs, plan, w_gate, w_up, w_down)
    return ys, dest, meta


def kernel(x, c, ctx, c_ctx, w_mod, b_mod, norm_mix, norm_ffn, norm_final, w_in_even, w_out_even,
           ret_decay, na_rpb, w_in_odd, conv_w, conv_b, dt_bias, a_log, d_skip, ssd_norm, w_out_odd,
           w_router, router_bias, w_gate, w_up, w_down):
    batch, seq, d = x.shape
    ctx_len = ctx.shape[1]
    depth = w_mod.shape[0]
    assert batch + 1 <= SUBLANES
    tl, tc = batch * seq, batch * ctx_len

    cvec = jnp.zeros((SUBLANES, d), F32).at[0].set(c_ctx).at[1:1 + batch].set(c)
    mod = _modulation(cvec, w_mod, b_mod)

    def mod_vecs(i, k):
        m = mod[i, :, k * d:(k + 1) * d]
        return m[1:1 + batch].reshape(batch, 1, d), m[0:1].reshape(1, 1, d)

    wr = jnp.zeros((d, LANES), F32).at[:, :N_EXPERTS].set(w_router.astype(F32))
    wrh, wrl = _split_bf16(wr)

    lat = x.reshape(tl, d)
    cx = ctx.reshape(tc, d)
    out = None
    for i in range(depth):
        last = i == depth - 1
        j = i // 2
        (sh1, csh1), (sc1, csc1), (g1, cg1) = mod_vecs(i, 0), mod_vecs(i, 1), mod_vecs(i, 2)
        (sh2, csh2), (sc2, csc2), (g2, cg2) = mod_vecs(i, 3), mod_vecs(i, 4), mod_vecs(i, 5)
        if i % 2 == 0:
            ret_heads = w_out_even.shape[1] // (2 * HEAD_DIM)
            rot_cols = 2 * ret_heads * HEAD_DIM
            w_in = w_in_even[j][:, _deinterleave_perm(w_in_even.shape[2], rot_cols)].astype(BF16)
            cos, sin = _rope_tables(seq)
            ones, zeros = jnp.ones((tc, HEAD_DIM), F32), jnp.zeros((tc, HEAD_DIM), F32)
            proj_l, = _inproj(lat, norm_mix[i], sh1, sc1, w_in, seg_rows=seq, tm=1024, tn=512,
                              even=True, cos=cos, sin=sin)
            proj_c, = _inproj(cx, norm_mix[i], csh1, csc1, w_in, seg_rows=tc, tm=tc, tn=512,
                              even=True, cos=ones, sin=zeros)
            tabs = _retention_tables(ret_decay[j])
            zero = jnp.zeros((batch, ret_heads, HEAD_DIM, HEAD_DIM), F32)
            ret_c, s_f, s_b = _retention(proj_c, tabs, zero, zero, batch=batch, seq=ctx_len, rblk=ctx_len)
            ret_l, _, _ = _retention(proj_l, tabs, s_f, s_b, batch=batch, seq=seq, rblk=1024)
            bias = _na_bias_tables(na_rpb[j], seq // GRID_W)
            na_l = _neighbourhood_attention(proj_l, proj_c, bias, batch=batch, seq=seq, ctx_len=ctx_len)
            w_out = w_out_even[j].astype(BF16)
            mix_l, ssd_l, mix_c, ssd_c = (ret_l, na_l), None, None, None
            if not last:
                mix_c = (ret_c, _ctx_attention(proj_c, batch=batch, ctx_len=ctx_len, heads=ret_heads))
            tm_out = 512
        else:
            inner = w_out_odd.shape[1]
            n_main = inner + conv_w.shape[2]
            w_in = w_in_odd[j]
            w_main = w_in[:, :n_main].astype(BF16)
            heads = w_in.shape[1] - n_main
            w_dt = jnp.zeros((d, LANES), F32).at[:, :heads].set(w_in[:, n_main:]).astype(BF16)
            zx_l, dt_l = _inproj(lat, norm_mix[i], sh1, sc1, w_main, seg_rows=seq, tm=1024, tn=512, wdt=w_dt)
            zx_c, dt_c = _inproj(cx, norm_mix[i], csh1, csc1, w_main, seg_rows=tc, tm=tc, tn=512, wdt=w_dt)
            act_l = _conv_silu(zx_l, conv_w[j], conv_b[j], seq=seq, col_off=inner)
            act_c = _conv_silu(zx_c, conv_w[j], conv_b[j], seq=ctx_len, col_off=inner)
            a_neg = -jnp.exp(a_log[j].astype(F32))

            def prm(dr):
                return (jnp.zeros((SUBLANES, LANES), F32).at[0, :heads].set(a_neg[dr])
                        .at[1, :heads].set(dt_bias[j, dr].astype(F32)))

            dsk = jnp.repeat((d_skip[j, 0] + d_skip[j, 1]).astype(F32), inner // heads)[None, :]
            zero = jnp.zeros((batch, SSD_GROUPS, SSD_STATE, inner // SSD_GROUPS), F32)
            yf_c, s_f = _ssd_pass(act_c, dt_c, prm(0), zero, batch=batch, seq=ctx_len, reverse=False)
            y_c, s_b = _ssd_pass(act_c, dt_c, prm(1), zero, batch=batch, seq=ctx_len, reverse=True,
                                 yprev=yf_c, dskip=dsk)
            yf_l, _ = _ssd_pass(act_l, dt_l, prm(0), s_f, batch=batch, seq=seq, reverse=False)
            y_l, _ = _ssd_pass(act_l, dt_l, prm(1), s_b, batch=batch, seq=seq, reverse=True,
                               yprev=yf_l, dskip=dsk)
            w_out = w_out_odd[j].astype(BF16)
            mix_l, ssd_l = None, (y_l, zx_l, ssd_norm[j].astype(F32))
            mix_c, ssd_c = None, (None if last else (y_c, zx_c, ssd_norm[j].astype(F32)))
            tm_out = 256

        nwf = norm_ffn[i].astype(F32)
        lat2, h_l, lg_l = _outproj(mix_l, w_out, lat, g1, nwf, sh2, sc2, wrh, wrl, seg_rows=seq,
                                   tm=tm_out, ssd=ssd_l)
        if last:
            ys, pos, meta = _moe_layer([h_l], lg_l, router_bias, w_gate[i], w_up[i], w_down[i])
            out = _moe_combine(ys, pos, meta, lat2, g2, norm_final.astype(F32), seg_rows=seq, final=True)
        else:
            cx2, h_c, lg_c = _outproj(mix_c, w_out, cx, cg1, nwf, csh2, csc2, wrh, wrl, seg_rows=tc,
                                      tm=tm_out, ssd=ssd_c)
            ys, pos, meta = _moe_layer([h_c, h_l], jnp.concatenate([lg_c, lg_l], axis=0), router_bias,
                                       w_gate[i], w_up[i], w_down[i])
            ones = jnp.ones((d,), F32)
            cx = _moe_combine(ys, pos[:tc], meta[:tc], cx2, cg2, ones, seg_rows=tc, final=False)
            lat = _moe_combine(ys, pos[tc:], meta[tc:], lat2, g2, ones, seg_rows=seq, final=False)
    return out.reshape(batch, seq, d)
```

```python
import functools
import math

import numpy as np
import jax
import jax.numpy as jnp
from jax import lax
from jax.experimental import pallas as pl
from jax.experimental.pallas import tpu as pltpu

GRID_W = 64
EPS = 1e-6
HEAD_DIM = 128
ROPE_BASE = 10000.0
RET_CHUNK = 128
NA_ROWS = 8
NA_COLS = 16
SSD_HEAD_DIM = 64
SSD_GROUPS = 8
SSD_STATE = 128
SSD_CONV = 5
SSD_CHUNK = 128
N_EXPERTS = 32
N_GROUPS = 8
EXPERTS_PER_GROUP = N_EXPERTS // N_GROUPS
TOP_K = 2
MOE_BLOCK = 256

LANES = 128
SUBLANES = 8
VMEM_LIMIT_BYTES = 56 * 1024 * 1024

NEG_INF = -1e30
F32 = jnp.float32
BF16 = jnp.bfloat16


def _cparams(*sem):
    return pltpu.CompilerParams(dimension_semantics=sem, vmem_limit_bytes=VMEM_LIMIT_BYTES)


def _silu(x):
    return x * (1.0 / (1.0 + jnp.exp(-x)))


def _mod_kernel(c_ref, w_ref, b_ref, o_ref):
    c = _silu(c_ref[...]).astype(BF16)
    o_ref[0] = jnp.dot(c, w_ref[0].astype(BF16), preferred_element_type=F32) + b_ref[0]


def _modulation(cvec, w_mod, b_mod):
    depth, d, n = w_mod.shape
    tn = 1024
    return pl.pallas_call(
        _mod_kernel,
        out_shape=jax.ShapeDtypeStruct((depth, SUBLANES, n), F32),
        grid=(depth, n // tn),
        in_specs=[
            pl.BlockSpec((SUBLANES, d), lambda i, j: (0, 0)),
            pl.BlockSpec((1, d, tn), lambda i, j: (i, 0, j)),
            pl.BlockSpec((1, 1, tn), lambda i, j: (i, 0, j)),
        ],
        out_specs=pl.BlockSpec((1, SUBLANES, tn), lambda i, j: (i, 0, j)),
        compiler_params=_cparams("parallel", "parallel"),
        name="adaln_modulation",
    )(cvec, w_mod, b_mod.reshape(depth, 1, n))


def _rms_modulate(x, nw, sh, sc):
    r = lax.rsqrt(jnp.mean(x * x, axis=-1, keepdims=True) + EPS)
    return (x * r * nw) * (1.0 + sc) + sh


def _rope(t, cos, sin):
    return t * cos + pltpu.roll(t, HEAD_DIM // 2, axis=1) * sin


def _inproj_kernel(*refs, even, has_dt, tn):
    it = iter(refs)
    x_ref, nw_ref, sh_ref, sc_ref, w_ref = (next(it) for _ in range(5))
    if even:
        cos_ref, sin_ref = next(it), next(it)
    if has_dt:
        wdt_ref = next(it)
    o_ref = next(it)
    if has_dt:
        dt_ref = next(it)
    h_scr = next(it)

    j = pl.program_id(1)

    @pl.when(j == 0)
    def _():
        h = _rms_modulate(x_ref[...], nw_ref[...], sh_ref[0], sc_ref[0]).astype(BF16)
        h_scr[...] = h
        if has_dt:
            dt_ref[...] = jnp.dot(h, wdt_ref[...], preferred_element_type=F32)

    acc = jnp.dot(h_scr[...], w_ref[...], preferred_element_type=F32)

    if not even:
        o_ref[...] = acc.astype(o_ref.dtype)
        return

    heads_per_tile = tn // HEAD_DIM
    tiles_per_kilo = 1024 // tn
    seg = j // tiles_per_kilo
    scale = HEAD_DIM ** -0.5

    @pl.when(seg <= 1)
    def _():
        mul = jnp.where(seg == 1, scale, 1.0).astype(F32)
        cos, sin = cos_ref[...], sin_ref[...]
        for hh in range(heads_per_tile):
            sl = slice(hh * HEAD_DIM, (hh + 1) * HEAD_DIM)
            o_ref[:, sl] = (_rope(acc[:, sl], cos, sin) * mul).astype(o_ref.dtype)

    @pl.when(seg == 4)
    def _():
        o_ref[...] = (acc * scale).astype(o_ref.dtype)

    @pl.when((seg == 2) | (seg == 3) | (seg >= 5))
    def _():
        o_ref[...] = acc.astype(o_ref.dtype)


def _seg_spec(d, seg_rows, tm):
    return pl.BlockSpec((1, 1, d), lambda i, j: ((i * tm) // seg_rows, 0, 0))


def _inproj(x, nw, sh, sc, w, *, seg_rows, tm, tn, even=False, cos=None, sin=None, wdt=None):
    t, d = x.shape
    n = w.shape[1]
    tm = min(tm, t)
    assert t % tm == 0 and n % tn == 0 and seg_rows % tm == 0
    args, in_specs = [x], [pl.BlockSpec((tm, d), lambda i, j: (i, 0))]
    args += [nw.reshape(1, d), sh, sc, w]
    in_specs += [pl.BlockSpec((1, d), lambda i, j: (0, 0)), _seg_spec(d, seg_rows, tm),
                 _seg_spec(d, seg_rows, tm), pl.BlockSpec((d, tn), lambda i, j: (0, j))]
    if even:
        lblk = cos.shape[0] // tm
        args += [cos, sin]
        in_specs += [pl.BlockSpec((tm, HEAD_DIM), lambda i, j: (i % lblk, 0))] * 2
    if wdt is not None:
        args.append(wdt)
        in_specs.append(pl.BlockSpec(wdt.shape, lambda i, j: (0, 0)))
    out_shape = [jax.ShapeDtypeStruct((t, n), BF16)]
    out_specs = [pl.BlockSpec((tm, tn), lambda i, j: (i, j))]
    if wdt is not None:
        out_shape.append(jax.ShapeDtypeStruct((t, wdt.shape[1]), F32))
        out_specs.append(pl.BlockSpec((tm, wdt.shape[1]), lambda i, j: (i, 0)))
    kern = functools.partial(_inproj_kernel, even=even, has_dt=wdt is not None, tn=tn)
    return pl.pallas_call(
        kern,
        out_shape=out_shape,
        grid=(t // tm, n // tn),
        in_specs=in_specs,
        out_specs=out_specs,
        scratch_shapes=[pltpu.VMEM((tm, d), BF16)],
        compiler_params=_cparams("parallel", "arbitrary"),
        name="inproj_even" if even else "inproj_odd",
    )(*args)


RET_UNROLL = 4
_TN = (((0,), (0,)), ((), ()))
_NT = (((1,), (1,)), ((), ()))


def _retention_kernel(q_ref, k_ref, v_ref, g_ref, tab_ref, sf0_ref, sb0_ref,
                      o_ref, sf_ref, sb_ref, sb_store, s_run, *, nchunk, nblk):
    p = pl.program_id(2)
    s = pl.program_id(3)
    q = RET_CHUNK
    dcomb = tab_ref[0, 0]

    @pl.when(p == 0)
    def _():
        @pl.when(s == 0)
        def _():
            s_run[...] = sb0_ref[0, 0]

        blk = nblk - 1 - s
        kb, cb = tab_ref[0, 4], tab_ref[0, 6]

        def body(ci, carry):
            c = nchunk - 1 - ci
            rows = pl.ds(pl.multiple_of(c * q, q), q)
            st = s_run[...]
            sb_store[blk * nchunk + c] = st.astype(BF16)
            kd = (k_ref[rows, :].astype(F32) * kb).astype(BF16)
            upd = lax.dot_general(kd, v_ref[rows, :], _TN, preferred_element_type=F32)
            s_run[...] = cb * st + upd
            return carry

        lax.fori_loop(0, nchunk, body, 0, unroll=min(nchunk, RET_UNROLL))

        @pl.when(s == nblk - 1)
        def _():
            sb_ref[0, 0] = s_run[...]

    @pl.when(p == 1)
    def _():
        @pl.when(s == 0)
        def _():
            s_run[...] = sf0_ref[0, 0]

        qf, qb, kf, cf = tab_ref[0, 1], tab_ref[0, 2], tab_ref[0, 3], tab_ref[0, 5]

        def body(c, carry):
            rows = pl.ds(pl.multiple_of(c * q, q), q)
            qc, kc, vc = q_ref[rows, :], k_ref[rows, :], v_ref[rows, :]
            qc32, kc32 = qc.astype(F32), kc.astype(F32)
            st = s_run[...]
            sc = lax.dot_general(qc, kc, _NT, preferred_element_type=F32) * dcomb
            o = jnp.dot(sc.astype(BF16), vc, preferred_element_type=F32)
            qcat = jnp.concatenate([(qc32 * qf).astype(BF16), (qc32 * qb).astype(BF16)], axis=1)
            scat = jnp.concatenate([st.astype(BF16), sb_store[s * nchunk + c]], axis=0)
            o = o + jnp.dot(qcat, scat, preferred_element_type=F32)
            upd = lax.dot_general((kc32 * kf).astype(BF16), vc, _TN, preferred_element_type=F32)
            s_run[...] = cf * st + upd
            o = o * lax.rsqrt(jnp.mean(o * o, axis=-1, keepdims=True) + EPS)
            o_ref[rows, :] = (o * _silu(g_ref[rows, :].astype(F32))).astype(o_ref.dtype)
            return carry

        lax.fori_loop(0, nchunk, body, 0, unroll=min(nchunk, RET_UNROLL))

        @pl.when(s == nblk - 1)
        def _():
            sf_ref[0, 0] = s_run[...]


def _retention_tables(ret_logit):
    q = RET_CHUNK
    lg = jax.nn.log_sigmoid(ret_logit.astype(F32))
    lf, lb = lg[0][:, None, None], lg[1][:, None, None]
    i = jnp.arange(q, dtype=F32)[None, :, None]
    j = jnp.arange(q, dtype=F32)[None, None, :]
    ones = jnp.ones((1, 1, q), F32)
    dcomb = (jnp.where(i >= j, jnp.exp(lf * jnp.maximum(i - j, 0.0)), 0.0)
             + jnp.where(j >= i, jnp.exp(lb * jnp.maximum(j - i, 0.0)), 0.0))
    qf = jnp.exp(lf * (i + 1.0)) * ones
    qb = jnp.exp(lb * (q - i)) * ones
    kf = jnp.exp(lf * (q - 1.0 - i)) * ones
    kb = jnp.exp(lb * i) * ones
    cf = jnp.exp(lf * q) * jnp.ones((1, q, q), F32)
    cb = jnp.exp(lb * q) * jnp.ones((1, q, q), F32)
    return jnp.stack([dcomb, qf, qb, kf, kb, cf, cb], axis=1)


def _retention(proj, tabs, sf0, sb0, *, batch, seq, rblk):
    heads = tabs.shape[0]
    rblk = min(rblk, seq)
    nblk = seq // rblk
    nchunk = rblk // RET_CHUNK
    hd = HEAD_DIM

    def qmap(off):
        return lambda b, h, p, s: (b * nblk + p * s, off + h)

    def kvmap(off):
        return lambda b, h, p, s: (b * nblk + s + (1 - p) * (nblk - 1 - 2 * s), off + h)

    st_spec = pl.BlockSpec((1, 1, hd, hd), lambda b, h, p, s: (b, h, 0, 0))
    kern = functools.partial(_retention_kernel, nchunk=nchunk, nblk=nblk)
    return pl.pallas_call(
        kern,
        out_shape=[jax.ShapeDtypeStruct((batch * seq, heads * hd), BF16),
                   jax.ShapeDtypeStruct((batch, heads, hd, hd), F32),
                   jax.ShapeDtypeStruct((batch, heads, hd, hd), F32)],
        grid=(batch, heads, 2, nblk),
        in_specs=[pl.BlockSpec((rblk, hd), qmap(0)),
                  pl.BlockSpec((rblk, hd), kvmap(heads)),
                  pl.BlockSpec((rblk, hd), kvmap(2 * heads)),
                  pl.BlockSpec((rblk, hd), qmap(3 * heads)),
                  pl.BlockSpec((1, 7, RET_CHUNK, RET_CHUNK), lambda b, h, p, s: (h, 0, 0, 0)),
                  st_spec, st_spec],
        out_specs=[pl.BlockSpec((rblk, hd), qmap(0)), st_spec, st_spec],
        scratch_shapes=[pltpu.VMEM((seq // RET_CHUNK, hd, hd), BF16), pltpu.VMEM((hd, hd), F32)],
        compiler_params=_cparams("parallel", "parallel", "arbitrary", "arbitrary"),
        name="retention",
    )(proj, proj, proj, proj, tabs, sf0, sb0)


NA_QROWS = 8
NA_WROWS = 16
NA_KBLK = 4
NA_QSPLIT = 2


def _na_kernel(q_ref, k0, k1, k2, k3, v0, v1, v2, v3, kc_ref, vc_ref, bias_ref, o_ref):
    kb = NA_KBLK * GRID_W
    nq = q_ref.shape[0] // NA_QSPLIT
    for part in range(NA_QSPLIT):
        rows = slice(part * nq, (part + 1) * nq)
        q = q_ref[rows, :]
        scores = []
        for i, kr in enumerate((k0, k1, k2, k3)):
            sc = lax.dot_general(q, kr[...], _NT, preferred_element_type=F32)
            scores.append(sc + bias_ref[0, 0, rows, i * kb:(i + 1) * kb])
        scores.append(lax.dot_general(q, kc_ref[...], _NT, preferred_element_type=F32))
        m = scores[0].max(axis=-1, keepdims=True)
        for sc in scores[1:]:
            m = jnp.maximum(m, sc.max(axis=-1, keepdims=True))
        acc = jnp.zeros((nq, HEAD_DIM), F32)
        l = jnp.zeros((nq, 1), F32)
        for sc, vr in zip(scores, (v0, v1, v2, v3, vc_ref)):
            pr = jnp.exp(sc - m)
            l = l + pr.sum(axis=-1, keepdims=True)
            acc = acc + jnp.dot(pr.astype(BF16), vr[...], preferred_element_type=F32)
        o_ref[rows, :] = (acc / l).astype(o_ref.dtype)


def _na_bias_tables(rpb, rows):
    nrb = rows // NA_QROWS
    nr, nc = 2 * NA_ROWS - 1, 2 * NA_COLS - 1
    hi = lax.Precision.HIGHEST
    qc = np.arange(GRID_W)[:, None]
    kc = np.arange(GRID_W)[None, :]
    c0 = np.clip(qc - NA_COLS // 2, 0, GRID_W - NA_COLS)
    col_valid = (kc >= c0) & (kc < c0 + NA_COLS)
    sel_c = (np.arange(nc)[:, None, None] == (kc - qc + NA_COLS - 1)[None]).astype(np.float32)
    toep = jnp.einsum('haj,jqk->haqk', rpb.astype(F32), sel_c, precision=hi)
    qr = np.arange(NA_QROWS)[:, None]
    kr = np.arange(NA_WROWS)[None, :]
    tabs = []
    for rb in (0, 1, nrb - 1):
        w0 = int(np.clip(rb * NA_QROWS - NA_ROWS // 2, 0, rows - NA_WROWS))
        r = rb * NA_QROWS + qr
        krow = w0 + kr
        r0 = np.clip(r - NA_ROWS // 2, 0, rows - NA_ROWS)
        row_valid = (krow >= r0) & (krow < r0 + NA_ROWS)
        sel_r = ((np.arange(nr)[None, None, :] == (krow - r + NA_ROWS - 1)[:, :, None])
                 & row_valid[:, :, None]).astype(np.float32)
        b = jnp.einsum('rka,haqc->hrqkc', sel_r, toep, precision=hi)
        valid = row_valid[:, None, :, None] & col_valid[None, :, None, :]
        b = jnp.where(valid[None], b, NEG_INF)
        tabs.append(b.reshape(rpb.shape[0], NA_QROWS * GRID_W, NA_WROWS * GRID_W))
    return jnp.stack(tabs, axis=1)


def _neighbourhood_attention(proj, proj_ctx, bias, *, batch, seq, ctx_len):
    heads = bias.shape[0]
    hd = HEAD_DIM
    qblk = NA_QROWS * GRID_W
    kblk = NA_KBLK * GRID_W
    nrb = seq // qblk
    nkb = seq // kblk
    assert seq // GRID_W >= NA_WROWS + NA_QROWS
    qoff, koff, voff = 4 * heads, 5 * heads, 6 * heads

    def kmap(off, i):
        def f(b, h, r):
            u0 = jnp.clip(2 * r - 1, 0, nkb - NA_WROWS // NA_KBLK)
            return (b * nkb + u0 + i, off + h)
        return f

    def bias_map(b, h, r):
        return (h, (r > 0).astype(jnp.int32) + (r == nrb - 1).astype(jnp.int32), 0, 0)

    kspecs = [pl.BlockSpec((kblk, hd), kmap(koff, i)) for i in range(4)]
    vspecs = [pl.BlockSpec((kblk, hd), kmap(voff, i)) for i in range(4)]
    return pl.pallas_call(
        _na_kernel,
        out_shape=jax.ShapeDtypeStruct((batch * seq, heads * hd), BF16),
        grid=(batch, heads, nrb),
        in_specs=[pl.BlockSpec((qblk, hd), lambda b, h, r: (b * nrb + r, qoff + h))] + kspecs + vspecs + [
            pl.BlockSpec((ctx_len, hd), lambda b, h, r: (b, koff + h)),
            pl.BlockSpec((ctx_len, hd), lambda b, h, r: (b, voff + h)),
            pl.BlockSpec((1, 1, qblk, NA_WROWS * GRID_W), bias_map)],
        out_specs=pl.BlockSpec((qblk, hd), lambda b, h, r: (b * nrb + r, h)),
        compiler_params=_cparams("parallel", "parallel", "arbitrary"),
        name="neighbourhood_attention",
    )(*([proj] * 9), proj_ctx, proj_ctx, bias)


def _ctx_attn_kernel(q_ref, k_ref, v_ref, o_ref):
    sc = lax.dot_general(q_ref[...], k_ref[...], _NT, preferred_element_type=F32)
    pr = jnp.exp(sc - sc.max(axis=-1, keepdims=True))
    acc = jnp.dot(pr.astype(BF16), v_ref[...], preferred_element_type=F32)
    o_ref[...] = (acc / pr.sum(axis=-1, keepdims=True)).astype(o_ref.dtype)


def _ctx_attention(proj_ctx, *, batch, ctx_len, heads):
    hd = HEAD_DIM
    return pl.pallas_call(
        _ctx_attn_kernel,
        out_shape=jax.ShapeDtypeStruct((batch * ctx_len, heads * hd), BF16),
        grid=(batch, heads),
        in_specs=[pl.BlockSpec((ctx_len, hd), lambda b, h: (b, 4 * heads + h)),
                  pl.BlockSpec((ctx_len, hd), lambda b, h: (b, 5 * heads + h)),
                  pl.BlockSpec((ctx_len, hd), lambda b, h: (b, 6 * heads + h))],
        out_specs=pl.BlockSpec((ctx_len, hd), lambda b, h: (b, h)),
        compiler_params=_cparams("parallel", "parallel"),
        name="context_attention",
    )(proj_ctx, proj_ctx, proj_ctx)


def _split_bf16(x):
    hi = x.astype(BF16)
    return hi, (x - hi.astype(F32)).astype(BF16)


def _outproj_kernel(*refs, odd):
    it = iter(refs)
    if odd:
        y_ref, z_ref, nws_ref = next(it), next(it), next(it)
    else:
        mr_ref, mn_ref = next(it), next(it)
    w_ref, lat_ref, g1_ref, nwf_ref, sh_ref, sc_ref = (next(it) for _ in range(6))
    wrh_ref, wrl_ref = next(it), next(it)
    lat2_ref, h_ref, lg_ref = next(it), next(it), next(it)

    if odd:
        u = y_ref[...].astype(F32) * _silu(z_ref[...].astype(F32))
        u = u * lax.rsqrt(jnp.mean(u * u, axis=-1, keepdims=True) + EPS) * nws_ref[...]
        o = jnp.dot(u.astype(BF16), w_ref[...], preferred_element_type=F32)
    else:
        half = mr_ref.shape[1]
        o = (jnp.dot(mr_ref[...], w_ref[0:half, :], preferred_element_type=F32)
             + jnp.dot(mn_ref[...], w_ref[half:2 * half, :], preferred_element_type=F32))
    lat2 = lat_ref[...] + g1_ref[0] * o
    lat2_ref[...] = lat2
    h = _rms_modulate(lat2, nwf_ref[...], sh_ref[0], sc_ref[0])
    h_ref[...] = h
    hh, hl = _split_bf16(h)
    wrh = wrh_ref[...]
    lg_ref[...] = (jnp.dot(hh, wrh, preferred_element_type=F32)
                   + jnp.dot(hl, wrh, preferred_element_type=F32)
                   + jnp.dot(hh, wrl_ref[...], preferred_element_type=F32))


def _outproj(mix, w, lat, g1, nwf, sh, sc, wrh, wrl, *, seg_rows, tm, ssd=None):
    t, d = lat.shape
    tm = min(tm, t)
    assert t % tm == 0 and seg_rows % tm == 0
    odd = ssd is not None
    row = lambda i: (i, 0)
    seg = pl.BlockSpec((1, 1, d), lambda i: ((i * tm) // seg_rows, 0, 0))
    const = lambda shape: pl.BlockSpec(shape, lambda i: (0,) * len(shape))
    if odd:
        y, zx, nws = ssd
        kin = y.shape[1]
        args = [y, zx, nws.reshape(1, kin)]
        in_specs = [pl.BlockSpec((tm, kin), row), pl.BlockSpec((tm, kin), row), const((1, kin))]
    else:
        mr, mn = mix
        args = [mr, mn]
        in_specs = [pl.BlockSpec((tm, mr.shape[1]), row), pl.BlockSpec((tm, mn.shape[1]), row)]
    args += [w, lat, g1, nwf.reshape(1, d), sh, sc, wrh, wrl]
    in_specs += [pl.BlockSpec(w.shape, lambda i: (0, 0), pipeline_mode=pl.Buffered(1)),
                 pl.BlockSpec((tm, d), row), seg, const((1, d)), seg, seg,
                 const(wrh.shape), const(wrl.shape)]
    nl = wrh.shape[1]
    return pl.pallas_call(
        functools.partial(_outproj_kernel, odd=odd),
        out_shape=[jax.ShapeDtypeStruct((t, d), F32),
                   jax.ShapeDtypeStruct((t, d), F32),
                   jax.ShapeDtypeStruct((t, nl), F32)],
        grid=(t // tm,),
        in_specs=in_specs,
        out_specs=[pl.BlockSpec((tm, d), row), pl.BlockSpec((tm, d), row), pl.BlockSpec((tm, nl), row)],
        compiler_params=_cparams("parallel"),
        name="outproj_odd" if odd else "outproj_even",
    )(*args)


ROUTER_TILE = 256
META_E, META_RANK, META_GATE = 0, 2, 4


def _router_kernel(lg_ref, rb_ref, meta_ref, cnt_ref, run_scr):
    tm = lg_ref.shape[0]
    i = pl.program_id(0)

    @pl.when(i == 0)
    def _():
        run_scr[...] = jnp.zeros_like(run_scr)

    big = 1e9
    lane = lax.broadcasted_iota(jnp.int32, (tm, LANES), 1)
    lanef = lane.astype(F32)
    scores = 1.0 / (1.0 + jnp.exp(-lg_ref[...]))
    biased = jnp.where(lane < N_EXPERTS, scores + rb_ref[...], NEG_INF)
    member = lane % EXPERTS_PER_GROUP
    best = jnp.full((tm, LANES), NEG_INF, F32)
    for k in range(1, EXPERTS_PER_GROUP):
        best = jnp.maximum(best, jnp.where(member >= k, biased + pltpu.roll(biased, k, axis=1), NEG_INF))
    grp = (lane // EXPERTS_PER_GROUP).astype(F32)
    top = best.max(axis=-1, keepdims=True)
    g_sel = jnp.where(best == top, grp, big).min(axis=-1, keepdims=True)
    v = jnp.where(grp == g_sel, biased, NEG_INF)
    i1 = jnp.where(v == v.max(axis=-1, keepdims=True), lanef, big).min(axis=-1, keepdims=True)
    v = jnp.where(lanef == i1, NEG_INF, v)
    i2 = jnp.where(v == v.max(axis=-1, keepdims=True), lanef, big).min(axis=-1, keepdims=True)
    oh1, oh2 = lanef == i1, lanef == i2
    s1 = jnp.where(oh1, scores, 0.0).sum(axis=-1, keepdims=True)
    s2 = jnp.where(oh2, scores, 0.0).sum(axis=-1, keepdims=True)
    den = s1 + s2
    oh = oh1.astype(F32) + oh2.astype(F32)
    earlier = (lax.broadcasted_iota(jnp.int32, (tm, tm), 0)
               > lax.broadcasted_iota(jnp.int32, (tm, tm), 1)).astype(BF16)
    base = jnp.dot(earlier, oh.astype(BF16), preferred_element_type=F32) + run_scr[0:1, :]
    r1 = jnp.where(oh1, base, 0.0).sum(axis=-1, keepdims=True)
    r2 = jnp.where(oh2, base, 0.0).sum(axis=-1, keepdims=True)
    total = run_scr[0:1, :] + oh.sum(axis=0, keepdims=True)
    run_scr[0:1, :] = total
    cnt_ref[...] = jnp.broadcast_to(total, cnt_ref.shape)
    meta = jnp.zeros((tm, LANES), F32)
    for ln, val in ((META_E, i1), (META_E + 1, i2), (META_RANK, r1), (META_RANK + 1, r2),
                    (META_GATE, s1 / den), (META_GATE + 1, s2 / den)):
        meta = jnp.where(lane == ln, val, meta)
    meta_ref[...] = meta


def _router(logits, router_bias):
    t = logits.shape[0]
    tm = ROUTER_TILE
    rb = jnp.zeros((1, LANES), F32).at[0, :N_EXPERTS].set(router_bias.astype(F32))
    return pl.pallas_call(
        _router_kernel,
        out_shape=[jax.ShapeDtypeStruct((t, LANES), F32), jax.ShapeDtypeStruct((SUBLANES, LANES), F32)],
        grid=(t // tm,),
        in_specs=[pl.BlockSpec((tm, LANES), lambda i: (i, 0)), pl.BlockSpec((1, LANES), lambda i: (0, 0))],
        out_specs=[pl.BlockSpec((tm, LANES), lambda i: (i, 0)),
                   pl.BlockSpec((SUBLANES, LANES), lambda i: (0, 0))],
        scratch_shapes=[pltpu.VMEM((SUBLANES, LANES), F32)],
        compiler_params=_cparams("arbitrary"),
        name="moe_router",
    )(logits, rb)


def _moe_plan(meta, cnt):
    t = meta.shape[0]
    blk = MOE_BLOCK
    counts = cnt[0, :N_EXPERTS].astype(jnp.int32)
    ends = jnp.cumsum(counts)
    offsets = ends - counts
    e = meta[:, META_E:META_E + TOP_K].astype(jnp.int32)
    rank = meta[:, META_RANK:META_RANK + TOP_K].astype(jnp.int32)
    onehot = e[..., None] == jnp.arange(N_EXPERTS, dtype=jnp.int32)
    dest = jnp.sum(jnp.where(onehot, offsets, 0), axis=-1) + rank
    n_tiles = t * TOP_K // blk
    tile_start = jnp.arange(n_tiles, dtype=jnp.int32) * blk
    def count_le(sorted_vals, query):
        return jnp.sum(sorted_vals[None, :] <= query[:, None], axis=-1).astype(jnp.int32)

    first_e = count_le(ends, tile_start)
    last_e = count_le(ends, tile_start + blk - 1)
    nsteps = last_e - first_e + 1
    step_end = jnp.cumsum(nsteps)
    step_start = step_end - nsteps
    s = jnp.arange(n_tiles + N_EXPERTS - 1, dtype=jnp.int32)
    valid = s < step_end[-1]
    tile_of = jnp.minimum(count_le(step_end, s), n_tiles - 1)
    exp_of = jnp.where(valid, first_e[tile_of] + s - step_start[tile_of], last_e[-1])
    exp_of = jnp.clip(exp_of, 0, N_EXPERTS - 1).astype(jnp.int32)
    lo = jnp.clip(offsets[exp_of] - tile_of * blk, 0, blk)
    hi = jnp.clip(ends[exp_of] - tile_of * blk, 0, blk)
    lo = jnp.where(valid, lo, 0).astype(jnp.int32)
    hi = jnp.where(valid, hi, 0).astype(jnp.int32)
    first = (valid & (s == step_start[tile_of])).astype(jnp.int32)
    return dest.astype(jnp.int32), (tile_of, exp_of, lo, hi, first)


DMA_UNROLL = 8


def _dispatch_kernel(*refs, tm, src_tiles):
    dest_ref, srcs, xs_hbm, sem = refs[0], refs[1:-2], refs[-2], refs[-1]
    i = pl.program_id(0)

    def scatter_from(src):
        def body(r, c):
            for k in range(TOP_K):
                pltpu.make_async_copy(src.at[pl.ds(r, 1), :],
                                      xs_hbm.at[pl.ds(dest_ref[0, 0, TOP_K * r + k], 1), :], sem).start()
            return c
        lax.fori_loop(0, tm, body, 0, unroll=DMA_UNROLL)
        for _ in range(TOP_K):
            pltpu.make_async_copy(src, xs_hbm.at[pl.ds(0, tm), :], sem).wait()

    first_tile = 0
    for src, ntile in zip(srcs, src_tiles):
        @pl.when((i >= first_tile) & (i < first_tile + ntile))
        def _(src=src):
            scatter_from(src)
        first_tile += ntile


def _moe_dispatch(hs, dest):
    d = hs[0].shape[1]
    t = dest.shape[0]
    tm = 256
    nt = t // tm
    src_tiles = tuple(h.shape[0] // tm for h in hs)
    assert sum(src_tiles) == nt and all(h.shape[0] % tm == 0 for h in hs)
    starts = np.cumsum((0,) + src_tiles[:-1])
    src_specs = [pl.BlockSpec((tm, d), lambda i, s=int(s), n=n: (jnp.clip(i - s, 0, n - 1), 0))
                 for s, n in zip(starts, src_tiles)]
    kern = functools.partial(_dispatch_kernel, tm=tm, src_tiles=src_tiles)
    return pl.pallas_call(
        kern,
        out_shape=jax.ShapeDtypeStruct((TOP_K * t, d), F32),
        grid=(nt,),
        in_specs=[pl.BlockSpec((1, 1, TOP_K * tm), lambda i: (i, 0, 0), memory_space=pltpu.SMEM)]
        + src_specs,
        out_specs=pl.BlockSpec(memory_space=pl.ANY),
        scratch_shapes=[pltpu.SemaphoreType.DMA(())],
        compiler_params=_cparams("arbitrary"),
        name="moe_dispatch",
    )(dest.reshape(nt, 1, TOP_K * tm), *hs)


def _expert_kernel(tile_ref, exp_ref, lo_ref, hi_ref, first_ref, x_ref, wg_ref, wu_ref, wd_ref,
                   o_ref, wg_s, wu_s, wd_s, cur_e):
    s = pl.program_id(0)
    lo, hi, e = lo_ref[s], hi_ref[s], exp_ref[s]

    @pl.when(s == 0)
    def _():
        cur_e[0] = -1

    @pl.when(hi > lo)
    def _():
        @pl.when(cur_e[0] != e)
        def _():
            wg_s[...] = wg_ref[0, 0].astype(BF16)
            wu_s[...] = wu_ref[0, 0].astype(BF16)
            wd_s[...] = wd_ref[0, 0].astype(BF16)
            cur_e[0] = e

        x = x_ref[...].astype(BF16)
        a = jnp.dot(x, wg_s[...], preferred_element_type=F32)
        u = jnp.dot(x, wu_s[...], preferred_element_type=F32)
        hid = (_silu(a) * u).astype(BF16)
        y = jnp.dot(hid, wd_s[...], preferred_element_type=F32)
        row = lax.broadcasted_iota(jnp.int32, (x.shape[0], 1), 0)
        mine = (row >= lo) & (row < hi)

        @pl.when(first_ref[s] == 1)
        def _():
            o_ref[...] = jnp.where(mine, y, 0.0)

        @pl.when(first_ref[s] == 0)
        def _():
            o_ref[...] = jnp.where(mine, y, o_ref[...])


def _expert_ffn(xs, plan, w_gate, w_up, w_down, layer):
    n_rows, d = xs.shape
    hid = w_gate.shape[3]
    n_steps = plan[0].shape[0]
    xmap = lambda s, tile, exp, lo, hi, first: (tile[s], 0)
    wmap = lambda s, tile, exp, lo, hi, first: (layer, exp[s], 0, 0)
    return pl.pallas_call(
        _expert_kernel,
        out_shape=jax.ShapeDtypeStruct((n_rows, d), F32),
        grid_spec=pltpu.PrefetchScalarGridSpec(
            num_scalar_prefetch=5,
            grid=(n_steps,),
            in_specs=[pl.BlockSpec((MOE_BLOCK, d), xmap),
                      pl.BlockSpec((1, 1, d, hid), wmap),
                      pl.BlockSpec((1, 1, d, hid), wmap),
                      pl.BlockSpec((1, 1, hid, d), wmap)],
            out_specs=pl.BlockSpec((MOE_BLOCK, d), xmap),
            scratch_shapes=[pltpu.VMEM((d, hid), BF16), pltpu.VMEM((d, hid), BF16),
                            pltpu.VMEM((hid, d), BF16), pltpu.SMEM((1,), jnp.int32)]),
        compiler_params=_cparams("arbitrary"),
        name="moe_experts",
    )(*plan, xs, w_gate, w_up, w_down)


def _combine_kernel(pos_ref, posn_ref, ys_hbm, meta_ref, lat_ref, g2_ref, nw_ref, o_ref, buf, sem,
                    *, final, tc):
    i = pl.program_id(0)
    nt = pl.num_programs(0)
    slot = i % 2
    n = TOP_K * tc

    def issue(idx_ref, sl):
        def body(r, c):
            pltpu.make_async_copy(ys_hbm.at[pl.ds(idx_ref[0, 0, r], 1), :],
                                  buf.at[sl, pl.ds(r, 1), :], sem.at[sl]).start()
            return c
        lax.fori_loop(0, n, body, 0, unroll=DMA_UNROLL)

    @pl.when(i == 0)
    def _():
        issue(pos_ref, 0)

    @pl.when(i + 1 < nt)
    def _():
        issue(posn_ref, 1 - slot)

    pltpu.make_async_copy(ys_hbm.at[pl.ds(0, n), :], buf.at[slot], sem.at[slot]).wait()
    gk = meta_ref[:, META_GATE:META_GATE + TOP_K]
    y = gk[:, 0:1] * buf[slot, 0:tc, :] + gk[:, 1:2] * buf[slot, tc:2 * tc, :]
    x = lat_ref[...] + g2_ref[0] * y
    if final:
        x = x * lax.rsqrt(jnp.mean(x * x, axis=-1, keepdims=True) + EPS) * nw_ref[...]
    o_ref[...] = x


def _moe_combine(ys, pos, meta, lat, g2, nw, *, seg_rows, final):
    t, d = lat.shape
    tc = 128
    nt = t // tc
    n = TOP_K * tc
    kern = functools.partial(_combine_kernel, final=final, tc=tc)
    posb = pos.reshape(nt, tc, TOP_K).transpose(0, 2, 1).reshape(nt, 1, n)
    return pl.pallas_call(
        kern,
        out_shape=jax.ShapeDtypeStruct((t, d), F32),
        grid=(nt,),
        in_specs=[pl.BlockSpec((1, 1, n), lambda i: (i, 0, 0), memory_space=pltpu.SMEM),
                  pl.BlockSpec((1, 1, n), lambda i: (jnp.minimum(i + 1, nt - 1), 0, 0),
                               memory_space=pltpu.SMEM),
                  pl.BlockSpec(memory_space=pl.ANY),
                  pl.BlockSpec((tc, LANES), lambda i: (i, 0)),
                  pl.BlockSpec((tc, d), lambda i: (i, 0)),
                  pl.BlockSpec((1, 1, d), lambda i: ((i * tc) // seg_rows, 0, 0)),
                  pl.BlockSpec((1, d), lambda i: (0, 0))],
        out_specs=pl.BlockSpec((tc, d), lambda i: (i, 0)),
        scratch_shapes=[pltpu.VMEM((2, n, d), F32), pltpu.SemaphoreType.DMA((2,))],
        compiler_params=_cparams("arbitrary"),
        name="moe_combine",
    )(posb, posb, ys, meta, lat, g2, nw.reshape(1, d))


CONV_HALO = 16


def _conv_kernel(prev_ref, cur_ref, next_ref, w_ref, b_ref, o_ref, *, seq, rblk):
    i = pl.program_id(0)
    first = (i * rblk) % seq == 0
    last = ((i + 1) * rblk) % seq == 0
    half = CONV_HALO // 2
    pv = jnp.where(first, 0.0, prev_ref[...].astype(F32))[half:, :]
    nx = jnp.where(last, 0.0, next_ref[...].astype(F32))[:half, :]
    xm = jnp.concatenate([pv, cur_ref[...].astype(F32), nx], axis=0)
    n = rblk + 2 * half
    acc = jnp.zeros((rblk, xm.shape[1]), F32) + b_ref[...]
    for k in range(SSD_CONV):
        sh = (SSD_CONV // 2 - k) % n
        xs = xm if sh == 0 else pltpu.roll(xm, sh, axis=0)
        acc = acc + w_ref[k:k + 1, :] * xs[half:half + rblk, :]
    o_ref[...] = _silu(acc).astype(o_ref.dtype)


def _conv_silu(zx, conv_w, conv_b, *, seq, col_off):
    t = zx.shape[0]
    c = conv_w.shape[1]
    rblk, tn = min(512, seq), 1024
    assert seq % rblk == 0 and c % tn == 0 and col_off % tn == 0
    cb = col_off // tn
    hb = rblk // CONV_HALO
    nhb = t // CONV_HALO
    kern = functools.partial(_conv_kernel, seq=seq, rblk=rblk)
    wpad = jnp.zeros((SUBLANES, c), F32).at[:SSD_CONV].set(conv_w.astype(F32))
    return pl.pallas_call(
        kern,
        out_shape=jax.ShapeDtypeStruct((t, c), BF16),
        grid=(t // rblk, c // tn),
        in_specs=[pl.BlockSpec((CONV_HALO, tn), lambda i, j: (jnp.maximum(i * hb - 1, 0), cb + j)),
                  pl.BlockSpec((rblk, tn), lambda i, j: (i, cb + j)),
                  pl.BlockSpec((CONV_HALO, tn), lambda i, j: (jnp.minimum((i + 1) * hb, nhb - 1), cb + j)),
                  pl.BlockSpec((SUBLANES, tn), lambda i, j: (0, j)),
                  pl.BlockSpec((1, tn), lambda i, j: (0, j))],
        out_specs=pl.BlockSpec((rblk, tn), lambda i, j: (i, j)),
        compiler_params=_cparams("parallel", "parallel"),
        name="ssd_conv_silu",
    )(zx, zx, zx, wpad, conv_b.astype(F32).reshape(1, c))


def _softplus(x):
    return jnp.maximum(x, 0.0) + jnp.log1p(jnp.exp(-jnp.abs(x)))


def _ssd_kernel(*refs, reverse, second, nchunk, nblk, groups):
    it = iter(refs)
    x_ref, b_ref, c_ref, dt_ref, prm_ref, s0_ref = (next(it) for _ in range(6))
    if second:
        yprev_ref, dsk_ref = next(it), next(it)
    y_ref, sfin_ref, s_scr = next(it), next(it), next(it)
    q = SSD_CHUNK
    nst = SSD_STATE
    gw = x_ref.shape[1] // groups
    pw = 2 * SSD_HEAD_DIM
    s = pl.program_id(1)

    @pl.when(s == 0)
    def _():
        s_scr[...] = s0_ref[0]

    ii = lax.broadcasted_iota(jnp.int32, (q, q), 0)
    jj = lax.broadcasted_iota(jnp.int32, (q, q), 1)
    tri = (jj >= ii) if reverse else (ii >= jj)
    tri_bf = tri.astype(BF16)
    lo_half = lax.broadcasted_iota(jnp.int32, (q, pw), 1) < SSD_HEAD_DIM
    a_neg = prm_ref[0:1, :]
    dtb = prm_ref[1:2, :]

    def chunk(ci, carry):
        c = (nchunk - 1 - ci) if reverse else ci
        rows = pl.ds(pl.multiple_of(c * q, q), q)
        dt = _softplus(dt_ref[rows, :] + dtb)
        da = dt * a_neg
        hi = da.astype(BF16)
        r1 = da - hi.astype(F32)
        mid = r1.astype(BF16)
        lo = (r1 - mid.astype(F32)).astype(BF16)
        acum = (jnp.dot(tri_bf, hi, preferred_element_type=F32)
                + jnp.dot(tri_bf, mid, preferred_element_type=F32)
                + jnp.dot(tri_bf, lo, preferred_element_type=F32))
        acum_t = acum.T
        dt_t = dt.T
        total = acum[0:1, :] if reverse else acum[q - 1:q, :]
        ea = jnp.exp(acum)
        fw = jnp.exp(total - acum) * dt
        etot = jnp.exp(total)
        for g in range(groups):
            bg = b_ref[rows, g * nst:(g + 1) * nst]
            cg = c_ref[rows, g * nst:(g + 1) * nst]
            cb = lax.dot_general(cg, bg, _NT, preferred_element_type=F32)
            sg = s_scr[g]
            yint = jnp.dot(cg, sg.astype(BF16), preferred_element_type=F32)
            ys, xws, decs = [], [], []
            for pr in range(gw // pw):
                h1 = g * (gw // SSD_HEAD_DIM) + 2 * pr
                h2 = h1 + 1
                ws = []
                for h in (h1, h2):
                    sgm = acum[:, h:h + 1] - acum_t[h:h + 1, :]
                    lm = jnp.where(tri, jnp.exp(sgm), 0.0)
                    ws.append((cb * lm * dt_t[h:h + 1, :]).astype(BF16))
                wp = jnp.concatenate(ws, axis=1)
                xp = x_ref[rows, g * gw + pr * pw:g * gw + (pr + 1) * pw].astype(F32)
                rhs = jnp.concatenate([jnp.where(lo_half, xp, 0.0), jnp.where(lo_half, 0.0, xp)],
                                      axis=0).astype(BF16)
                yp = jnp.dot(wp, rhs, preferred_element_type=F32)
                yp = yp + yint[:, pr * pw:(pr + 1) * pw] * jnp.where(lo_half, ea[:, h1:h1 + 1], ea[:, h2:h2 + 1])
                xws.append((xp * jnp.where(lo_half, fw[:, h1:h1 + 1], fw[:, h2:h2 + 1])).astype(BF16))
                decs.append(jnp.where(lo_half[0:1, :], etot[:, h1:h1 + 1], etot[:, h2:h2 + 1]))
                if second:
                    cols = slice(g * gw + pr * pw, g * gw + (pr + 1) * pw)
                    yp = yp + yprev_ref[rows, cols] + dsk_ref[:, cols] * xp
                ys.append(yp)
            xw = jnp.concatenate(xws, axis=1)
            upd = lax.dot_general(bg, xw, _TN, preferred_element_type=F32)
            s_scr[g] = jnp.concatenate(decs, axis=1) * sg + upd
            y_ref[rows, g * gw:(g + 1) * gw] = jnp.concatenate(ys, axis=1).astype(y_ref.dtype)
        return carry

    lax.fori_loop(0, nchunk, chunk, 0)

    @pl.when(s == nblk - 1)
    def _():
        sfin_ref[0] = s_scr[...]


def _ssd_pass(act, dt, prm, s0, *, batch, seq, reverse, yprev=None, dskip=None):
    groups, nst = SSD_GROUPS, SSD_STATE
    inner = act.shape[1] - 2 * groups * nst
    rblk = min(256, seq)
    nblk = seq // rblk
    second = yprev is not None
    bc = groups * nst
    rmap = (lambda b, s: (b * nblk + nblk - 1 - s, 0)) if reverse else (lambda b, s: (b * nblk + s, 0))
    cmap = lambda k: ((lambda b, s: (b * nblk + nblk - 1 - s, k)) if reverse
                      else (lambda b, s: (b * nblk + s, k)))
    st_spec = pl.BlockSpec((1, groups, nst, inner // groups), lambda b, s: (b, 0, 0, 0))
    args = [act, act, act, dt, prm, s0]
    in_specs = [pl.BlockSpec((rblk, inner), rmap),
                pl.BlockSpec((rblk, bc), cmap(inner // bc)),
                pl.BlockSpec((rblk, bc), cmap(inner // bc + 1)),
                pl.BlockSpec((rblk, dt.shape[1]), rmap),
                pl.BlockSpec(prm.shape, lambda b, s: (0, 0)),
                st_spec]
    if second:
        args += [yprev, dskip]
        in_specs += [pl.BlockSpec((rblk, inner), rmap), pl.BlockSpec((1, inner), lambda b, s: (0, 0))]
    kern = functools.partial(_ssd_kernel, reverse=reverse, second=second, nchunk=rblk // SSD_CHUNK,
                             nblk=nblk, groups=groups)
    return pl.pallas_call(
        kern,
        out_shape=[jax.ShapeDtypeStruct((batch * seq, inner), BF16 if second else F32),
                   jax.ShapeDtypeStruct((batch, groups, nst, inner // groups), F32)],
        grid=(batch, nblk),
        in_specs=in_specs,
        out_specs=[pl.BlockSpec((rblk, inner), rmap), st_spec],
        scratch_shapes=[pltpu.VMEM((groups, nst, inner // groups), F32)],
        compiler_params=_cparams("parallel", "arbitrary"),
        name="ssd_scan_bwd" if reverse else "ssd_scan_fwd",
    )(*args)


def _rope_tables(seq):
    pos = jnp.arange(seq)
    rows = (pos // GRID_W).astype(F32)
    cols = (pos % GRID_W).astype(F32)
    half = HEAD_DIM // 2
    inv = ROPE_BASE ** (-jnp.arange(0, half, 2, dtype=F32) / half)
    ang = jnp.concatenate([rows[:, None] * inv, cols[:, None] * inv], axis=-1)
    cos, sin = jnp.cos(ang), jnp.sin(ang)
    return jnp.concatenate([cos, cos], axis=-1), jnp.concatenate([-sin, sin], axis=-1)


def _deinterleave_perm(n_cols, n_rot_cols):
    within = np.concatenate([np.arange(0, HEAD_DIM, 2), np.arange(1, HEAD_DIM, 2)])
    perm = np.arange(n_cols)
    for h in range(n_rot_cols // HEAD_DIM):
        perm[h * HEAD_DIM:(h + 1) * HEAD_DIM] = h * HEAD_DIM + within
    return perm


def _moe_layer(hs, logits, router_bias, w_gate, w_up, w_down, layer):
    meta, cnt = _router(logits, router_bias)
    dest, plan = _moe_plan(meta, cnt)
    xs = _moe_dispatch(hs, dest)
    ys = _expert_ffn(xs, plan, w_gate, w_up, w_down, layer)
    return ys, dest, meta


def kernel(x, c, ctx, c_ctx, w_mod, b_mod, norm_mix, norm_ffn, norm_final, w_in_even, w_out_even,
           ret_decay, na_rpb, w_in_odd, conv_w, conv_b, dt_bias, a_log, d_skip, ssd_norm, w_out_odd,
           w_router, router_bias, w_gate, w_up, w_down):
    batch, seq, d = x.shape
    ctx_len = ctx.shape[1]
    depth = w_mod.shape[0]
    assert batch + 1 <= SUBLANES
    tl, tc = batch * seq, batch * ctx_len

    cvec = jnp.zeros((SUBLANES, d), F32).at[0].set(c_ctx).at[1:1 + batch].set(c)
    mod = _modulation(cvec, w_mod, b_mod)

    def mod_vecs(i, k):
        m = mod[i, :, k * d:(k + 1) * d]
        return m[1:1 + batch].reshape(batch, 1, d), m[0:1].reshape(1, 1, d)

    wr = jnp.zeros((d, LANES), F32).at[:, :N_EXPERTS].set(w_router.astype(F32))
    wrh, wrl = _split_bf16(wr)

    lat = x.reshape(tl, d)
    cx = ctx.reshape(tc, d)
    out = None
    for i in range(depth):
        last = i == depth - 1
        j = i // 2
        (sh1, csh1), (sc1, csc1), (g1, cg1) = mod_vecs(i, 0), mod_vecs(i, 1), mod_vecs(i, 2)
        (sh2, csh2), (sc2, csc2), (g2, cg2) = mod_vecs(i, 3), mod_vecs(i, 4), mod_vecs(i, 5)
        if i % 2 == 0:
            ret_heads = w_out_even.shape[1] // (2 * HEAD_DIM)
            rot_cols = 2 * ret_heads * HEAD_DIM
            w_in = w_in_even[j][:, _deinterleave_perm(w_in_even.shape[2], rot_cols)].astype(BF16)
            cos, sin = _rope_tables(seq)
            ones, zeros = jnp.ones((tc, HEAD_DIM), F32), jnp.zeros((tc, HEAD_DIM), F32)
            proj_l, = _inproj(lat, norm_mix[i], sh1, sc1, w_in, seg_rows=seq, tm=1024, tn=512,
                              even=True, cos=cos, sin=sin)
            proj_c, = _inproj(cx, norm_mix[i], csh1, csc1, w_in, seg_rows=tc, tm=tc, tn=512,
                              even=True, cos=ones, sin=zeros)
            tabs = _retention_tables(ret_decay[j])
            zero = jnp.zeros((batch, ret_heads, HEAD_DIM, HEAD_DIM), F32)
            ret_c, s_f, s_b = _retention(proj_c, tabs, zero, zero, batch=batch, seq=ctx_len, rblk=ctx_len)
            ret_l, _, _ = _retention(proj_l, tabs, s_f, s_b, batch=batch, seq=seq, rblk=1024)
            bias = _na_bias_tables(na_rpb[j], seq // GRID_W)
            na_l = _neighbourhood_attention(proj_l, proj_c, bias, batch=batch, seq=seq, ctx_len=ctx_len)
            w_out = w_out_even[j].astype(BF16)
            mix_l, ssd_l, mix_c, ssd_c = (ret_l, na_l), None, None, None
            if not last:
                mix_c = (ret_c, _ctx_attention(proj_c, batch=batch, ctx_len=ctx_len, heads=ret_heads))
            tm_out = 512
        else:
            inner = w_out_odd.shape[1]
            n_main = inner + conv_w.shape[2]
            w_in = w_in_odd[j]
            w_main = w_in[:, :n_main].astype(BF16)
            heads = w_in.shape[1] - n_main
            w_dt = jnp.zeros((d, LANES), F32).at[:, :heads].set(w_in[:, n_main:]).astype(BF16)
            zx_l, dt_l = _inproj(lat, norm_mix[i], sh1, sc1, w_main, seg_rows=seq, tm=1024, tn=512, wdt=w_dt)
            zx_c, dt_c = _inproj(cx, norm_mix[i], csh1, csc1, w_main, seg_rows=tc, tm=tc, tn=512, wdt=w_dt)
            act_l = _conv_silu(zx_l, conv_w[j], conv_b[j], seq=seq, col_off=inner)
            act_c = _conv_silu(zx_c, conv_w[j], conv_b[j], seq=ctx_len, col_off=inner)
            a_neg = -jnp.exp(a_log[j].astype(F32))

            def prm(dr):
                return (jnp.zeros((SUBLANES, LANES), F32).at[0, :heads].set(a_neg[dr])
                        .at[1, :heads].set(dt_bias[j, dr].astype(F32)))

            dsk = jnp.repeat((d_skip[j, 0] + d_skip[j, 1]).astype(F32), inner // heads)[None, :]
            zero = jnp.zeros((batch, SSD_GROUPS, SSD_STATE, inner // SSD_GROUPS), F32)
            yf_c, s_f = _ssd_pass(act_c, dt_c, prm(0), zero, batch=batch, seq=ctx_len, reverse=False)
            y_c, s_b = _ssd_pass(act_c, dt_c, prm(1), zero, batch=batch, seq=ctx_len, reverse=True,
                                 yprev=yf_c, dskip=dsk)
            yf_l, _ = _ssd_pass(act_l, dt_l, prm(0), s_f, batch=batch, seq=seq, reverse=False)
            y_l, _ = _ssd_pass(act_l, dt_l, prm(1), s_b, batch=batch, seq=seq, reverse=True,
                               yprev=yf_l, dskip=dsk)
            w_out = w_out_odd[j].astype(BF16)
            mix_l, ssd_l = None, (y_l, zx_l, ssd_norm[j].astype(F32))
            mix_c, ssd_c = None, (None if last else (y_c, zx_c, ssd_norm[j].astype(F32)))
            tm_out = 256

        nwf = norm_ffn[i].astype(F32)
        lat2, h_l, lg_l = _outproj(mix_l, w_out, lat, g1, nwf, sh2, sc2, wrh, wrl, seg_rows=seq,
                                   tm=tm_out, ssd=ssd_l)
        if last:
            ys, pos, meta = _moe_layer([h_l], lg_l, router_bias, w_gate, w_up, w_down, i)
            out = _moe_combine(ys, pos, meta, lat2, g2, norm_final.astype(F32), seg_rows=seq, final=True)
        else:
            cx2, h_c, lg_c = _outproj(mix_c, w_out, cx, cg1, nwf, csh2, csc2, wrh, wrl, seg_rows=tc,
                                      tm=tm_out, ssd=ssd_c)
            ys, pos, meta = _moe_layer([h_c, h_l], jnp.concatenate([lg_c, lg_l], axis=0), router_bias,
                                       w_gate, w_up, w_down, i)
            ones = jnp.ones((d,), F32)
            cx = _moe_combine(ys, pos[:tc], meta[:tc], cx2, cg2, ones, seg_rows=tc, final=False)
            lat = _moe_combine(ys, pos[tc:], meta[tc:], lat2, g2, ones, seg_rows=seq, final=False)
    return out.reshape(batch, seq, d)
```

```python
import functools
import math

import numpy as np
import jax
import jax.numpy as jnp
from jax import lax
from jax.experimental import pallas as pl
from jax.experimental.pallas import tpu as pltpu

GRID_W = 64
EPS = 1e-6
HEAD_DIM = 128
ROPE_BASE = 10000.0
RET_CHUNK = 128
NA_ROWS = 8
NA_COLS = 16
SSD_HEAD_DIM = 64
SSD_GROUPS = 8
SSD_STATE = 128
SSD_CONV = 5
SSD_CHUNK = 128
N_EXPERTS = 32
N_GROUPS = 8
EXPERTS_PER_GROUP = N_EXPERTS // N_GROUPS
TOP_K = 2
MOE_BLOCK = 512

LANES = 128
SUBLANES = 8
VMEM_LIMIT_BYTES = 56 * 1024 * 1024

NEG_INF = -1e30
F32 = jnp.float32
BF16 = jnp.bfloat16


def _cparams(*sem):
    return pltpu.CompilerParams(dimension_semantics=sem, vmem_limit_bytes=VMEM_LIMIT_BYTES)


def _silu(x):
    return x * (1.0 / (1.0 + jnp.exp(-x)))


def _mod_kernel(c_ref, w_ref, b_ref, o_ref):
    c = _silu(c_ref[...]).astype(BF16)
    o_ref[0] = jnp.dot(c, w_ref[0].astype(BF16), preferred_element_type=F32) + b_ref[0]


def _modulation(cvec, w_mod, b_mod):
    depth, d, n = w_mod.shape
    tn = 1024
    return pl.pallas_call(
        _mod_kernel,
        out_shape=jax.ShapeDtypeStruct((depth, SUBLANES, n), F32),
        grid=(depth, n // tn),
        in_specs=[
            pl.BlockSpec((SUBLANES, d), lambda i, j: (0, 0)),
            pl.BlockSpec((1, d, tn), lambda i, j: (i, 0, j)),
            pl.BlockSpec((1, 1, tn), lambda i, j: (i, 0, j)),
        ],
        out_specs=pl.BlockSpec((1, SUBLANES, tn), lambda i, j: (i, 0, j)),
        compiler_params=_cparams("parallel", "parallel"),
        name="adaln_modulation",
    )(cvec, w_mod, b_mod.reshape(depth, 1, n))


def _rms_modulate(x, nw, sh, sc):
    r = lax.rsqrt(jnp.mean(x * x, axis=-1, keepdims=True) + EPS)
    return (x * r * nw) * (1.0 + sc) + sh


def _rope(t, cos, sin):
    return t * cos + pltpu.roll(t, HEAD_DIM // 2, axis=1) * sin


def _inproj_kernel(*refs, even, has_dt, tn):
    it = iter(refs)
    x_ref, nw_ref, sh_ref, sc_ref, w_ref = (next(it) for _ in range(5))
    if even:
        cos_ref, sin_ref = next(it), next(it)
    if has_dt:
        wdt_ref = next(it)
    o_ref = next(it)
    if has_dt:
        dt_ref = next(it)
    h_scr = next(it)

    j = pl.program_id(1)

    @pl.when(j == 0)
    def _():
        h = _rms_modulate(x_ref[...], nw_ref[...], sh_ref[0], sc_ref[0]).astype(BF16)
        h_scr[...] = h
        if has_dt:
            dt_ref[...] = jnp.dot(h, wdt_ref[...], preferred_element_type=F32)

    acc = jnp.dot(h_scr[...], w_ref[...], preferred_element_type=F32)

    if not even:
        o_ref[...] = acc.astype(o_ref.dtype)
        return

    heads_per_tile = tn // HEAD_DIM
    tiles_per_kilo = 1024 // tn
    seg = j // tiles_per_kilo
    scale = HEAD_DIM ** -0.5

    @pl.when(seg <= 1)
    def _():
        mul = jnp.where(seg == 1, scale, 1.0).astype(F32)
        cos, sin = cos_ref[...], sin_ref[...]
        for hh in range(heads_per_tile):
            sl = slice(hh * HEAD_DIM, (hh + 1) * HEAD_DIM)
            o_ref[:, sl] = (_rope(acc[:, sl], cos, sin) * mul).astype(o_ref.dtype)

    @pl.when(seg == 4)
    def _():
        o_ref[...] = (acc * scale).astype(o_ref.dtype)

    @pl.when((seg == 2) | (seg == 3) | (seg >= 5))
    def _():
        o_ref[...] = acc.astype(o_ref.dtype)


def _seg_spec(d, seg_rows, tm):
    return pl.BlockSpec((1, 1, d), lambda i, j: ((i * tm) // seg_rows, 0, 0))


def _inproj(x, nw, sh, sc, w, *, seg_rows, tm, tn, even=False, cos=None, sin=None, wdt=None):
    t, d = x.shape
    n = w.shape[1]
    tm = min(tm, t)
    assert t % tm == 0 and n % tn == 0 and seg_rows % tm == 0
    args, in_specs = [x], [pl.BlockSpec((tm, d), lambda i, j: (i, 0))]
    args += [nw.reshape(1, d), sh, sc, w]
    in_specs += [pl.BlockSpec((1, d), lambda i, j: (0, 0)), _seg_spec(d, seg_rows, tm),
                 _seg_spec(d, seg_rows, tm), pl.BlockSpec((d, tn), lambda i, j: (0, j))]
    if even:
        lblk = cos.shape[0] // tm
        args += [cos, sin]
        in_specs += [pl.BlockSpec((tm, HEAD_DIM), lambda i, j: (i % lblk, 0))] * 2
    if wdt is not None:
        args.append(wdt)
        in_specs.append(pl.BlockSpec(wdt.shape, lambda i, j: (0, 0)))
    out_shape = [jax.ShapeDtypeStruct((t, n), BF16)]
    out_specs = [pl.BlockSpec((tm, tn), lambda i, j: (i, j))]
    if wdt is not None:
        out_shape.append(jax.ShapeDtypeStruct((t, wdt.shape[1]), F32))
        out_specs.append(pl.BlockSpec((tm, wdt.shape[1]), lambda i, j: (i, 0)))
    kern = functools.partial(_inproj_kernel, even=even, has_dt=wdt is not None, tn=tn)
    return pl.pallas_call(
        kern,
        out_shape=out_shape,
        grid=(t // tm, n // tn),
        in_specs=in_specs,
        out_specs=out_specs,
        scratch_shapes=[pltpu.VMEM((tm, d), BF16)],
        compiler_params=_cparams("parallel", "arbitrary"),
        name="inproj_even" if even else "inproj_odd",
    )(*args)


RET_UNROLL = 4
_TN = (((0,), (0,)), ((), ()))
_NT = (((1,), (1,)), ((), ()))


def _retention_kernel(q_ref, k_ref, v_ref, g_ref, tab_ref, sf0_ref, sb0_ref,
                      o_ref, sf_ref, sb_ref, sb_store, s_run, *, nchunk, nblk):
    p = pl.program_id(2)
    s = pl.program_id(3)
    q = RET_CHUNK
    dcomb = tab_ref[0, 0]

    @pl.when(p == 0)
    def _():
        @pl.when(s == 0)
        def _():
            s_run[...] = sb0_ref[0, 0]

        blk = nblk - 1 - s
        kb, cb = tab_ref[0, 4], tab_ref[0, 6]

        def body(ci, carry):
            c = nchunk - 1 - ci
            rows = pl.ds(pl.multiple_of(c * q, q), q)
            st = s_run[...]
            sb_store[blk * nchunk + c] = st.astype(BF16)
            kd = (k_ref[rows, :].astype(F32) * kb).astype(BF16)
            upd = lax.dot_general(kd, v_ref[rows, :], _TN, preferred_element_type=F32)
            s_run[...] = cb * st + upd
            return carry

        lax.fori_loop(0, nchunk, body, 0, unroll=min(nchunk, RET_UNROLL))

        @pl.when(s == nblk - 1)
        def _():
            sb_ref[0, 0] = s_run[...]

    @pl.when(p == 1)
    def _():
        @pl.when(s == 0)
        def _():
            s_run[...] = sf0_ref[0, 0]

        qf, qb, kf, cf = tab_ref[0, 1], tab_ref[0, 2], tab_ref[0, 3], tab_ref[0, 5]

        def body(c, carry):
            rows = pl.ds(pl.multiple_of(c * q, q), q)
            qc, kc, vc = q_ref[rows, :], k_ref[rows, :], v_ref[rows, :]
            qc32, kc32 = qc.astype(F32), kc.astype(F32)
            st = s_run[...]
            sc = lax.dot_general(qc, kc, _NT, preferred_element_type=F32) * dcomb
            o = jnp.dot(sc.astype(BF16), vc, preferred_element_type=F32)
            qcat = jnp.concatenate([(qc32 * qf).astype(BF16), (qc32 * qb).astype(BF16)], axis=1)
            scat = jnp.concatenate([st.astype(BF16), sb_store[s * nchunk + c]], axis=0)
            o = o + jnp.dot(qcat, scat, preferred_element_type=F32)
            upd = lax.dot_general((kc32 * kf).astype(BF16), vc, _TN, preferred_element_type=F32)
            s_run[...] = cf * st + upd
            o = o * lax.rsqrt(jnp.mean(o * o, axis=-1, keepdims=True) + EPS)
            o_ref[rows, :] = (o * _silu(g_ref[rows, :].astype(F32))).astype(o_ref.dtype)
            return carry

        lax.fori_loop(0, nchunk, body, 0, unroll=min(nchunk, RET_UNROLL))

        @pl.when(s == nblk - 1)
        def _():
            sf_ref[0, 0] = s_run[...]


def _retention_tables(ret_logit):
    q = RET_CHUNK
    lg = jax.nn.log_sigmoid(ret_logit.astype(F32))
    lf, lb = lg[0][:, None, None], lg[1][:, None, None]
    i = jnp.arange(q, dtype=F32)[None, :, None]
    j = jnp.arange(q, dtype=F32)[None, None, :]
    ones = jnp.ones((1, 1, q), F32)
    dcomb = (jnp.where(i >= j, jnp.exp(lf * jnp.maximum(i - j, 0.0)), 0.0)
             + jnp.where(j >= i, jnp.exp(lb * jnp.maximum(j - i, 0.0)), 0.0))
    qf = jnp.exp(lf * (i + 1.0)) * ones
    qb = jnp.exp(lb * (q - i)) * ones
    kf = jnp.exp(lf * (q - 1.0 - i)) * ones
    kb = jnp.exp(lb * i) * ones
    cf = jnp.exp(lf * q) * jnp.ones((1, q, q), F32)
    cb = jnp.exp(lb * q) * jnp.ones((1, q, q), F32)
    return jnp.stack([dcomb, qf, qb, kf, kb, cf, cb], axis=1)


def _retention(proj, tabs, sf0, sb0, *, batch, seq, rblk):
    heads = tabs.shape[0]
    rblk = min(rblk, seq)
    nblk = seq // rblk
    nchunk = rblk // RET_CHUNK
    hd = HEAD_DIM

    def qmap(off):
        return lambda b, h, p, s: (b * nblk + p * s, off + h)

    def kvmap(off):
        return lambda b, h, p, s: (b * nblk + s + (1 - p) * (nblk - 1 - 2 * s), off + h)

    st_spec = pl.BlockSpec((1, 1, hd, hd), lambda b, h, p, s: (b, h, 0, 0))
    kern = functools.partial(_retention_kernel, nchunk=nchunk, nblk=nblk)
    return pl.pallas_call(
        kern,
        out_shape=[jax.ShapeDtypeStruct((batch * seq, heads * hd), BF16),
                   jax.ShapeDtypeStruct((batch, heads, hd, hd), F32),
                   jax.ShapeDtypeStruct((batch, heads, hd, hd), F32)],
        grid=(batch, heads, 2, nblk),
        in_specs=[pl.BlockSpec((rblk, hd), qmap(0)),
                  pl.BlockSpec((rblk, hd), kvmap(heads)),
                  pl.BlockSpec((rblk, hd), kvmap(2 * heads)),
                  pl.BlockSpec((rblk, hd), qmap(3 * heads)),
                  pl.BlockSpec((1, 7, RET_CHUNK, RET_CHUNK), lambda b, h, p, s: (h, 0, 0, 0)),
                  st_spec, st_spec],
        out_specs=[pl.BlockSpec((rblk, hd), qmap(0)), st_spec, st_spec],
        scratch_shapes=[pltpu.VMEM((seq // RET_CHUNK, hd, hd), BF16), pltpu.VMEM((hd, hd), F32)],
        compiler_params=_cparams("parallel", "parallel", "arbitrary", "arbitrary"),
        name="retention",
    )(proj, proj, proj, proj, tabs, sf0, sb0)


NA_QROWS = 8
NA_WROWS = 16
NA_KBLK = 4
NA_QSPLIT = 2


def _na_kernel(q_ref, k0, k1, k2, k3, v0, v1, v2, v3, kc_ref, vc_ref, bias_ref, o_ref):
    kb = NA_KBLK * GRID_W
    nq = q_ref.shape[0] // NA_QSPLIT
    for part in range(NA_QSPLIT):
        rows = slice(part * nq, (part + 1) * nq)
        q = q_ref[rows, :]
        scores = []
        for i, kr in enumerate((k0, k1, k2, k3)):
            sc = lax.dot_general(q, kr[...], _NT, preferred_element_type=F32)
            scores.append(sc + bias_ref[0, 0, rows, i * kb:(i + 1) * kb])
        scores.append(lax.dot_general(q, kc_ref[...], _NT, preferred_element_type=F32))
        m = scores[0].max(axis=-1, keepdims=True)
        for sc in scores[1:]:
            m = jnp.maximum(m, sc.max(axis=-1, keepdims=True))
        acc = jnp.zeros((nq, HEAD_DIM), F32)
        l = jnp.zeros((nq, 1), F32)
        for sc, vr in zip(scores, (v0, v1, v2, v3, vc_ref)):
            pr = jnp.exp(sc - m)
            l = l + pr.sum(axis=-1, keepdims=True)
            acc = acc + jnp.dot(pr.astype(BF16), vr[...], preferred_element_type=F32)
        o_ref[rows, :] = (acc / l).astype(o_ref.dtype)


def _na_bias_tables(rpb, rows):
    nrb = rows // NA_QROWS
    nr, nc = 2 * NA_ROWS - 1, 2 * NA_COLS - 1
    hi = lax.Precision.HIGHEST
    qc = np.arange(GRID_W)[:, None]
    kc = np.arange(GRID_W)[None, :]
    c0 = np.clip(qc - NA_COLS // 2, 0, GRID_W - NA_COLS)
    col_valid = (kc >= c0) & (kc < c0 + NA_COLS)
    sel_c = (np.arange(nc)[:, None, None] == (kc - qc + NA_COLS - 1)[None]).astype(np.float32)
    toep = jnp.einsum('haj,jqk->haqk', rpb.astype(F32), sel_c, precision=hi)
    qr = np.arange(NA_QROWS)[:, None]
    kr = np.arange(NA_WROWS)[None, :]
    tabs = []
    for rb in (0, 1, nrb - 1):
        w0 = int(np.clip(rb * NA_QROWS - NA_ROWS // 2, 0, rows - NA_WROWS))
        r = rb * NA_QROWS + qr
        krow = w0 + kr
        r0 = np.clip(r - NA_ROWS // 2, 0, rows - NA_ROWS)
        row_valid = (krow >= r0) & (krow < r0 + NA_ROWS)
        sel_r = ((np.arange(nr)[None, None, :] == (krow - r + NA_ROWS - 1)[:, :, None])
                 & row_valid[:, :, None]).astype(np.float32)
        b = jnp.einsum('rka,haqc->hrqkc', sel_r, toep, precision=hi)
        valid = row_valid[:, None, :, None] & col_valid[None, :, None, :]
        b = jnp.where(valid[None], b, NEG_INF)
        tabs.append(b.reshape(rpb.shape[0], NA_QROWS * GRID_W, NA_WROWS * GRID_W))
    return jnp.stack(tabs, axis=1)


def _neighbourhood_attention(proj, proj_ctx, bias, *, batch, seq, ctx_len):
    heads = bias.shape[0]
    hd = HEAD_DIM
    qblk = NA_QROWS * GRID_W
    kblk = NA_KBLK * GRID_W
    nrb = seq // qblk
    nkb = seq // kblk
    assert seq // GRID_W >= NA_WROWS + NA_QROWS
    qoff, koff, voff = 4 * heads, 5 * heads, 6 * heads

    def kmap(off, i):
        def f(b, h, r):
            u0 = jnp.clip(2 * r - 1, 0, nkb - NA_WROWS // NA_KBLK)
            return (b * nkb + u0 + i, off + h)
        return f

    def bias_map(b, h, r):
        return (h, (r > 0).astype(jnp.int32) + (r == nrb - 1).astype(jnp.int32), 0, 0)

    kspecs = [pl.BlockSpec((kblk, hd), kmap(koff, i)) for i in range(4)]
    vspecs = [pl.BlockSpec((kblk, hd), kmap(voff, i)) for i in range(4)]
    return pl.pallas_call(
        _na_kernel,
        out_shape=jax.ShapeDtypeStruct((batch * seq, heads * hd), BF16),
        grid=(batch, heads, nrb),
        in_specs=[pl.BlockSpec((qblk, hd), lambda b, h, r: (b * nrb + r, qoff + h))] + kspecs + vspecs + [
            pl.BlockSpec((ctx_len, hd), lambda b, h, r: (b, koff + h)),
            pl.BlockSpec((ctx_len, hd), lambda b, h, r: (b, voff + h)),
            pl.BlockSpec((1, 1, qblk, NA_WROWS * GRID_W), bias_map)],
        out_specs=pl.BlockSpec((qblk, hd), lambda b, h, r: (b * nrb + r, h)),
        compiler_params=_cparams("parallel", "parallel", "arbitrary"),
        name="neighbourhood_attention",
    )(*([proj] * 9), proj_ctx, proj_ctx, bias)


def _ctx_attn_kernel(q_ref, k_ref, v_ref, o_ref):
    sc = lax.dot_general(q_ref[...], k_ref[...], _NT, preferred_element_type=F32)
    pr = jnp.exp(sc - sc.max(axis=-1, keepdims=True))
    acc = jnp.dot(pr.astype(BF16), v_ref[...], preferred_element_type=F32)
    o_ref[...] = (acc / pr.sum(axis=-1, keepdims=True)).astype(o_ref.dtype)


def _ctx_attention(proj_ctx, *, batch, ctx_len, heads):
    hd = HEAD_DIM
    return pl.pallas_call(
        _ctx_attn_kernel,
        out_shape=jax.ShapeDtypeStruct((batch * ctx_len, heads * hd), BF16),
        grid=(batch, heads),
        in_specs=[pl.BlockSpec((ctx_len, hd), lambda b, h: (b, 4 * heads + h)),
                  pl.BlockSpec((ctx_len, hd), lambda b, h: (b, 5 * heads + h)),
                  pl.BlockSpec((ctx_len, hd), lambda b, h: (b, 6 * heads + h))],
        out_specs=pl.BlockSpec((ctx_len, hd), lambda b, h: (b, h)),
        compiler_params=_cparams("parallel", "parallel"),
        name="context_attention",
    )(proj_ctx, proj_ctx, proj_ctx)


OUTPROJ_SPLIT = 2


def _split_bf16(x):
    hi = x.astype(BF16)
    return hi, (x - hi.astype(F32)).astype(BF16)


def _outproj_kernel(*refs, odd):
    it = iter(refs)
    if odd:
        y_ref, z_ref, nws_ref = next(it), next(it), next(it)
    else:
        mr_ref, mn_ref = next(it), next(it)
    w_ref, lat_ref, g1_ref, nwf_ref, sh_ref, sc_ref = (next(it) for _ in range(6))
    wrh_ref, wrl_ref = next(it), next(it)
    lat2_ref, h_ref, lg_ref = next(it), next(it), next(it)

    nr = lat_ref.shape[0] // OUTPROJ_SPLIT
    for part in range(OUTPROJ_SPLIT):
        rows = slice(part * nr, (part + 1) * nr)
        if odd:
            u = y_ref[rows, :].astype(F32) * _silu(z_ref[rows, :].astype(F32))
            u = u * lax.rsqrt(jnp.mean(u * u, axis=-1, keepdims=True) + EPS) * nws_ref[...]
            o = jnp.dot(u.astype(BF16), w_ref[...], preferred_element_type=F32)
        else:
            half = mr_ref.shape[1]
            o = (jnp.dot(mr_ref[rows, :], w_ref[0:half, :], preferred_element_type=F32)
                 + jnp.dot(mn_ref[rows, :], w_ref[half:2 * half, :], preferred_element_type=F32))
        lat2 = lat_ref[rows, :] + g1_ref[0] * o
        lat2_ref[rows, :] = lat2
        h = _rms_modulate(lat2, nwf_ref[...], sh_ref[0], sc_ref[0])
        h_ref[rows, :] = h
        hh, hl = _split_bf16(h)
        wrh = wrh_ref[...]
        lg_ref[rows, :] = (jnp.dot(hh, wrh, preferred_element_type=F32)
                           + jnp.dot(hl, wrh, preferred_element_type=F32)
                           + jnp.dot(hh, wrl_ref[...], preferred_element_type=F32))


def _outproj(mix, w, lat, g1, nwf, sh, sc, wrh, wrl, *, seg_rows, tm, ssd=None):
    t, d = lat.shape
    tm = min(tm, t)
    assert t % tm == 0 and seg_rows % tm == 0
    odd = ssd is not None
    row = lambda i: (i, 0)
    seg = pl.BlockSpec((1, 1, d), lambda i: ((i * tm) // seg_rows, 0, 0))
    const = lambda shape: pl.BlockSpec(shape, lambda i: (0,) * len(shape))
    if odd:
        y, zx, nws = ssd
        kin = y.shape[1]
        args = [y, zx, nws.reshape(1, kin)]
        in_specs = [pl.BlockSpec((tm, kin), row), pl.BlockSpec((tm, kin), row), const((1, kin))]
    else:
        mr, mn = mix
        args = [mr, mn]
        in_specs = [pl.BlockSpec((tm, mr.shape[1]), row), pl.BlockSpec((tm, mn.shape[1]), row)]
    args += [w, lat, g1, nwf.reshape(1, d), sh, sc, wrh, wrl]
    in_specs += [pl.BlockSpec(w.shape, lambda i: (0, 0), pipeline_mode=pl.Buffered(1)),
                 pl.BlockSpec((tm, d), row), seg, const((1, d)), seg, seg,
                 const(wrh.shape), const(wrl.shape)]
    nl = wrh.shape[1]
    return pl.pallas_call(
        functools.partial(_outproj_kernel, odd=odd),
        out_shape=[jax.ShapeDtypeStruct((t, d), F32),
                   jax.ShapeDtypeStruct((t, d), F32),
                   jax.ShapeDtypeStruct((t, nl), F32)],
        grid=(t // tm,),
        in_specs=in_specs,
        out_specs=[pl.BlockSpec((tm, d), row), pl.BlockSpec((tm, d), row), pl.BlockSpec((tm, nl), row)],
        compiler_params=_cparams("parallel"),
        name="outproj_odd" if odd else "outproj_even",
    )(*args)


ROUTER_TILE = 256
META_E, META_RANK, META_GATE = 0, 2, 4


def _router_kernel(lg_ref, rb_ref, meta_ref, cnt_ref, run_scr):
    tm = lg_ref.shape[0]
    i = pl.program_id(0)

    @pl.when(i == 0)
    def _():
        run_scr[...] = jnp.zeros_like(run_scr)

    big = 1e9
    lane = lax.broadcasted_iota(jnp.int32, (tm, LANES), 1)
    lanef = lane.astype(F32)
    scores = 1.0 / (1.0 + jnp.exp(-lg_ref[...]))
    biased = jnp.where(lane < N_EXPERTS, scores + rb_ref[...], NEG_INF)
    member = lane % EXPERTS_PER_GROUP
    best = jnp.full((tm, LANES), NEG_INF, F32)
    for k in range(1, EXPERTS_PER_GROUP):
        best = jnp.maximum(best, jnp.where(member >= k, biased + pltpu.roll(biased, k, axis=1), NEG_INF))
    grp = (lane // EXPERTS_PER_GROUP).astype(F32)
    top = best.max(axis=-1, keepdims=True)
    g_sel = jnp.where(best == top, grp, big).min(axis=-1, keepdims=True)
    v = jnp.where(grp == g_sel, biased, NEG_INF)
    i1 = jnp.where(v == v.max(axis=-1, keepdims=True), lanef, big).min(axis=-1, keepdims=True)
    v = jnp.where(lanef == i1, NEG_INF, v)
    i2 = jnp.where(v == v.max(axis=-1, keepdims=True), lanef, big).min(axis=-1, keepdims=True)
    oh1, oh2 = lanef == i1, lanef == i2
    s1 = jnp.where(oh1, scores, 0.0).sum(axis=-1, keepdims=True)
    s2 = jnp.where(oh2, scores, 0.0).sum(axis=-1, keepdims=True)
    den = s1 + s2
    oh = oh1.astype(F32) + oh2.astype(F32)
    earlier = (lax.broadcasted_iota(jnp.int32, (tm, tm), 0)
               > lax.broadcasted_iota(jnp.int32, (tm, tm), 1)).astype(BF16)
    base = jnp.dot(earlier, oh.astype(BF16), preferred_element_type=F32) + run_scr[0:1, :]
    r1 = jnp.where(oh1, base, 0.0).sum(axis=-1, keepdims=True)
    r2 = jnp.where(oh2, base, 0.0).sum(axis=-1, keepdims=True)
    total = run_scr[0:1, :] + oh.sum(axis=0, keepdims=True)
    run_scr[0:1, :] = total
    cnt_ref[...] = jnp.broadcast_to(total, cnt_ref.shape)
    meta = jnp.zeros((tm, LANES), F32)
    for ln, val in ((META_E, i1), (META_E + 1, i2), (META_RANK, r1), (META_RANK + 1, r2),
                    (META_GATE, s1 / den), (META_GATE + 1, s2 / den)):
        meta = jnp.where(lane == ln, val, meta)
    meta_ref[...] = meta


def _router(logits, router_bias):
    t = logits.shape[0]
    tm = ROUTER_TILE
    rb = jnp.zeros((1, LANES), F32).at[0, :N_EXPERTS].set(router_bias.astype(F32))
    return pl.pallas_call(
        _router_kernel,
        out_shape=[jax.ShapeDtypeStruct((t, LANES), F32), jax.ShapeDtypeStruct((SUBLANES, LANES), F32)],
        grid=(t // tm,),
        in_specs=[pl.BlockSpec((tm, LANES), lambda i: (i, 0)), pl.BlockSpec((1, LANES), lambda i: (0, 0))],
        out_specs=[pl.BlockSpec((tm, LANES), lambda i: (i, 0)),
                   pl.BlockSpec((SUBLANES, LANES), lambda i: (0, 0))],
        scratch_shapes=[pltpu.VMEM((SUBLANES, LANES), F32)],
        compiler_params=_cparams("arbitrary"),
        name="moe_router",
    )(logits, rb)


def _moe_plan(meta, cnt):
    t = meta.shape[0]
    blk = MOE_BLOCK
    counts = cnt[0, :N_EXPERTS].astype(jnp.int32)
    ends = jnp.cumsum(counts)
    offsets = ends - counts
    e = meta[:, META_E:META_E + TOP_K].astype(jnp.int32)
    rank = meta[:, META_RANK:META_RANK + TOP_K].astype(jnp.int32)
    onehot = e[..., None] == jnp.arange(N_EXPERTS, dtype=jnp.int32)
    dest = jnp.sum(jnp.where(onehot, offsets, 0), axis=-1) + rank
    n_tiles = t * TOP_K // blk
    tile_start = jnp.arange(n_tiles, dtype=jnp.int32) * blk
    def count_le(sorted_vals, query):
        return jnp.sum(sorted_vals[None, :] <= query[:, None], axis=-1).astype(jnp.int32)

    first_e = count_le(ends, tile_start)
    last_e = count_le(ends, tile_start + blk - 1)
    nsteps = last_e - first_e + 1
    step_end = jnp.cumsum(nsteps)
    step_start = step_end - nsteps
    s = jnp.arange(n_tiles + N_EXPERTS - 1, dtype=jnp.int32)
    valid = s < step_end[-1]
    tile_of = jnp.minimum(count_le(step_end, s), n_tiles - 1)
    exp_of = jnp.where(valid, first_e[tile_of] + s - step_start[tile_of], last_e[-1])
    exp_of = jnp.clip(exp_of, 0, N_EXPERTS - 1).astype(jnp.int32)
    lo = jnp.clip(offsets[exp_of] - tile_of * blk, 0, blk)
    hi = jnp.clip(ends[exp_of] - tile_of * blk, 0, blk)
    lo = jnp.where(valid, lo, 0).astype(jnp.int32)
    hi = jnp.where(valid, hi, 0).astype(jnp.int32)
    first = (valid & (s == step_start[tile_of])).astype(jnp.int32)
    return dest.astype(jnp.int32), (tile_of, exp_of, lo, hi, first)


DMA_UNROLL = 8


def _dispatch_kernel(*refs, tm, src_tiles):
    dest_ref, srcs, xs_hbm, sem = refs[0], refs[1:-2], refs[-2], refs[-1]
    i = pl.program_id(0)

    def scatter_from(src):
        def body(r, c):
            for k in range(TOP_K):
                pltpu.make_async_copy(src.at[pl.ds(r, 1), :],
                                      xs_hbm.at[pl.ds(dest_ref[0, 0, TOP_K * r + k], 1), :], sem).start()
            return c
        lax.fori_loop(0, tm, body, 0, unroll=DMA_UNROLL)
        for _ in range(TOP_K):
            pltpu.make_async_copy(src, xs_hbm.at[pl.ds(0, tm), :], sem).wait()

    first_tile = 0
    for src, ntile in zip(srcs, src_tiles):
        @pl.when((i >= first_tile) & (i < first_tile + ntile))
        def _(src=src):
            scatter_from(src)
        first_tile += ntile


def _moe_dispatch(hs, dest):
    d = hs[0].shape[1]
    t = dest.shape[0]
    tm = 256
    nt = t // tm
    src_tiles = tuple(h.shape[0] // tm for h in hs)
    assert sum(src_tiles) == nt and all(h.shape[0] % tm == 0 for h in hs)
    starts = np.cumsum((0,) + src_tiles[:-1])
    src_specs = [pl.BlockSpec((tm, d), lambda i, s=int(s), n=n: (jnp.clip(i - s, 0, n - 1), 0))
                 for s, n in zip(starts, src_tiles)]
    kern = functools.partial(_dispatch_kernel, tm=tm, src_tiles=src_tiles)
    return pl.pallas_call(
        kern,
        out_shape=jax.ShapeDtypeStruct((TOP_K * t, d), F32),
        grid=(nt,),
        in_specs=[pl.BlockSpec((1, 1, TOP_K * tm), lambda i: (i, 0, 0), memory_space=pltpu.SMEM)]
        + src_specs,
        out_specs=pl.BlockSpec(memory_space=pl.ANY),
        scratch_shapes=[pltpu.SemaphoreType.DMA(())],
        compiler_params=_cparams("arbitrary"),
        name="moe_dispatch",
    )(dest.reshape(nt, 1, TOP_K * tm), *hs)


def _expert_kernel(tile_ref, exp_ref, lo_ref, hi_ref, first_ref, x_ref, wg_ref, wu_ref, wd_ref,
                   o_ref, wg_s, wu_s, wd_s, cur_e):
    s = pl.program_id(0)
    lo, hi, e = lo_ref[s], hi_ref[s], exp_ref[s]

    @pl.when(s == 0)
    def _():
        cur_e[0] = -1

    @pl.when(hi > lo)
    def _():
        @pl.when(cur_e[0] != e)
        def _():
            wg_s[...] = wg_ref[0, 0].astype(BF16)
            wu_s[...] = wu_ref[0, 0].astype(BF16)
            wd_s[...] = wd_ref[0, 0].astype(BF16)
            cur_e[0] = e

        x = x_ref[...].astype(BF16)
        a = jnp.dot(x, wg_s[...], preferred_element_type=F32)
        u = jnp.dot(x, wu_s[...], preferred_element_type=F32)
        hid = (_silu(a) * u).astype(BF16)
        y = jnp.dot(hid, wd_s[...], preferred_element_type=F32)
        row = lax.broadcasted_iota(jnp.int32, (x.shape[0], 1), 0)
        mine = (row >= lo) & (row < hi)

        @pl.when(first_ref[s] == 1)
        def _():
            o_ref[...] = jnp.where(mine, y, 0.0)

        @pl.when(first_ref[s] == 0)
        def _():
            o_ref[...] = jnp.where(mine, y, o_ref[...])


def _expert_ffn(xs, plan, w_gate, w_up, w_down, layer):
    n_rows, d = xs.shape
    hid = w_gate.shape[3]
    n_steps = plan[0].shape[0]
    xmap = lambda s, tile, exp, lo, hi, first: (tile[s], 0)
    wmap = lambda s, tile, exp, lo, hi, first: (layer, exp[s], 0, 0)
    return pl.pallas_call(
        _expert_kernel,
        out_shape=jax.ShapeDtypeStruct((n_rows, d), F32),
        grid_spec=pltpu.PrefetchScalarGridSpec(
            num_scalar_prefetch=5,
            grid=(n_steps,),
            in_specs=[pl.BlockSpec((MOE_BLOCK, d), xmap),
                      pl.BlockSpec((1, 1, d, hid), wmap),
                      pl.BlockSpec((1, 1, d, hid), wmap),
                      pl.BlockSpec((1, 1, hid, d), wmap)],
            out_specs=pl.BlockSpec((MOE_BLOCK, d), xmap),
            scratch_shapes=[pltpu.VMEM((d, hid), BF16), pltpu.VMEM((d, hid), BF16),
                            pltpu.VMEM((hid, d), BF16), pltpu.SMEM((1,), jnp.int32)]),
        compiler_params=_cparams("arbitrary"),
        name="moe_experts",
    )(*plan, xs, w_gate, w_up, w_down)


def _combine_kernel(pos_ref, posn_ref, ys_hbm, meta_ref, lat_ref, g2_ref, nw_ref, o_ref, buf, sem,
                    *, final, tc):
    i = pl.program_id(0)
    nt = pl.num_programs(0)
    slot = i % 2
    n = TOP_K * tc

    def issue(idx_ref, sl):
        def body(r, c):
            pltpu.make_async_copy(ys_hbm.at[pl.ds(idx_ref[0, 0, r], 1), :],
                                  buf.at[sl, pl.ds(r, 1), :], sem.at[sl]).start()
            return c
        lax.fori_loop(0, n, body, 0, unroll=DMA_UNROLL)

    @pl.when(i == 0)
    def _():
        issue(pos_ref, 0)

    @pl.when(i + 1 < nt)
    def _():
        issue(posn_ref, 1 - slot)

    pltpu.make_async_copy(ys_hbm.at[pl.ds(0, n), :], buf.at[slot], sem.at[slot]).wait()
    gk = meta_ref[:, META_GATE:META_GATE + TOP_K]
    y = gk[:, 0:1] * buf[slot, 0:tc, :] + gk[:, 1:2] * buf[slot, tc:2 * tc, :]
    x = lat_ref[...] + g2_ref[0] * y
    if final:
        x = x * lax.rsqrt(jnp.mean(x * x, axis=-1, keepdims=True) + EPS) * nw_ref[...]
    o_ref[...] = x


def _moe_combine(ys, pos, meta, lat, g2, nw, *, seg_rows, final):
    t, d = lat.shape
    tc = 128
    nt = t // tc
    n = TOP_K * tc
    kern = functools.partial(_combine_kernel, final=final, tc=tc)
    posb = pos.reshape(nt, tc, TOP_K).transpose(0, 2, 1).reshape(nt, 1, n)
    return pl.pallas_call(
        kern,
        out_shape=jax.ShapeDtypeStruct((t, d), F32),
        grid=(nt,),
        in_specs=[pl.BlockSpec((1, 1, n), lambda i: (i, 0, 0), memory_space=pltpu.SMEM),
                  pl.BlockSpec((1, 1, n), lambda i: (jnp.minimum(i + 1, nt - 1), 0, 0),
                               memory_space=pltpu.SMEM),
                  pl.BlockSpec(memory_space=pl.ANY),
                  pl.BlockSpec((tc, LANES), lambda i: (i, 0)),
                  pl.BlockSpec((tc, d), lambda i: (i, 0)),
                  pl.BlockSpec((1, 1, d), lambda i: ((i * tc) // seg_rows, 0, 0)),
                  pl.BlockSpec((1, d), lambda i: (0, 0))],
        out_specs=pl.BlockSpec((tc, d), lambda i: (i, 0)),
        scratch_shapes=[pltpu.VMEM((2, n, d), F32), pltpu.SemaphoreType.DMA((2,))],
        compiler_params=_cparams("arbitrary"),
        name="moe_combine",
    )(posb, posb, ys, meta, lat, g2, nw.reshape(1, d))


CONV_HALO = 16


def _conv_kernel(prev_ref, cur_ref, next_ref, w_ref, b_ref, o_ref, *, seq, rblk):
    i = pl.program_id(0)
    first = (i * rblk) % seq == 0
    last = ((i + 1) * rblk) % seq == 0
    half = CONV_HALO // 2
    pv = jnp.where(first, 0.0, prev_ref[...].astype(F32))[half:, :]
    nx = jnp.where(last, 0.0, next_ref[...].astype(F32))[:half, :]
    xm = jnp.concatenate([pv, cur_ref[...].astype(F32), nx], axis=0)
    n = rblk + 2 * half
    acc = jnp.zeros((rblk, xm.shape[1]), F32) + b_ref[...]
    for k in range(SSD_CONV):
        sh = (SSD_CONV // 2 - k) % n
        xs = xm if sh == 0 else pltpu.roll(xm, sh, axis=0)
        acc = acc + w_ref[k:k + 1, :] * xs[half:half + rblk, :]
    o_ref[...] = _silu(acc).astype(o_ref.dtype)


def _conv_silu(zx, conv_w, conv_b, *, seq, col_off):
    t = zx.shape[0]
    c = conv_w.shape[1]
    rblk, tn = min(512, seq), 1024
    assert seq % rblk == 0 and c % tn == 0 and col_off % tn == 0
    cb = col_off // tn
    hb = rblk // CONV_HALO
    nhb = t // CONV_HALO
    kern = functools.partial(_conv_kernel, seq=seq, rblk=rblk)
    wpad = jnp.zeros((SUBLANES, c), F32).at[:SSD_CONV].set(conv_w.astype(F32))
    return pl.pallas_call(
        kern,
        out_shape=jax.ShapeDtypeStruct((t, c), BF16),
        grid=(t // rblk, c // tn),
        in_specs=[pl.BlockSpec((CONV_HALO, tn), lambda i, j: (jnp.maximum(i * hb - 1, 0), cb + j)),
                  pl.BlockSpec((rblk, tn), lambda i, j: (i, cb + j)),
                  pl.BlockSpec((CONV_HALO, tn), lambda i, j: (jnp.minimum((i + 1) * hb, nhb - 1), cb + j)),
                  pl.BlockSpec((SUBLANES, tn), lambda i, j: (0, j)),
                  pl.BlockSpec((1, tn), lambda i, j: (0, j))],
        out_specs=pl.BlockSpec((rblk, tn), lambda i, j: (i, j)),
        compiler_params=_cparams("parallel", "parallel"),
        name="ssd_conv_silu",
    )(zx, zx, zx, wpad, conv_b.astype(F32).reshape(1, c))


def _softplus(x):
    return jnp.maximum(x, 0.0) + jnp.log1p(jnp.exp(-jnp.abs(x)))


def _ssd_kernel(*refs, reverse, second, nchunk, nblk, groups):
    it = iter(refs)
    x_ref, b_ref, c_ref, dt_ref, prm_ref, s0_ref = (next(it) for _ in range(6))
    if second:
        yprev_ref, dsk_ref = next(it), next(it)
    y_ref, sfin_ref, s_scr = next(it), next(it), next(it)
    acum_s, acumt_s, dtt_s, fwt_s, etot_s, cb_s, bgt_s, yint_s = (next(it) for _ in range(8))
    q = SSD_CHUNK
    nst = SSD_STATE
    gw = x_ref.shape[1] // groups
    pw = 2 * SSD_HEAD_DIM
    s = pl.program_id(1)

    @pl.when(s == 0)
    def _():
        s_scr[...] = s0_ref[0]

    ii = lax.broadcasted_iota(jnp.int32, (q, q), 0)
    jj = lax.broadcasted_iota(jnp.int32, (q, q), 1)
    tri = (jj >= ii) if reverse else (ii >= jj)
    tri_bf = tri.astype(BF16)
    lo_half = lax.broadcasted_iota(jnp.int32, (q, pw), 1) < SSD_HEAD_DIM
    a_neg = prm_ref[0:1, :]
    dtb = prm_ref[1:2, :]

    def chunk(ci, carry):
        c = (nchunk - 1 - ci) if reverse else ci
        rows = pl.ds(pl.multiple_of(c * q, q), q)
        dt = _softplus(dt_ref[rows, :] + dtb)
        da = dt * a_neg
        hi = da.astype(BF16)
        r1 = da - hi.astype(F32)
        mid = r1.astype(BF16)
        lo = (r1 - mid.astype(F32)).astype(BF16)
        acum = (jnp.dot(tri_bf, hi, preferred_element_type=F32)
                + jnp.dot(tri_bf, mid, preferred_element_type=F32)
                + jnp.dot(tri_bf, lo, preferred_element_type=F32))
        total = acum[0:1, :] if reverse else acum[q - 1:q, :]
        acum_s[...] = acum
        acumt_s[...] = acum.T
        dtt_s[...] = dt.T
        fwt_s[...] = (jnp.exp(total - acum) * dt).T
        etot_s[...] = jnp.broadcast_to(jnp.exp(total), etot_s.shape)
        for g in range(groups):
            bg = b_ref[rows, g * nst:(g + 1) * nst]
            cg = c_ref[rows, g * nst:(g + 1) * nst]
            cb_s[...] = lax.dot_general(cg, bg, _NT, preferred_element_type=F32)
            bgt_s[...] = bg.astype(F32).T
            yint_s[...] = jnp.dot(cg, s_scr[g].astype(BF16), preferred_element_type=F32)
            for pr in range(gw // pw):
                h1 = g * (gw // SSD_HEAD_DIM) + 2 * pr
                h2 = h1 + 1
                cols = slice(g * gw + pr * pw, g * gw + (pr + 1) * pw)
                gcols = slice(pr * pw, (pr + 1) * pw)
                ws, bs, eas = [], [], []
                for h in (h1, h2):
                    acol = jnp.broadcast_to(acum_s[:, h:h + 1], (q, q))
                    lm = jnp.where(tri, jnp.exp(acol - acumt_s[h:h + 1, :]), 0.0)
                    ws.append((cb_s[...] * lm * dtt_s[h:h + 1, :]).astype(BF16))
                    bs.append((bgt_s[...] * fwt_s[h:h + 1, :]).astype(BF16))
                    eas.append(jnp.exp(acol))
                xp = x_ref[rows, cols].astype(F32)
                base = yint_s[:, gcols] * jnp.where(lo_half, eas[0], eas[1])
                if second:
                    base = base + yprev_ref[rows, cols] + dsk_ref[:, cols] * xp
                lhs = jnp.concatenate([jnp.concatenate(ws, axis=1), jnp.concatenate(bs, axis=1)], axis=0)
                rhs = jnp.concatenate([jnp.where(lo_half, xp, 0.0), jnp.where(lo_half, 0.0, xp)],
                                      axis=0).astype(BF16)
                res = jnp.dot(lhs, rhs, preferred_element_type=F32)
                y_ref[rows, cols] = (res[0:q, :] + base).astype(y_ref.dtype)
                dec = jnp.where(lo_half[0:1, :], etot_s[0:1, h1:h1 + 1], etot_s[0:1, h2:h2 + 1])
                s_scr[g, :, gcols] = dec * s_scr[g, :, gcols] + res[q:q + nst, :]
        return carry

    lax.fori_loop(0, nchunk, chunk, 0)

    @pl.when(s == nblk - 1)
    def _():
        sfin_ref[0] = s_scr[...]


def _ssd_pass(act, dt, prm, s0, *, batch, seq, reverse, yprev=None, dskip=None):
    groups, nst = SSD_GROUPS, SSD_STATE
    inner = act.shape[1] - 2 * groups * nst
    rblk = min(256, seq)
    nblk = seq // rblk
    second = yprev is not None
    bc = groups * nst
    rmap = (lambda b, s: (b * nblk + nblk - 1 - s, 0)) if reverse else (lambda b, s: (b * nblk + s, 0))
    cmap = lambda k: ((lambda b, s: (b * nblk + nblk - 1 - s, k)) if reverse
                      else (lambda b, s: (b * nblk + s, k)))
    st_spec = pl.BlockSpec((1, groups, nst, inner // groups), lambda b, s: (b, 0, 0, 0))
    args = [act, act, act, dt, prm, s0]
    in_specs = [pl.BlockSpec((rblk, inner), rmap),
                pl.BlockSpec((rblk, bc), cmap(inner // bc)),
                pl.BlockSpec((rblk, bc), cmap(inner // bc + 1)),
                pl.BlockSpec((rblk, dt.shape[1]), rmap),
                pl.BlockSpec(prm.shape, lambda b, s: (0, 0)),
                st_spec]
    if second:
        args += [yprev, dskip]
        in_specs += [pl.BlockSpec((rblk, inner), rmap), pl.BlockSpec((1, inner), lambda b, s: (0, 0))]
    kern = functools.partial(_ssd_kernel, reverse=reverse, second=second, nchunk=rblk // SSD_CHUNK,
                             nblk=nblk, groups=groups)
    return pl.pallas_call(
        kern,
        out_shape=[jax.ShapeDtypeStruct((batch * seq, inner), BF16 if second else F32),
                   jax.ShapeDtypeStruct((batch, groups, nst, inner // groups), F32)],
        grid=(batch, nblk),
        in_specs=in_specs,
        out_specs=[pl.BlockSpec((rblk, inner), rmap), st_spec],
        scratch_shapes=[pltpu.VMEM((groups, nst, inner // groups), F32),
                        pltpu.VMEM((SSD_CHUNK, LANES), F32),
                        pltpu.VMEM((LANES, SSD_CHUNK), F32),
                        pltpu.VMEM((LANES, SSD_CHUNK), F32),
                        pltpu.VMEM((LANES, SSD_CHUNK), F32),
                        pltpu.VMEM((SUBLANES, LANES), F32),
                        pltpu.VMEM((SSD_CHUNK, SSD_CHUNK), F32),
                        pltpu.VMEM((nst, SSD_CHUNK), F32),
                        pltpu.VMEM((SSD_CHUNK, inner // groups), F32)],
        compiler_params=_cparams("parallel", "arbitrary"),
        name="ssd_scan_bwd" if reverse else "ssd_scan_fwd",
    )(*args)


def _rope_tables(seq):
    pos = jnp.arange(seq)
    rows = (pos // GRID_W).astype(F32)
    cols = (pos % GRID_W).astype(F32)
    half = HEAD_DIM // 2
    inv = ROPE_BASE ** (-jnp.arange(0, half, 2, dtype=F32) / half)
    ang = jnp.concatenate([rows[:, None] * inv, cols[:, None] * inv], axis=-1)
    cos, sin = jnp.cos(ang), jnp.sin(ang)
    return jnp.concatenate([cos, cos], axis=-1), jnp.concatenate([-sin, sin], axis=-1)


def _deinterleave_perm(n_cols, n_rot_cols):
    within = np.concatenate([np.arange(0, HEAD_DIM, 2), np.arange(1, HEAD_DIM, 2)])
    perm = np.arange(n_cols)
    for h in range(n_rot_cols // HEAD_DIM):
        perm[h * HEAD_DIM:(h + 1) * HEAD_DIM] = h * HEAD_DIM + within
    return perm


def _moe_layer(hs, logits, router_bias, w_gate, w_up, w_down, layer):
    meta, cnt = _router(logits, router_bias)
    dest, plan = _moe_plan(meta, cnt)
    xs = _moe_dispatch(hs, dest)
    ys = _expert_ffn(xs, plan, w_gate, w_up, w_down, layer)
    return ys, dest, meta


def kernel(x, c, ctx, c_ctx, w_mod, b_mod, norm_mix, norm_ffn, norm_final, w_in_even, w_out_even,
           ret_decay, na_rpb, w_in_odd, conv_w, conv_b, dt_bias, a_log, d_skip, ssd_norm, w_out_odd,
           w_router, router_bias, w_gate, w_up, w_down):
    batch, seq, d = x.shape
    ctx_len = ctx.shape[1]
    depth = w_mod.shape[0]
    assert batch + 1 <= SUBLANES
    tl, tc = batch * seq, batch * ctx_len

    cvec = jnp.zeros((SUBLANES, d), F32).at[0].set(c_ctx).at[1:1 + batch].set(c)
    mod = _modulation(cvec, w_mod, b_mod)

    def mod_vecs(i, k):
        m = mod[i, :, k * d:(k + 1) * d]
        return m[1:1 + batch].reshape(batch, 1, d), m[0:1].reshape(1, 1, d)

    wr = jnp.zeros((d, LANES), F32).at[:, :N_EXPERTS].set(w_router.astype(F32))
    wrh, wrl = _split_bf16(wr)

    lat = x.reshape(tl, d)
    cx = ctx.reshape(tc, d)
    out = None
    for i in range(depth):
        last = i == depth - 1
        j = i // 2
        (sh1, csh1), (sc1, csc1), (g1, cg1) = mod_vecs(i, 0), mod_vecs(i, 1), mod_vecs(i, 2)
        (sh2, csh2), (sc2, csc2), (g2, cg2) = mod_vecs(i, 3), mod_vecs(i, 4), mod_vecs(i, 5)
        if i % 2 == 0:
            ret_heads = w_out_even.shape[1] // (2 * HEAD_DIM)
            rot_cols = 2 * ret_heads * HEAD_DIM
            w_in = w_in_even[j][:, _deinterleave_perm(w_in_even.shape[2], rot_cols)].astype(BF16)
            cos, sin = _rope_tables(seq)
            ones, zeros = jnp.ones((tc, HEAD_DIM), F32), jnp.zeros((tc, HEAD_DIM), F32)
            proj_l, = _inproj(lat, norm_mix[i], sh1, sc1, w_in, seg_rows=seq, tm=1024, tn=1024,
                              even=True, cos=cos, sin=sin)
            proj_c, = _inproj(cx, norm_mix[i], csh1, csc1, w_in, seg_rows=tc, tm=tc, tn=1024,
                              even=True, cos=ones, sin=zeros)
            tabs = _retention_tables(ret_decay[j])
            zero = jnp.zeros((batch, ret_heads, HEAD_DIM, HEAD_DIM), F32)
            ret_c, s_f, s_b = _retention(proj_c, tabs, zero, zero, batch=batch, seq=ctx_len, rblk=ctx_len)
            ret_l, _, _ = _retention(proj_l, tabs, s_f, s_b, batch=batch, seq=seq, rblk=1024)
            bias = _na_bias_tables(na_rpb[j], seq // GRID_W)
            na_l = _neighbourhood_attention(proj_l, proj_c, bias, batch=batch, seq=seq, ctx_len=ctx_len)
            w_out = w_out_even[j].astype(BF16)
            mix_l, ssd_l, mix_c, ssd_c = (ret_l, na_l), None, None, None
            if not last:
                mix_c = (ret_c, _ctx_attention(proj_c, batch=batch, ctx_len=ctx_len, heads=ret_heads))
            tm_out = 512
        else:
            inner = w_out_odd.shape[1]
            n_main = inner + conv_w.shape[2]
            w_in = w_in_odd[j]
            w_main = w_in[:, :n_main].astype(BF16)
            heads = w_in.shape[1] - n_main
            w_dt = jnp.zeros((d, LANES), F32).at[:, :heads].set(w_in[:, n_main:]).astype(BF16)
            zx_l, dt_l = _inproj(lat, norm_mix[i], sh1, sc1, w_main, seg_rows=seq, tm=1024, tn=1024, wdt=w_dt)
            zx_c, dt_c = _inproj(cx, norm_mix[i], csh1, csc1, w_main, seg_rows=tc, tm=tc, tn=1024, wdt=w_dt)
            act_l = _conv_silu(zx_l, conv_w[j], conv_b[j], seq=seq, col_off=inner)
            act_c = _conv_silu(zx_c, conv_w[j], conv_b[j], seq=ctx_len, col_off=inner)
            a_neg = -jnp.exp(a_log[j].astype(F32))

            def prm(dr):
                return (jnp.zeros((SUBLANES, LANES), F32).at[0, :heads].set(a_neg[dr])
                        .at[1, :heads].set(dt_bias[j, dr].astype(F32)))

            dsk = jnp.repeat((d_skip[j, 0] + d_skip[j, 1]).astype(F32), inner // heads)[None, :]
            zero = jnp.zeros((batch, SSD_GROUPS, SSD_STATE, inner // SSD_GROUPS), F32)
            yf_c, s_f = _ssd_pass(act_c, dt_c, prm(0), zero, batch=batch, seq=ctx_len, reverse=False)
            y_c, s_b = _ssd_pass(act_c, dt_c, prm(1), zero, batch=batch, seq=ctx_len, reverse=True,
                                 yprev=yf_c, dskip=dsk)
            yf_l, _ = _ssd_pass(act_l, dt_l, prm(0), s_f, batch=batch, seq=seq, reverse=False)
            y_l, _ = _ssd_pass(act_l, dt_l, prm(1), s_b, batch=batch, seq=seq, reverse=True,
                               yprev=yf_l, dskip=dsk)
            w_out = w_out_odd[j].astype(BF16)
            mix_l, ssd_l = None, (y_l, zx_l, ssd_norm[j].astype(F32))
            mix_c, ssd_c = None, (None if last else (y_c, zx_c, ssd_norm[j].astype(F32)))
            tm_out = 256

        nwf = norm_ffn[i].astype(F32)
        lat2, h_l, lg_l = _outproj(mix_l, w_out, lat, g1, nwf, sh2, sc2, wrh, wrl, seg_rows=seq,
                                   tm=tm_out, ssd=ssd_l)
        if last:
            ys, pos, meta = _moe_layer([h_l], lg_l, router_bias, w_gate, w_up, w_down, i)
            out = _moe_combine(ys, pos, meta, lat2, g2, norm_final.astype(F32), seg_rows=seq, final=True)
        else:
            cx2, h_c, lg_c = _outproj(mix_c, w_out, cx, cg1, nwf, csh2, csc2, wrh, wrl, seg_rows=tc,
                                      tm=tm_out, ssd=ssd_c)
            ys, pos, meta = _moe_layer([h_c, h_l], jnp.concatenate([lg_c, lg_l], axis=0), router_bias,
                                       w_gate, w_up, w_down, i)
            ones = jnp.ones((d,), F32)
            cx = _moe_combine(ys, pos[:tc], meta[:tc], cx2, cg2, ones, seg_rows=tc, final=False)
            lat = _moe_combine(ys, pos[tc:], meta[tc:], lat2, g2, ones, seg_rows=seq, final=False)
    return out.reshape(batch, seq, d)
```

```python
import functools
import math

import numpy as np
import jax
import jax.numpy as jnp
from jax import lax
from jax.experimental import pallas as pl
from jax.experimental.pallas import tpu as pltpu

GRID_W = 64
EPS = 1e-6
HEAD_DIM = 128
ROPE_BASE = 10000.0
RET_CHUNK = 128
NA_ROWS = 8
NA_COLS = 16
SSD_HEAD_DIM = 64
SSD_GROUPS = 8
SSD_STATE = 128
SSD_CONV = 5
SSD_CHUNK = 128
N_EXPERTS = 32
N_GROUPS = 8
EXPERTS_PER_GROUP = N_EXPERTS // N_GROUPS
TOP_K = 2
MOE_BLOCK = 512

LANES = 128
SUBLANES = 8
VMEM_LIMIT_BYTES = 56 * 1024 * 1024

NEG_INF = -1e30
F32 = jnp.float32
BF16 = jnp.bfloat16


def _cparams(*sem):
    return pltpu.CompilerParams(dimension_semantics=sem, vmem_limit_bytes=VMEM_LIMIT_BYTES)


def _silu(x):
    return x * (1.0 / (1.0 + jnp.exp(-x)))


def _mod_kernel(c_ref, w_ref, b_ref, o_ref):
    c = _silu(c_ref[...]).astype(BF16)
    o_ref[0] = jnp.dot(c, w_ref[0].astype(BF16), preferred_element_type=F32) + b_ref[0]


def _modulation(cvec, w_mod, b_mod):
    depth, d, n = w_mod.shape
    tn = 1024
    return pl.pallas_call(
        _mod_kernel,
        out_shape=jax.ShapeDtypeStruct((depth, SUBLANES, n), F32),
        grid=(depth, n // tn),
        in_specs=[
            pl.BlockSpec((SUBLANES, d), lambda i, j: (0, 0)),
            pl.BlockSpec((1, d, tn), lambda i, j: (i, 0, j)),
            pl.BlockSpec((1, 1, tn), lambda i, j: (i, 0, j)),
        ],
        out_specs=pl.BlockSpec((1, SUBLANES, tn), lambda i, j: (i, 0, j)),
        compiler_params=_cparams("parallel", "parallel"),
        name="adaln_modulation",
    )(cvec, w_mod, b_mod.reshape(depth, 1, n))


def _rms_modulate(x, nw, sh, sc):
    r = lax.rsqrt(jnp.mean(x * x, axis=-1, keepdims=True) + EPS)
    return (x * r * nw) * (1.0 + sc) + sh


def _rope(t, cos, sin):
    return t * cos + pltpu.roll(t, HEAD_DIM // 2, axis=1) * sin


def _inproj_kernel(*refs, even, has_dt, tn):
    it = iter(refs)
    x_ref, nw_ref, sh_ref, sc_ref, w_ref = (next(it) for _ in range(5))
    if even:
        cos_ref, sin_ref = next(it), next(it)
    if has_dt:
        wdt_ref = next(it)
    o_ref = next(it)
    if has_dt:
        dt_ref = next(it)
    h_scr = next(it)

    j = pl.program_id(1)

    @pl.when(j == 0)
    def _():
        h = _rms_modulate(x_ref[...], nw_ref[...], sh_ref[0], sc_ref[0]).astype(BF16)
        h_scr[...] = h
        if has_dt:
            dt_ref[...] = jnp.dot(h, wdt_ref[...], preferred_element_type=F32)

    acc = jnp.dot(h_scr[...], w_ref[...], preferred_element_type=F32)

    if not even:
        o_ref[...] = acc.astype(o_ref.dtype)
        return

    heads_per_tile = tn // HEAD_DIM
    tiles_per_kilo = 1024 // tn
    seg = j // tiles_per_kilo
    scale = HEAD_DIM ** -0.5

    @pl.when(seg <= 1)
    def _():
        mul = jnp.where(seg == 1, scale, 1.0).astype(F32)
        cos, sin = cos_ref[...], sin_ref[...]
        for hh in range(heads_per_tile):
            sl = slice(hh * HEAD_DIM, (hh + 1) * HEAD_DIM)
            o_ref[:, sl] = (_rope(acc[:, sl], cos, sin) * mul).astype(o_ref.dtype)

    @pl.when(seg == 4)
    def _():
        o_ref[...] = (acc * scale).astype(o_ref.dtype)

    @pl.when((seg == 2) | (seg == 3) | (seg >= 5))
    def _():
        o_ref[...] = acc.astype(o_ref.dtype)


def _seg_spec(d, seg_rows, tm):
    return pl.BlockSpec((1, 1, d), lambda i, j: ((i * tm) // seg_rows, 0, 0))


def _inproj(x, nw, sh, sc, w, *, seg_rows, tm, tn, even=False, cos=None, sin=None, wdt=None):
    t, d = x.shape
    n = w.shape[1]
    tm = min(tm, t)
    assert t % tm == 0 and n % tn == 0 and seg_rows % tm == 0
    args, in_specs = [x], [pl.BlockSpec((tm, d), lambda i, j: (i, 0))]
    args += [nw.reshape(1, d), sh, sc, w]
    in_specs += [pl.BlockSpec((1, d), lambda i, j: (0, 0)), _seg_spec(d, seg_rows, tm),
                 _seg_spec(d, seg_rows, tm), pl.BlockSpec((d, tn), lambda i, j: (0, j))]
    if even:
        lblk = cos.shape[0] // tm
        args += [cos, sin]
        in_specs += [pl.BlockSpec((tm, HEAD_DIM), lambda i, j: (i % lblk, 0))] * 2
    if wdt is not None:
        args.append(wdt)
        in_specs.append(pl.BlockSpec(wdt.shape, lambda i, j: (0, 0)))
    out_shape = [jax.ShapeDtypeStruct((t, n), BF16)]
    out_specs = [pl.BlockSpec((tm, tn), lambda i, j: (i, j))]
    if wdt is not None:
        out_shape.append(jax.ShapeDtypeStruct((t, wdt.shape[1]), F32))
        out_specs.append(pl.BlockSpec((tm, wdt.shape[1]), lambda i, j: (i, 0)))
    kern = functools.partial(_inproj_kernel, even=even, has_dt=wdt is not None, tn=tn)
    return pl.pallas_call(
        kern,
        out_shape=out_shape,
        grid=(t // tm, n // tn),
        in_specs=in_specs,
        out_specs=out_specs,
        scratch_shapes=[pltpu.VMEM((tm, d), BF16)],
        compiler_params=_cparams("parallel", "arbitrary"),
        name="inproj_even" if even else "inproj_odd",
    )(*args)


RET_UNROLL = 8
_TN = (((0,), (0,)), ((), ()))
_NT = (((1,), (1,)), ((), ()))


def _retention_kernel(q_ref, k_ref, v_ref, g_ref, tab_ref, sf0_ref, sb0_ref,
                      o_ref, sf_ref, sb_ref, sb_store, s_run, *, nchunk, nblk):
    p = pl.program_id(2)
    s = pl.program_id(3)
    q = RET_CHUNK
    dcomb = tab_ref[0, 0]

    @pl.when(p == 0)
    def _():
        @pl.when(s == 0)
        def _():
            s_run[...] = sb0_ref[0, 0]

        blk = nblk - 1 - s
        kb, cb = tab_ref[0, 4], tab_ref[0, 6]

        def body(ci, carry):
            c = nchunk - 1 - ci
            rows = pl.ds(pl.multiple_of(c * q, q), q)
            st = s_run[...]
            sb_store[blk * nchunk + c] = st.astype(BF16)
            kd = (k_ref[rows, :].astype(F32) * kb).astype(BF16)
            upd = lax.dot_general(kd, v_ref[rows, :], _TN, preferred_element_type=F32)
            s_run[...] = cb * st + upd
            return carry

        lax.fori_loop(0, nchunk, body, 0, unroll=min(nchunk, RET_UNROLL))

        @pl.when(s == nblk - 1)
        def _():
            sb_ref[0, 0] = s_run[...]

    @pl.when(p == 1)
    def _():
        @pl.when(s == 0)
        def _():
            s_run[...] = sf0_ref[0, 0]

        qf, qb, kf, cf = tab_ref[0, 1], tab_ref[0, 2], tab_ref[0, 3], tab_ref[0, 5]

        def body(c, carry):
            rows = pl.ds(pl.multiple_of(c * q, q), q)
            qc, kc, vc = q_ref[rows, :], k_ref[rows, :], v_ref[rows, :]
            qc32, kc32 = qc.astype(F32), kc.astype(F32)
            st = s_run[...]
            sc = lax.dot_general(qc, kc, _NT, preferred_element_type=F32) * dcomb
            o = jnp.dot(sc.astype(BF16), vc, preferred_element_type=F32)
            qcat = jnp.concatenate([(qc32 * qf).astype(BF16), (qc32 * qb).astype(BF16)], axis=1)
            scat = jnp.concatenate([st.astype(BF16), sb_store[s * nchunk + c]], axis=0)
            o = o + jnp.dot(qcat, scat, preferred_element_type=F32)
            upd = lax.dot_general((kc32 * kf).astype(BF16), vc, _TN, preferred_element_type=F32)
            s_run[...] = cf * st + upd
            o = o * lax.rsqrt(jnp.mean(o * o, axis=-1, keepdims=True) + EPS)
            o_ref[rows, :] = (o * _silu(g_ref[rows, :].astype(F32))).astype(o_ref.dtype)
            return carry

        lax.fori_loop(0, nchunk, body, 0, unroll=min(nchunk, RET_UNROLL))

        @pl.when(s == nblk - 1)
        def _():
            sf_ref[0, 0] = s_run[...]


def _retention_tables(ret_logit):
    q = RET_CHUNK
    lg = jax.nn.log_sigmoid(ret_logit.astype(F32))
    lf, lb = lg[0][:, None, None], lg[1][:, None, None]
    i = jnp.arange(q, dtype=F32)[None, :, None]
    j = jnp.arange(q, dtype=F32)[None, None, :]
    ones = jnp.ones((1, 1, q), F32)
    dcomb = (jnp.where(i >= j, jnp.exp(lf * jnp.maximum(i - j, 0.0)), 0.0)
             + jnp.where(j >= i, jnp.exp(lb * jnp.maximum(j - i, 0.0)), 0.0))
    qf = jnp.exp(lf * (i + 1.0)) * ones
    qb = jnp.exp(lb * (q - i)) * ones
    kf = jnp.exp(lf * (q - 1.0 - i)) * ones
    kb = jnp.exp(lb * i) * ones
    cf = jnp.exp(lf * q) * jnp.ones((1, q, q), F32)
    cb = jnp.exp(lb * q) * jnp.ones((1, q, q), F32)
    return jnp.stack([dcomb, qf, qb, kf, kb, cf, cb], axis=1)


def _retention(proj, tabs, sf0, sb0, *, batch, seq, rblk):
    heads = tabs.shape[0]
    rblk = min(rblk, seq)
    nblk = seq // rblk
    nchunk = rblk // RET_CHUNK
    hd = HEAD_DIM

    def qmap(off):
        return lambda b, h, p, s: (b * nblk + p * s, off + h)

    def kvmap(off):
        return lambda b, h, p, s: (b * nblk + s + (1 - p) * (nblk - 1 - 2 * s), off + h)

    st_spec = pl.BlockSpec((1, 1, hd, hd), lambda b, h, p, s: (b, h, 0, 0))
    kern = functools.partial(_retention_kernel, nchunk=nchunk, nblk=nblk)
    return pl.pallas_call(
        kern,
        out_shape=[jax.ShapeDtypeStruct((batch * seq, heads * hd), BF16),
                   jax.ShapeDtypeStruct((batch, heads, hd, hd), F32),
                   jax.ShapeDtypeStruct((batch, heads, hd, hd), F32)],
        grid=(batch, heads, 2, nblk),
        in_specs=[pl.BlockSpec((rblk, hd), qmap(0)),
                  pl.BlockSpec((rblk, hd), kvmap(heads)),
                  pl.BlockSpec((rblk, hd), kvmap(2 * heads)),
                  pl.BlockSpec((rblk, hd), qmap(3 * heads)),
                  pl.BlockSpec((1, 7, RET_CHUNK, RET_CHUNK), lambda b, h, p, s: (h, 0, 0, 0)),
                  st_spec, st_spec],
        out_specs=[pl.BlockSpec((rblk, hd), qmap(0)), st_spec, st_spec],
        scratch_shapes=[pltpu.VMEM((seq // RET_CHUNK, hd, hd), BF16), pltpu.VMEM((hd, hd), F32)],
        compiler_params=_cparams("parallel", "parallel", "arbitrary", "arbitrary"),
        name="retention",
    )(proj, proj, proj, proj, tabs, sf0, sb0)


NA_QROWS = 8
NA_WROWS = 16
NA_KBLK = 4
NA_QSPLIT = 2


def _na_kernel(q_ref, k0, k1, k2, k3, v0, v1, v2, v3, kc_ref, vc_ref, bias_ref, o_ref):
    kb = NA_KBLK * GRID_W
    nq = q_ref.shape[0] // NA_QSPLIT
    for part in range(NA_QSPLIT):
        rows = slice(part * nq, (part + 1) * nq)
        q = q_ref[rows, :]
        scores = []
        for i, kr in enumerate((k0, k1, k2, k3)):
            sc = lax.dot_general(q, kr[...], _NT, preferred_element_type=F32)
            scores.append(sc + bias_ref[0, 0, rows, i * kb:(i + 1) * kb])
        scores.append(lax.dot_general(q, kc_ref[...], _NT, preferred_element_type=F32))
        m = scores[0].max(axis=-1, keepdims=True)
        for sc in scores[1:]:
            m = jnp.maximum(m, sc.max(axis=-1, keepdims=True))
        acc = jnp.zeros((nq, HEAD_DIM), F32)
        l = jnp.zeros((nq, 1), F32)
        for sc, vr in zip(scores, (v0, v1, v2, v3, vc_ref)):
            pr = jnp.exp(sc - m)
            l = l + pr.sum(axis=-1, keepdims=True)
            acc = acc + jnp.dot(pr.astype(BF16), vr[...], preferred_element_type=F32)
        o_ref[rows, :] = (acc / l).astype(o_ref.dtype)


def _na_bias_tables(rpb, rows):
    nrb = rows // NA_QROWS
    nr, nc = 2 * NA_ROWS - 1, 2 * NA_COLS - 1
    hi = lax.Precision.HIGHEST
    qc = np.arange(GRID_W)[:, None]
    kc = np.arange(GRID_W)[None, :]
    c0 = np.clip(qc - NA_COLS // 2, 0, GRID_W - NA_COLS)
    col_valid = (kc >= c0) & (kc < c0 + NA_COLS)
    sel_c = (np.arange(nc)[:, None, None] == (kc - qc + NA_COLS - 1)[None]).astype(np.float32)
    toep = jnp.einsum('haj,jqk->haqk', rpb.astype(F32), sel_c, precision=hi)
    qr = np.arange(NA_QROWS)[:, None]
    kr = np.arange(NA_WROWS)[None, :]
    tabs = []
    for rb in (0, 1, nrb - 1):
        w0 = int(np.clip(rb * NA_QROWS - NA_ROWS // 2, 0, rows - NA_WROWS))
        r = rb * NA_QROWS + qr
        krow = w0 + kr
        r0 = np.clip(r - NA_ROWS // 2, 0, rows - NA_ROWS)
        row_valid = (krow >= r0) & (krow < r0 + NA_ROWS)
        sel_r = ((np.arange(nr)[None, None, :] == (krow - r + NA_ROWS - 1)[:, :, None])
                 & row_valid[:, :, None]).astype(np.float32)
        b = jnp.einsum('rka,haqc->hrqkc', sel_r, toep, precision=hi)
        valid = row_valid[:, None, :, None] & col_valid[None, :, None, :]
        b = jnp.where(valid[None], b, NEG_INF)
        tabs.append(b.reshape(rpb.shape[0], NA_QROWS * GRID_W, NA_WROWS * GRID_W))
    return jnp.stack(tabs, axis=1)


def _neighbourhood_attention(proj, proj_ctx, bias, *, batch, seq, ctx_len):
    heads = bias.shape[0]
    hd = HEAD_DIM
    qblk = NA_QROWS * GRID_W
    kblk = NA_KBLK * GRID_W
    nrb = seq // qblk
    nkb = seq // kblk
    assert seq // GRID_W >= NA_WROWS + NA_QROWS
    qoff, koff, voff = 4 * heads, 5 * heads, 6 * heads

    def kmap(off, i):
        def f(b, h, r):
            u0 = jnp.clip(2 * r - 1, 0, nkb - NA_WROWS // NA_KBLK)
            return (b * nkb + u0 + i, off + h)
        return f

    def bias_map(b, h, r):
        return (h, (r > 0).astype(jnp.int32) + (r == nrb - 1).astype(jnp.int32), 0, 0)

    kspecs = [pl.BlockSpec((kblk, hd), kmap(koff, i)) for i in range(4)]
    vspecs = [pl.BlockSpec((kblk, hd), kmap(voff, i)) for i in range(4)]
    return pl.pallas_call(
        _na_kernel,
        out_shape=jax.ShapeDtypeStruct((batch * seq, heads * hd), BF16),
        grid=(batch, heads, nrb),
        in_specs=[pl.BlockSpec((qblk, hd), lambda b, h, r: (b * nrb + r, qoff + h))] + kspecs + vspecs + [
            pl.BlockSpec((ctx_len, hd), lambda b, h, r: (b, koff + h)),
            pl.BlockSpec((ctx_len, hd), lambda b, h, r: (b, voff + h)),
            pl.BlockSpec((1, 1, qblk, NA_WROWS * GRID_W), bias_map)],
        out_specs=pl.BlockSpec((qblk, hd), lambda b, h, r: (b * nrb + r, h)),
        compiler_params=_cparams("parallel", "parallel", "arbitrary"),
        name="neighbourhood_attention",
    )(*([proj] * 9), proj_ctx, proj_ctx, bias)


def _ctx_attn_kernel(q_ref, k_ref, v_ref, o_ref):
    sc = lax.dot_general(q_ref[...], k_ref[...], _NT, preferred_element_type=F32)
    pr = jnp.exp(sc - sc.max(axis=-1, keepdims=True))
    acc = jnp.dot(pr.astype(BF16), v_ref[...], preferred_element_type=F32)
    o_ref[...] = (acc / pr.sum(axis=-1, keepdims=True)).astype(o_ref.dtype)


def _ctx_attention(proj_ctx, *, batch, ctx_len, heads):
    hd = HEAD_DIM
    return pl.pallas_call(
        _ctx_attn_kernel,
        out_shape=jax.ShapeDtypeStruct((batch * ctx_len, heads * hd), BF16),
        grid=(batch, heads),
        in_specs=[pl.BlockSpec((ctx_len, hd), lambda b, h: (b, 4 * heads + h)),
                  pl.BlockSpec((ctx_len, hd), lambda b, h: (b, 5 * heads + h)),
                  pl.BlockSpec((ctx_len, hd), lambda b, h: (b, 6 * heads + h))],
        out_specs=pl.BlockSpec((ctx_len, hd), lambda b, h: (b, h)),
        compiler_params=_cparams("parallel", "parallel"),
        name="context_attention",
    )(proj_ctx, proj_ctx, proj_ctx)


OUTPROJ_SPLIT = 2


def _split_bf16(x):
    hi = x.astype(BF16)
    return hi, (x - hi.astype(F32)).astype(BF16)


def _outproj_kernel(*refs, odd):
    it = iter(refs)
    if odd:
        y_ref, z_ref, nws_ref = next(it), next(it), next(it)
    else:
        mr_ref, mn_ref = next(it), next(it)
    w_ref, lat_ref, g1_ref, nwf_ref, sh_ref, sc_ref = (next(it) for _ in range(6))
    wrh_ref, wrl_ref = next(it), next(it)
    lat2_ref, h_ref, lg_ref = next(it), next(it), next(it)

    nsplit = 1 if odd else OUTPROJ_SPLIT
    nr = lat_ref.shape[0] // nsplit
    for part in range(nsplit):
        rows = slice(part * nr, (part + 1) * nr)
        if odd:
            u = y_ref[rows, :].astype(F32) * _silu(z_ref[rows, :].astype(F32))
            u = u * lax.rsqrt(jnp.mean(u * u, axis=-1, keepdims=True) + EPS) * nws_ref[...]
            o = jnp.dot(u.astype(BF16), w_ref[...], preferred_element_type=F32)
        else:
            half = mr_ref.shape[1]
            o = (jnp.dot(mr_ref[rows, :], w_ref[0:half, :], preferred_element_type=F32)
                 + jnp.dot(mn_ref[rows, :], w_ref[half:2 * half, :], preferred_element_type=F32))
        lat2 = lat_ref[rows, :] + g1_ref[0] * o
        lat2_ref[rows, :] = lat2
        h = _rms_modulate(lat2, nwf_ref[...], sh_ref[0], sc_ref[0])
        h_ref[rows, :] = h
        hh, hl = _split_bf16(h)
        wrh = wrh_ref[...]
        lg_ref[rows, :] = (jnp.dot(hh, wrh, preferred_element_type=F32)
                           + jnp.dot(hl, wrh, preferred_element_type=F32)
                           + jnp.dot(hh, wrl_ref[...], preferred_element_type=F32))


def _outproj(mix, w, lat, g1, nwf, sh, sc, wrh, wrl, *, seg_rows, tm, ssd=None):
    t, d = lat.shape
    tm = min(tm, t)
    assert t % tm == 0 and seg_rows % tm == 0
    odd = ssd is not None
    row = lambda i: (i, 0)
    seg = pl.BlockSpec((1, 1, d), lambda i: ((i * tm) // seg_rows, 0, 0))
    const = lambda shape: pl.BlockSpec(shape, lambda i: (0,) * len(shape))
    if odd:
        y, zx, nws = ssd
        kin = y.shape[1]
        args = [y, zx, nws.reshape(1, kin)]
        in_specs = [pl.BlockSpec((tm, kin), row), pl.BlockSpec((tm, kin), row), const((1, kin))]
    else:
        mr, mn = mix
        args = [mr, mn]
        in_specs = [pl.BlockSpec((tm, mr.shape[1]), row), pl.BlockSpec((tm, mn.shape[1]), row)]
    args += [w, lat, g1, nwf.reshape(1, d), sh, sc, wrh, wrl]
    in_specs += [pl.BlockSpec(w.shape, lambda i: (0, 0), pipeline_mode=pl.Buffered(1)),
                 pl.BlockSpec((tm, d), row), seg, const((1, d)), seg, seg,
                 const(wrh.shape), const(wrl.shape)]
    nl = wrh.shape[1]
    return pl.pallas_call(
        functools.partial(_outproj_kernel, odd=odd),
        out_shape=[jax.ShapeDtypeStruct((t, d), F32),
                   jax.ShapeDtypeStruct((t, d), F32),
                   jax.ShapeDtypeStruct((t, nl), F32)],
        grid=(t // tm,),
        in_specs=in_specs,
        out_specs=[pl.BlockSpec((tm, d), row), pl.BlockSpec((tm, d), row), pl.BlockSpec((tm, nl), row)],
        compiler_params=_cparams("parallel"),
        name="outproj_odd" if odd else "outproj_even",
    )(*args)


ROUTER_TILE = 256
META_E, META_RANK, META_GATE = 0, 2, 4


def _router_kernel(lg_ref, rb_ref, meta_ref, cnt_ref, run_scr):
    tm = lg_ref.shape[0]
    i = pl.program_id(0)

    @pl.when(i == 0)
    def _():
        run_scr[...] = jnp.zeros_like(run_scr)

    big = 1e9
    lane = lax.broadcasted_iota(jnp.int32, (tm, LANES), 1)
    lanef = lane.astype(F32)
    scores = 1.0 / (1.0 + jnp.exp(-lg_ref[...]))
    biased = jnp.where(lane < N_EXPERTS, scores + rb_ref[...], NEG_INF)
    member = lane % EXPERTS_PER_GROUP
    best = jnp.full((tm, LANES), NEG_INF, F32)
    for k in range(1, EXPERTS_PER_GROUP):
        best = jnp.maximum(best, jnp.where(member >= k, biased + pltpu.roll(biased, k, axis=1), NEG_INF))
    grp = (lane // EXPERTS_PER_GROUP).astype(F32)
    top = best.max(axis=-1, keepdims=True)
    g_sel = jnp.where(best == top, grp, big).min(axis=-1, keepdims=True)
    v = jnp.where(grp == g_sel, biased, NEG_INF)
    i1 = jnp.where(v == v.max(axis=-1, keepdims=True), lanef, big).min(axis=-1, keepdims=True)
    v = jnp.where(lanef == i1, NEG_INF, v)
    i2 = jnp.where(v == v.max(axis=-1, keepdims=True), lanef, big).min(axis=-1, keepdims=True)
    oh1, oh2 = lanef == i1, lanef == i2
    s1 = jnp.where(oh1, scores, 0.0).sum(axis=-1, keepdims=True)
    s2 = jnp.where(oh2, scores, 0.0).sum(axis=-1, keepdims=True)
    den = s1 + s2
    oh = oh1.astype(F32) + oh2.astype(F32)
    earlier = (lax.broadcasted_iota(jnp.int32, (tm, tm), 0)
               > lax.broadcasted_iota(jnp.int32, (tm, tm), 1)).astype(BF16)
    base = jnp.dot(earlier, oh.astype(BF16), preferred_element_type=F32) + run_scr[0:1, :]
    r1 = jnp.where(oh1, base, 0.0).sum(axis=-1, keepdims=True)
    r2 = jnp.where(oh2, base, 0.0).sum(axis=-1, keepdims=True)
    total = run_scr[0:1, :] + oh.sum(axis=0, keepdims=True)
    run_scr[0:1, :] = total
    cnt_ref[...] = jnp.broadcast_to(total, cnt_ref.shape)
    meta = jnp.zeros((tm, LANES), F32)
    for ln, val in ((META_E, i1), (META_E + 1, i2), (META_RANK, r1), (META_RANK + 1, r2),
                    (META_GATE, s1 / den), (META_GATE + 1, s2 / den)):
        meta = jnp.where(lane == ln, val, meta)
    meta_ref[...] = meta


def _router(logits, router_bias):
    t = logits.shape[0]
    tm = ROUTER_TILE
    rb = jnp.zeros((1, LANES), F32).at[0, :N_EXPERTS].set(router_bias.astype(F32))
    return pl.pallas_call(
        _router_kernel,
        out_shape=[jax.ShapeDtypeStruct((t, LANES), F32), jax.ShapeDtypeStruct((SUBLANES, LANES), F32)],
        grid=(t // tm,),
        in_specs=[pl.BlockSpec((tm, LANES), lambda i: (i, 0)), pl.BlockSpec((1, LANES), lambda i: (0, 0))],
        out_specs=[pl.BlockSpec((tm, LANES), lambda i: (i, 0)),
                   pl.BlockSpec((SUBLANES, LANES), lambda i: (0, 0))],
        scratch_shapes=[pltpu.VMEM((SUBLANES, LANES), F32)],
        compiler_params=_cparams("arbitrary"),
        name="moe_router",
    )(logits, rb)


def _moe_plan(meta, cnt):
    t = meta.shape[0]
    blk = MOE_BLOCK
    counts = cnt[0, :N_EXPERTS].astype(jnp.int32)
    ends = jnp.cumsum(counts)
    offsets = ends - counts
    e = meta[:, META_E:META_E + TOP_K].astype(jnp.int32)
    rank = meta[:, META_RANK:META_RANK + TOP_K].astype(jnp.int32)
    onehot = e[..., None] == jnp.arange(N_EXPERTS, dtype=jnp.int32)
    dest = jnp.sum(jnp.where(onehot, offsets, 0), axis=-1) + rank
    n_tiles = t * TOP_K // blk
    tile_start = jnp.arange(n_tiles, dtype=jnp.int32) * blk
    def count_le(sorted_vals, query):
        return jnp.sum(sorted_vals[None, :] <= query[:, None], axis=-1).astype(jnp.int32)

    first_e = count_le(ends, tile_start)
    last_e = count_le(ends, tile_start + blk - 1)
    nsteps = last_e - first_e + 1
    step_end = jnp.cumsum(nsteps)
    step_start = step_end - nsteps
    s = jnp.arange(n_tiles + N_EXPERTS - 1, dtype=jnp.int32)
    valid = s < step_end[-1]
    tile_of = jnp.minimum(count_le(step_end, s), n_tiles - 1)
    exp_of = jnp.where(valid, first_e[tile_of] + s - step_start[tile_of], last_e[-1])
    exp_of = jnp.clip(exp_of, 0, N_EXPERTS - 1).astype(jnp.int32)
    lo = jnp.clip(offsets[exp_of] - tile_of * blk, 0, blk)
    hi = jnp.clip(ends[exp_of] - tile_of * blk, 0, blk)
    lo = jnp.where(valid, lo, 0).astype(jnp.int32)
    hi = jnp.where(valid, hi, 0).astype(jnp.int32)
    first = (valid & (s == step_start[tile_of])).astype(jnp.int32)
    return dest.astype(jnp.int32), (tile_of, exp_of, lo, hi, first)


DMA_UNROLL = 8


def _dispatch_kernel(*refs, tm, src_tiles):
    dest_ref, srcs, xs_hbm, sem = refs[0], refs[1:-2], refs[-2], refs[-1]
    i = pl.program_id(0)

    def scatter_from(src):
        def body(r, c):
            for k in range(TOP_K):
                pltpu.make_async_copy(src.at[pl.ds(r, 1), :],
                                      xs_hbm.at[pl.ds(dest_ref[0, 0, TOP_K * r + k], 1), :], sem).start()
            return c
        lax.fori_loop(0, tm, body, 0, unroll=DMA_UNROLL)
        for _ in range(TOP_K):
            pltpu.make_async_copy(src, xs_hbm.at[pl.ds(0, tm), :], sem).wait()

    first_tile = 0
    for src, ntile in zip(srcs, src_tiles):
        @pl.when((i >= first_tile) & (i < first_tile + ntile))
        def _(src=src):
            scatter_from(src)
        first_tile += ntile


def _moe_dispatch(hs, dest):
    d = hs[0].shape[1]
    t = dest.shape[0]
    tm = 256
    nt = t // tm
    src_tiles = tuple(h.shape[0] // tm for h in hs)
    assert sum(src_tiles) == nt and all(h.shape[0] % tm == 0 for h in hs)
    starts = np.cumsum((0,) + src_tiles[:-1])
    src_specs = [pl.BlockSpec((tm, d), lambda i, s=int(s), n=n: (jnp.clip(i - s, 0, n - 1), 0))
                 for s, n in zip(starts, src_tiles)]
    kern = functools.partial(_dispatch_kernel, tm=tm, src_tiles=src_tiles)
    return pl.pallas_call(
        kern,
        out_shape=jax.ShapeDtypeStruct((TOP_K * t, d), F32),
        grid=(nt,),
        in_specs=[pl.BlockSpec((1, 1, TOP_K * tm), lambda i: (i, 0, 0), memory_space=pltpu.SMEM)]
        + src_specs,
        out_specs=pl.BlockSpec(memory_space=pl.ANY),
        scratch_shapes=[pltpu.SemaphoreType.DMA(())],
        compiler_params=_cparams("arbitrary"),
        name="moe_dispatch",
    )(dest.reshape(nt, 1, TOP_K * tm), *hs)


EXPERT_SPLIT = 2


def _expert_kernel(tile_ref, exp_ref, lo_ref, hi_ref, first_ref, x_ref, wg_ref, wu_ref, wd_ref,
                   o_ref, wg_s, wu_s, wd_s, cur_e):
    s = pl.program_id(0)
    lo, hi, e = lo_ref[s], hi_ref[s], exp_ref[s]

    @pl.when(s == 0)
    def _():
        cur_e[0] = -1

    @pl.when(hi > lo)
    def _():
        @pl.when(cur_e[0] != e)
        def _():
            wg_s[...] = wg_ref[0, 0].astype(BF16)
            wu_s[...] = wu_ref[0, 0].astype(BF16)
            wd_s[...] = wd_ref[0, 0].astype(BF16)
            cur_e[0] = e

        nr = x_ref.shape[0] // EXPERT_SPLIT
        ys = []
        for part in range(EXPERT_SPLIT):
            x = x_ref[part * nr:(part + 1) * nr, :].astype(BF16)
            a = jnp.dot(x, wg_s[...], preferred_element_type=F32)
            u = jnp.dot(x, wu_s[...], preferred_element_type=F32)
            hid = (_silu(a) * u).astype(BF16)
            ys.append(jnp.dot(hid, wd_s[...], preferred_element_type=F32))
        y = jnp.concatenate(ys, axis=0)
        row = lax.broadcasted_iota(jnp.int32, (y.shape[0], 1), 0)
        mine = (row >= lo) & (row < hi)

        @pl.when(first_ref[s] == 1)
        def _():
            o_ref[...] = jnp.where(mine, y, 0.0)

        @pl.when(first_ref[s] == 0)
        def _():
            o_ref[...] = jnp.where(mine, y, o_ref[...])


def _expert_ffn(xs, plan, w_gate, w_up, w_down, layer):
    n_rows, d = xs.shape
    hid = w_gate.shape[3]
    n_steps = plan[0].shape[0]
    xmap = lambda s, tile, exp, lo, hi, first: (tile[s], 0)
    wmap = lambda s, tile, exp, lo, hi, first: (layer, exp[s], 0, 0)
    return pl.pallas_call(
        _expert_kernel,
        out_shape=jax.ShapeDtypeStruct((n_rows, d), F32),
        grid_spec=pltpu.PrefetchScalarGridSpec(
            num_scalar_prefetch=5,
            grid=(n_steps,),
            in_specs=[pl.BlockSpec((MOE_BLOCK, d), xmap),
                      pl.BlockSpec((1, 1, d, hid), wmap),
                      pl.BlockSpec((1, 1, d, hid), wmap),
                      pl.BlockSpec((1, 1, hid, d), wmap)],
            out_specs=pl.BlockSpec((MOE_BLOCK, d), xmap),
            scratch_shapes=[pltpu.VMEM((d, hid), BF16), pltpu.VMEM((d, hid), BF16),
                            pltpu.VMEM((hid, d), BF16), pltpu.SMEM((1,), jnp.int32)]),
        compiler_params=_cparams("arbitrary"),
        name="moe_experts",
    )(*plan, xs, w_gate, w_up, w_down)


def _combine_kernel(pos_ref, posn_ref, ys_hbm, meta_ref, lat_ref, g2_ref, nw_ref, o_ref, buf, sem,
                    *, final, tc):
    i = pl.program_id(0)
    nt = pl.num_programs(0)
    slot = i % 2
    n = TOP_K * tc

    def issue(idx_ref, sl):
        def body(r, c):
            pltpu.make_async_copy(ys_hbm.at[pl.ds(idx_ref[0, 0, r], 1), :],
                                  buf.at[sl, pl.ds(r, 1), :], sem.at[sl]).start()
            return c
        lax.fori_loop(0, n, body, 0, unroll=DMA_UNROLL)

    @pl.when(i == 0)
    def _():
        issue(pos_ref, 0)

    @pl.when(i + 1 < nt)
    def _():
        issue(posn_ref, 1 - slot)

    pltpu.make_async_copy(ys_hbm.at[pl.ds(0, n), :], buf.at[slot], sem.at[slot]).wait()
    gk = meta_ref[:, META_GATE:META_GATE + TOP_K]
    y = gk[:, 0:1] * buf[slot, 0:tc, :] + gk[:, 1:2] * buf[slot, tc:2 * tc, :]
    x = lat_ref[...] + g2_ref[0] * y
    if final:
        x = x * lax.rsqrt(jnp.mean(x * x, axis=-1, keepdims=True) + EPS) * nw_ref[...]
    o_ref[...] = x


def _moe_combine(ys, pos, meta, lat, g2, nw, *, seg_rows, final):
    t, d = lat.shape
    tc = 256
    nt = t // tc
    n = TOP_K * tc
    kern = functools.partial(_combine_kernel, final=final, tc=tc)
    posb = pos.reshape(nt, tc, TOP_K).transpose(0, 2, 1).reshape(nt, 1, n)
    return pl.pallas_call(
        kern,
        out_shape=jax.ShapeDtypeStruct((t, d), F32),
        grid=(nt,),
        in_specs=[pl.BlockSpec((1, 1, n), lambda i: (i, 0, 0), memory_space=pltpu.SMEM),
                  pl.BlockSpec((1, 1, n), lambda i: (jnp.minimum(i + 1, nt - 1), 0, 0),
                               memory_space=pltpu.SMEM),
                  pl.BlockSpec(memory_space=pl.ANY),
                  pl.BlockSpec((tc, LANES), lambda i: (i, 0)),
                  pl.BlockSpec((tc, d), lambda i: (i, 0)),
                  pl.BlockSpec((1, 1, d), lambda i: ((i * tc) // seg_rows, 0, 0)),
                  pl.BlockSpec((1, d), lambda i: (0, 0))],
        out_specs=pl.BlockSpec((tc, d), lambda i: (i, 0)),
        scratch_shapes=[pltpu.VMEM((2, n, d), F32), pltpu.SemaphoreType.DMA((2,))],
        compiler_params=_cparams("arbitrary"),
        name="moe_combine",
    )(posb, posb, ys, meta, lat, g2, nw.reshape(1, d))


CONV_HALO = 16


def _conv_kernel(prev_ref, cur_ref, next_ref, w_ref, b_ref, o_ref, *, seq, rblk):
    i = pl.program_id(0)
    first = (i * rblk) % seq == 0
    last = ((i + 1) * rblk) % seq == 0
    half = CONV_HALO // 2
    pv = jnp.where(first, 0.0, prev_ref[...].astype(F32))[half:, :]
    nx = jnp.where(last, 0.0, next_ref[...].astype(F32))[:half, :]
    xm = jnp.concatenate([pv, cur_ref[...].astype(F32), nx], axis=0)
    n = rblk + 2 * half
    acc = jnp.zeros((rblk, xm.shape[1]), F32) + b_ref[...]
    for k in range(SSD_CONV):
        sh = (SSD_CONV // 2 - k) % n
        xs = xm if sh == 0 else pltpu.roll(xm, sh, axis=0)
        acc = acc + w_ref[k:k + 1, :] * xs[half:half + rblk, :]
    o_ref[...] = _silu(acc).astype(o_ref.dtype)


def _conv_silu(zx, conv_w, conv_b, *, seq, col_off):
    t = zx.shape[0]
    c = conv_w.shape[1]
    rblk, tn = min(512, seq), 1024
    assert seq % rblk == 0 and c % tn == 0 and col_off % tn == 0
    cb = col_off // tn
    hb = rblk // CONV_HALO
    nhb = t // CONV_HALO
    kern = functools.partial(_conv_kernel, seq=seq, rblk=rblk)
    wpad = jnp.zeros((SUBLANES, c), F32).at[:SSD_CONV].set(conv_w.astype(F32))
    return pl.pallas_call(
        kern,
        out_shape=jax.ShapeDtypeStruct((t, c), BF16),
        grid=(t // rblk, c // tn),
        in_specs=[pl.BlockSpec((CONV_HALO, tn), lambda i, j: (jnp.maximum(i * hb - 1, 0), cb + j)),
                  pl.BlockSpec((rblk, tn), lambda i, j: (i, cb + j)),
                  pl.BlockSpec((CONV_HALO, tn), lambda i, j: (jnp.minimum((i + 1) * hb, nhb - 1), cb + j)),
                  pl.BlockSpec((SUBLANES, tn), lambda i, j: (0, j)),
                  pl.BlockSpec((1, tn), lambda i, j: (0, j))],
        out_specs=pl.BlockSpec((rblk, tn), lambda i, j: (i, j)),
        compiler_params=_cparams("parallel", "parallel"),
        name="ssd_conv_silu",
    )(zx, zx, zx, wpad, conv_b.astype(F32).reshape(1, c))


LOG2E = 1.4426950408889634


def _softplus(x):
    return jnp.maximum(x, 0.0) + jnp.log1p(jnp.exp(-jnp.abs(x)))


def _ssd_kernel(*refs, reverse, second, nchunk, nblk, groups):
    it = iter(refs)
    x_ref, b_ref, c_ref, dt_ref, prm_ref, s0_ref = (next(it) for _ in range(6))
    if second:
        yprev_ref, dsk_ref = next(it), next(it)
    y_ref, sfin_ref, s_scr = next(it), next(it), next(it)
    acum_s, acumt_s, dtt_s, fwt_s, etot_s, cb_s, bgt_s, yint_s = (next(it) for _ in range(8))
    q = SSD_CHUNK
    nst = SSD_STATE
    gw = x_ref.shape[1] // groups
    pw = 2 * SSD_HEAD_DIM
    s = pl.program_id(1)

    @pl.when(s == 0)
    def _():
        s_scr[...] = s0_ref[0]

    ii = lax.broadcasted_iota(jnp.int32, (q, q), 0)
    jj = lax.broadcasted_iota(jnp.int32, (q, q), 1)
    tri = (jj >= ii) if reverse else (ii >= jj)
    tri_bf = tri.astype(BF16)
    lo_half = lax.broadcasted_iota(jnp.int32, (q, pw), 1) < SSD_HEAD_DIM
    a_neg = prm_ref[0:1, :]
    dtb = prm_ref[1:2, :]

    def chunk(ci, carry):
        c = (nchunk - 1 - ci) if reverse else ci
        rows = pl.ds(pl.multiple_of(c * q, q), q)
        dt = _softplus(dt_ref[rows, :] + dtb)
        da = dt * a_neg
        hi = da.astype(BF16)
        r1 = da - hi.astype(F32)
        mid = r1.astype(BF16)
        lo = (r1 - mid.astype(F32)).astype(BF16)
        acum = (jnp.dot(tri_bf, hi, preferred_element_type=F32)
                + jnp.dot(tri_bf, mid, preferred_element_type=F32)
                + jnp.dot(tri_bf, lo, preferred_element_type=F32))
        acum = acum * LOG2E
        total = acum[0:1, :] if reverse else acum[q - 1:q, :]
        acum_s[...] = acum
        acumt_s[...] = acum.T
        dtt_s[...] = dt.T
        fwt_s[...] = (jnp.exp2(total - acum) * dt).T
        etot_s[...] = jnp.broadcast_to(jnp.exp2(total), etot_s.shape)
        for g in range(groups):
            bg = b_ref[rows, g * nst:(g + 1) * nst]
            cg = c_ref[rows, g * nst:(g + 1) * nst]
            cb_s[...] = lax.dot_general(cg, bg, _NT, preferred_element_type=F32)
            bgt_s[...] = bg.astype(F32).T
            yint_s[...] = jnp.dot(cg, s_scr[g].astype(BF16), preferred_element_type=F32)
            for pr in range(gw // pw):
                h1 = g * (gw // SSD_HEAD_DIM) + 2 * pr
                h2 = h1 + 1
                cols = slice(g * gw + pr * pw, g * gw + (pr + 1) * pw)
                gcols = slice(pr * pw, (pr + 1) * pw)
                ws, bs, acols = [], [], []
                for h in (h1, h2):
                    acol = jnp.broadcast_to(acum_s[:, h:h + 1], (q, q))
                    lm = jnp.where(tri, jnp.exp2(acol - acumt_s[h:h + 1, :]), 0.0)
                    ws.append((cb_s[...] * lm * dtt_s[h:h + 1, :]).astype(BF16))
                    bs.append((bgt_s[...] * fwt_s[h:h + 1, :]).astype(BF16))
                    acols.append(acol)
                xp = x_ref[rows, cols].astype(F32)
                base = yint_s[:, gcols] * jnp.exp2(jnp.where(lo_half, acols[0], acols[1]))
                if second:
                    base = base + yprev_ref[rows, cols] + dsk_ref[:, cols] * xp
                lhs = jnp.concatenate([jnp.concatenate(ws, axis=1), jnp.concatenate(bs, axis=1)], axis=0)
                rhs = jnp.concatenate([jnp.where(lo_half, xp, 0.0), jnp.where(lo_half, 0.0, xp)],
                                      axis=0).astype(BF16)
                res = jnp.dot(lhs, rhs, preferred_element_type=F32)
                y_ref[rows, cols] = (res[0:q, :] + base).astype(y_ref.dtype)
                dec = jnp.where(lo_half[0:1, :], etot_s[0:1, h1:h1 + 1], etot_s[0:1, h2:h2 + 1])
                s_scr[g, :, gcols] = dec * s_scr[g, :, gcols] + res[q:q + nst, :]
        return carry

    lax.fori_loop(0, nchunk, chunk, 0)

    @pl.when(s == nblk - 1)
    def _():
        sfin_ref[0] = s_scr[...]


def _ssd_pass(act, dt, prm, s0, *, batch, seq, reverse, yprev=None, dskip=None):
    groups, nst = SSD_GROUPS, SSD_STATE
    inner = act.shape[1] - 2 * groups * nst
    rblk = min(256, seq)
    nblk = seq // rblk
    second = yprev is not None
    bc = groups * nst
    rmap = (lambda b, s: (b * nblk + nblk - 1 - s, 0)) if reverse else (lambda b, s: (b * nblk + s, 0))
    cmap = lambda k: ((lambda b, s: (b * nblk + nblk - 1 - s, k)) if reverse
                      else (lambda b, s: (b * nblk + s, k)))
    st_spec = pl.BlockSpec((1, groups, nst, inner // groups), lambda b, s: (b, 0, 0, 0))
    args = [act, act, act, dt, prm, s0]
    in_specs = [pl.BlockSpec((rblk, inner), rmap),
                pl.BlockSpec((rblk, bc), cmap(inner // bc)),
                pl.BlockSpec((rblk, bc), cmap(inner // bc + 1)),
                pl.BlockSpec((rblk, dt.shape[1]), rmap),
                pl.BlockSpec(prm.shape, lambda b, s: (0, 0)),
                st_spec]
    if second:
        args += [yprev, dskip]
        in_specs += [pl.BlockSpec((rblk, inner), rmap), pl.BlockSpec((1, inner), lambda b, s: (0, 0))]
    kern = functools.partial(_ssd_kernel, reverse=reverse, second=second, nchunk=rblk // SSD_CHUNK,
                             nblk=nblk, groups=groups)
    return pl.pallas_call(
        kern,
        out_shape=[jax.ShapeDtypeStruct((batch * seq, inner), BF16 if second else F32),
                   jax.ShapeDtypeStruct((batch, groups, nst, inner // groups), F32)],
        grid=(batch, nblk),
        in_specs=in_specs,
        out_specs=[pl.BlockSpec((rblk, inner), rmap), st_spec],
        scratch_shapes=[pltpu.VMEM((groups, nst, inner // groups), F32),
                        pltpu.VMEM((SSD_CHUNK, LANES), F32),
                        pltpu.VMEM((LANES, SSD_CHUNK), F32),
                        pltpu.VMEM((LANES, SSD_CHUNK), F32),
                        pltpu.VMEM((LANES, SSD_CHUNK), F32),
                        pltpu.VMEM((SUBLANES, LANES), F32),
                        pltpu.VMEM((SSD_CHUNK, SSD_CHUNK), F32),
                        pltpu.VMEM((nst, SSD_CHUNK), F32),
                        pltpu.VMEM((SSD_CHUNK, inner // groups), F32)],
        compiler_params=_cparams("parallel", "arbitrary"),
        name="ssd_scan_bwd" if reverse else "ssd_scan_fwd",
    )(*args)


def _rope_tables(seq):
    pos = jnp.arange(seq)
    rows = (pos // GRID_W).astype(F32)
    cols = (pos % GRID_W).astype(F32)
    half = HEAD_DIM // 2
    inv = ROPE_BASE ** (-jnp.arange(0, half, 2, dtype=F32) / half)
    ang = jnp.concatenate([rows[:, None] * inv, cols[:, None] * inv], axis=-1)
    cos, sin = jnp.cos(ang), jnp.sin(ang)
    return jnp.concatenate([cos, cos], axis=-1), jnp.concatenate([-sin, sin], axis=-1)


def _deinterleave_perm(n_cols, n_rot_cols):
    within = np.concatenate([np.arange(0, HEAD_DIM, 2), np.arange(1, HEAD_DIM, 2)])
    perm = np.arange(n_cols)
    for h in range(n_rot_cols // HEAD_DIM):
        perm[h * HEAD_DIM:(h + 1) * HEAD_DIM] = h * HEAD_DIM + within
    return perm


def _moe_layer(hs, logits, router_bias, w_gate, w_up, w_down, layer):
    meta, cnt = _router(logits, router_bias)
    dest, plan = _moe_plan(meta, cnt)
    xs = _moe_dispatch(hs, dest)
    ys = _expert_ffn(xs, plan, w_gate, w_up, w_down, layer)
    return ys, dest, meta


def kernel(x, c, ctx, c_ctx, w_mod, b_mod, norm_mix, norm_ffn, norm_final, w_in_even, w_out_even,
           ret_decay, na_rpb, w_in_odd, conv_w, conv_b, dt_bias, a_log, d_skip, ssd_norm, w_out_odd,
           w_router, router_bias, w_gate, w_up, w_down):
    batch, seq, d = x.shape
    ctx_len = ctx.shape[1]
    depth = w_mod.shape[0]
    assert batch + 1 <= SUBLANES
    tl, tc = batch * seq, batch * ctx_len

    cvec = jnp.zeros((SUBLANES, d), F32).at[0].set(c_ctx).at[1:1 + batch].set(c)
    mod = _modulation(cvec, w_mod, b_mod)

    def mod_vecs(i, k):
        m = mod[i, :, k * d:(k + 1) * d]
        return m[1:1 + batch].reshape(batch, 1, d), m[0:1].reshape(1, 1, d)

    wr = jnp.zeros((d, LANES), F32).at[:, :N_EXPERTS].set(w_router.astype(F32))
    wrh, wrl = _split_bf16(wr)

    lat = x.reshape(tl, d)
    cx = ctx.reshape(tc, d)
    out = None
    for i in range(depth):
        last = i == depth - 1
        j = i // 2
        (sh1, csh1), (sc1, csc1), (g1, cg1) = mod_vecs(i, 0), mod_vecs(i, 1), mod_vecs(i, 2)
        (sh2, csh2), (sc2, csc2), (g2, cg2) = mod_vecs(i, 3), mod_vecs(i, 4), mod_vecs(i, 5)
        if i % 2 == 0:
            ret_heads = w_out_even.shape[1] // (2 * HEAD_DIM)
            rot_cols = 2 * ret_heads * HEAD_DIM
            w_in = w_in_even[j][:, _deinterleave_perm(w_in_even.shape[2], rot_cols)].astype(BF16)
            cos, sin = _rope_tables(seq)
            ones, zeros = jnp.ones((tc, HEAD_DIM), F32), jnp.zeros((tc, HEAD_DIM), F32)
            proj_l, = _inproj(lat, norm_mix[i], sh1, sc1, w_in, seg_rows=seq, tm=1024, tn=1024,
                              even=True, cos=cos, sin=sin)
            proj_c, = _inproj(cx, norm_mix[i], csh1, csc1, w_in, seg_rows=tc, tm=tc, tn=1024,
                              even=True, cos=ones, sin=zeros)
            tabs = _retention_tables(ret_decay[j])
            zero = jnp.zeros((batch, ret_heads, HEAD_DIM, HEAD_DIM), F32)
            ret_c, s_f, s_b = _retention(proj_c, tabs, zero, zero, batch=batch, seq=ctx_len, rblk=ctx_len)
            ret_l, _, _ = _retention(proj_l, tabs, s_f, s_b, batch=batch, seq=seq, rblk=1024)
            bias = _na_bias_tables(na_rpb[j], seq // GRID_W)
            na_l = _neighbourhood_attention(proj_l, proj_c, bias, batch=batch, seq=seq, ctx_len=ctx_len)
            w_out = w_out_even[j].astype(BF16)
            mix_l, ssd_l, mix_c, ssd_c = (ret_l, na_l), None, None, None
            if not last:
                mix_c = (ret_c, _ctx_attention(proj_c, batch=batch, ctx_len=ctx_len, heads=ret_heads))
            tm_out = 512
        else:
            inner = w_out_odd.shape[1]
            n_main = inner + conv_w.shape[2]
            w_in = w_in_odd[j]
            w_main = w_in[:, :n_main].astype(BF16)
            heads = w_in.shape[1] - n_main
            w_dt = jnp.zeros((d, LANES), F32).at[:, :heads].set(w_in[:, n_main:]).astype(BF16)
            zx_l, dt_l = _inproj(lat, norm_mix[i], sh1, sc1, w_main, seg_rows=seq, tm=1024, tn=1024, wdt=w_dt)
            zx_c, dt_c = _inproj(cx, norm_mix[i], csh1, csc1, w_main, seg_rows=tc, tm=tc, tn=1024, wdt=w_dt)
            act_l = _conv_silu(zx_l, conv_w[j], conv_b[j], seq=seq, col_off=inner)
            act_c = _conv_silu(zx_c, conv_w[j], conv_b[j], seq=ctx_len, col_off=inner)
            a_neg = -jnp.exp(a_log[j].astype(F32))

            def prm(dr):
                return (jnp.zeros((SUBLANES, LANES), F32).at[0, :heads].set(a_neg[dr])
                        .at[1, :heads].set(dt_bias[j, dr].astype(F32)))

            dsk = jnp.repeat((d_skip[j, 0] + d_skip[j, 1]).astype(F32), inner // heads)[None, :]
            zero = jnp.zeros((batch, SSD_GROUPS, SSD_STATE, inner // SSD_GROUPS), F32)
            yf_c, s_f = _ssd_pass(act_c, dt_c, prm(0), zero, batch=batch, seq=ctx_len, reverse=False)
            y_c, s_b = _ssd_pass(act_c, dt_c, prm(1), zero, batch=batch, seq=ctx_len, reverse=True,
                                 yprev=yf_c, dskip=dsk)
            yf_l, _ = _ssd_pass(act_l, dt_l, prm(0), s_f, batch=batch, seq=seq, reverse=False)
            y_l, _ = _ssd_pass(act_l, dt_l, prm(1), s_b, batch=batch, seq=seq, reverse=True,
                               yprev=yf_l, dskip=dsk)
            w_out = w_out_odd[j].astype(BF16)
            mix_l, ssd_l = None, (y_l, zx_l, ssd_norm[j].astype(F32))
            mix_c, ssd_c = None, (None if last else (y_c, zx_c, ssd_norm[j].astype(F32)))
            tm_out = 256

        nwf = norm_ffn[i].astype(F32)
        lat2, h_l, lg_l = _outproj(mix_l, w_out, lat, g1, nwf, sh2, sc2, wrh, wrl, seg_rows=seq,
                                   tm=tm_out, ssd=ssd_l)
        if last:
            ys, pos, meta = _moe_layer([h_l], lg_l, router_bias, w_gate, w_up, w_down, i)
            out = _moe_combine(ys, pos, meta, lat2, g2, norm_final.astype(F32), seg_rows=seq, final=True)
        else:
            cx2, h_c, lg_c = _outproj(mix_c, w_out, cx, cg1, nwf, csh2, csc2, wrh, wrl, seg_rows=tc,
                                      tm=tm_out, ssd=ssd_c)
            ys, pos, meta = _moe_layer([h_c, h_l], jnp.concatenate([lg_c, lg_l], axis=0), router_bias,
                                       w_gate, w_up, w_down, i)
            ones = jnp.ones((d,), F32)
            cx = _moe_combine(ys, pos[:tc], meta[:tc], cx2, cg2, ones, seg_rows=tc, final=False)
            lat = _moe_combine(ys, pos[tc:], meta[tc:], lat2, g2, ones, seg_rows=seq, final=False)
    return out.reshape(batch, seq, d)
```

```python
import functools
import math

import numpy as np
import jax
import jax.numpy as jnp
from jax import lax
from jax.experimental import pallas as pl
from jax.experimental.pallas import tpu as pltpu

GRID_W = 64
EPS = 1e-6
HEAD_DIM = 128
ROPE_BASE = 10000.0
RET_CHUNK = 128
NA_ROWS = 8
NA_COLS = 16
SSD_HEAD_DIM = 64
SSD_GROUPS = 8
SSD_STATE = 128
SSD_CONV = 5
SSD_CHUNK = 128
N_EXPERTS = 32
N_GROUPS = 8
EXPERTS_PER_GROUP = N_EXPERTS // N_GROUPS
TOP_K = 2
MOE_BLOCK = 512

LANES = 128
SUBLANES = 8
VMEM_LIMIT_BYTES = 56 * 1024 * 1024

NEG_INF = -1e30
F32 = jnp.float32
BF16 = jnp.bfloat16


def _cparams(*sem):
    return pltpu.CompilerParams(dimension_semantics=sem, vmem_limit_bytes=VMEM_LIMIT_BYTES)


def _silu(x):
    return x * (1.0 / (1.0 + jnp.exp(-x)))


def _mod_kernel(c_ref, w_ref, b_ref, o_ref):
    c = _silu(c_ref[...]).astype(BF16)
    o_ref[0] = jnp.dot(c, w_ref[0].astype(BF16), preferred_element_type=F32) + b_ref[0]


def _modulation(cvec, w_mod, b_mod):
    depth, d, n = w_mod.shape
    tn = 1024
    return pl.pallas_call(
        _mod_kernel,
        out_shape=jax.ShapeDtypeStruct((depth, SUBLANES, n), F32),
        grid=(depth, n // tn),
        in_specs=[
            pl.BlockSpec((SUBLANES, d), lambda i, j: (0, 0)),
            pl.BlockSpec((1, d, tn), lambda i, j: (i, 0, j)),
            pl.BlockSpec((1, 1, tn), lambda i, j: (i, 0, j)),
        ],
        out_specs=pl.BlockSpec((1, SUBLANES, tn), lambda i, j: (i, 0, j)),
        compiler_params=_cparams("parallel", "parallel"),
        name="adaln_modulation",
    )(cvec, w_mod, b_mod.reshape(depth, 1, n))


def _rms_modulate(x, nw, sh, sc):
    r = lax.rsqrt(jnp.mean(x * x, axis=-1, keepdims=True) + EPS)
    return (x * r * nw) * (1.0 + sc) + sh


def _rope(t, cos, sin):
    return t * cos + pltpu.roll(t, HEAD_DIM // 2, axis=1) * sin


def _inproj_kernel(*refs, even, has_dt, tn):
    it = iter(refs)
    x_ref, nw_ref, sh_ref, sc_ref, w_ref = (next(it) for _ in range(5))
    if even:
        cos_ref, sin_ref = next(it), next(it)
    if has_dt:
        wdt_ref = next(it)
    o_ref = next(it)
    if has_dt:
        dt_ref = next(it)
    h_scr = next(it)

    j = pl.program_id(1)

    @pl.when(j == 0)
    def _():
        h = _rms_modulate(x_ref[...], nw_ref[...], sh_ref[0], sc_ref[0]).astype(BF16)
        h_scr[...] = h
        if has_dt:
            dt_ref[...] = jnp.dot(h, wdt_ref[...], preferred_element_type=F32)

    acc = jnp.dot(h_scr[...], w_ref[...], preferred_element_type=F32)

    if not even:
        o_ref[...] = acc.astype(o_ref.dtype)
        return

    heads_per_tile = tn // HEAD_DIM
    tiles_per_kilo = 1024 // tn
    seg = j // tiles_per_kilo
    scale = HEAD_DIM ** -0.5

    @pl.when(seg <= 1)
    def _():
        mul = jnp.where(seg == 1, scale, 1.0).astype(F32)
        cos, sin = cos_ref[...], sin_ref[...]
        for hh in range(heads_per_tile):
            sl = slice(hh * HEAD_DIM, (hh + 1) * HEAD_DIM)
            o_ref[:, sl] = (_rope(acc[:, sl], cos, sin) * mul).astype(o_ref.dtype)

    @pl.when(seg == 4)
    def _():
        o_ref[...] = (acc * scale).astype(o_ref.dtype)

    @pl.when((seg == 2) | (seg == 3) | (seg >= 5))
    def _():
        o_ref[...] = acc.astype(o_ref.dtype)


def _seg_spec(d, seg_rows, tm):
    return pl.BlockSpec((1, 1, d), lambda i, j: ((i * tm) // seg_rows, 0, 0))


def _inproj(x, nw, sh, sc, w, *, seg_rows, tm, tn, even=False, cos=None, sin=None, wdt=None):
    t, d = x.shape
    n = w.shape[1]
    tm = min(tm, t)
    assert t % tm == 0 and n % tn == 0 and seg_rows % tm == 0
    args, in_specs = [x], [pl.BlockSpec((tm, d), lambda i, j: (i, 0))]
    args += [nw.reshape(1, d), sh, sc, w]
    in_specs += [pl.BlockSpec((1, d), lambda i, j: (0, 0)), _seg_spec(d, seg_rows, tm),
                 _seg_spec(d, seg_rows, tm), pl.BlockSpec((d, tn), lambda i, j: (0, j))]
    if even:
        lblk = cos.shape[0] // tm
        args += [cos, sin]
        in_specs += [pl.BlockSpec((tm, HEAD_DIM), lambda i, j: (i % lblk, 0))] * 2
    if wdt is not None:
        args.append(wdt)
        in_specs.append(pl.BlockSpec(wdt.shape, lambda i, j: (0, 0)))
    out_shape = [jax.ShapeDtypeStruct((t, n), BF16)]
    out_specs = [pl.BlockSpec((tm, tn), lambda i, j: (i, j))]
    if wdt is not None:
        out_shape.append(jax.ShapeDtypeStruct((t, wdt.shape[1]), F32))
        out_specs.append(pl.BlockSpec((tm, wdt.shape[1]), lambda i, j: (i, 0)))
    kern = functools.partial(_inproj_kernel, even=even, has_dt=wdt is not None, tn=tn)
    return pl.pallas_call(
        kern,
        out_shape=out_shape,
        grid=(t // tm, n // tn),
        in_specs=in_specs,
        out_specs=out_specs,
        scratch_shapes=[pltpu.VMEM((tm, d), BF16)],
        compiler_params=_cparams("parallel", "arbitrary"),
        name="inproj_even" if even else "inproj_odd",
    )(*args)


RET_UNROLL = 8
_TN = (((0,), (0,)), ((), ()))
_NT = (((1,), (1,)), ((), ()))


def _retention_kernel(q_ref, k_ref, v_ref, g_ref, tab_ref, sf0_ref, sb0_ref,
                      o_ref, sf_ref, sb_ref, sb_store, s_run, *, nchunk, nblk):
    p = pl.program_id(2)
    s = pl.program_id(3)
    q = RET_CHUNK
    dcomb = tab_ref[0, 0]

    @pl.when(p == 0)
    def _():
        @pl.when(s == 0)
        def _():
            s_run[...] = sb0_ref[0, 0]

        blk = nblk - 1 - s
        kb, cb = tab_ref[0, 4], tab_ref[0, 6]

        def body(ci, carry):
            c = nchunk - 1 - ci
            rows = pl.ds(pl.multiple_of(c * q, q), q)
            st = s_run[...]
            sb_store[blk * nchunk + c] = st.astype(BF16)
            kd = (k_ref[rows, :].astype(F32) * kb).astype(BF16)
            upd = lax.dot_general(kd, v_ref[rows, :], _TN, preferred_element_type=F32)
            s_run[...] = cb * st + upd
            return carry

        lax.fori_loop(0, nchunk, body, 0, unroll=min(nchunk, RET_UNROLL))

        @pl.when(s == nblk - 1)
        def _():
            sb_ref[0, 0] = s_run[...]

    @pl.when(p == 1)
    def _():
        @pl.when(s == 0)
        def _():
            s_run[...] = sf0_ref[0, 0]

        qf, qb, kf, cf = tab_ref[0, 1], tab_ref[0, 2], tab_ref[0, 3], tab_ref[0, 5]

        def body(c, carry):
            rows = pl.ds(pl.multiple_of(c * q, q), q)
            qc, kc, vc = q_ref[rows, :], k_ref[rows, :], v_ref[rows, :]
            qc32, kc32 = qc.astype(F32), kc.astype(F32)
            st = s_run[...]
            sc = lax.dot_general(qc, kc, _NT, preferred_element_type=F32) * dcomb
            o = jnp.dot(sc.astype(BF16), vc, preferred_element_type=F32)
            qcat = jnp.concatenate([(qc32 * qf).astype(BF16), (qc32 * qb).astype(BF16)], axis=1)
            scat = jnp.concatenate([st.astype(BF16), sb_store[s * nchunk + c]], axis=0)
            o = o + jnp.dot(qcat, scat, preferred_element_type=F32)
            upd = lax.dot_general((kc32 * kf).astype(BF16), vc, _TN, preferred_element_type=F32)
            s_run[...] = cf * st + upd
            o = o * lax.rsqrt(jnp.mean(o * o, axis=-1, keepdims=True) + EPS)
            o_ref[rows, :] = (o * _silu(g_ref[rows, :].astype(F32))).astype(o_ref.dtype)
            return carry

        lax.fori_loop(0, nchunk, body, 0, unroll=min(nchunk, RET_UNROLL))

        @pl.when(s == nblk - 1)
        def _():
            sf_ref[0, 0] = s_run[...]


def _retention_tables(ret_logit):
    q = RET_CHUNK
    lg = jax.nn.log_sigmoid(ret_logit.astype(F32))
    lf, lb = lg[0][:, None, None], lg[1][:, None, None]
    i = jnp.arange(q, dtype=F32)[None, :, None]
    j = jnp.arange(q, dtype=F32)[None, None, :]
    ones = jnp.ones((1, 1, q), F32)
    dcomb = (jnp.where(i >= j, jnp.exp(lf * jnp.maximum(i - j, 0.0)), 0.0)
             + jnp.where(j >= i, jnp.exp(lb * jnp.maximum(j - i, 0.0)), 0.0))
    qf = jnp.exp(lf * (i + 1.0)) * ones
    qb = jnp.exp(lb * (q - i)) * ones
    kf = jnp.exp(lf * (q - 1.0 - i)) * ones
    kb = jnp.exp(lb * i) * ones
    cf = jnp.exp(lf * q) * jnp.ones((1, q, q), F32)
    cb = jnp.exp(lb * q) * jnp.ones((1, q, q), F32)
    return jnp.stack([dcomb, qf, qb, kf, kb, cf, cb], axis=1)


def _retention(proj, tabs, sf0, sb0, *, batch, seq, rblk):
    heads = tabs.shape[0]
    rblk = min(rblk, seq)
    nblk = seq // rblk
    nchunk = rblk // RET_CHUNK
    hd = HEAD_DIM

    def qmap(off):
        return lambda b, h, p, s: (b * nblk + p * s, off + h)

    def kvmap(off):
        return lambda b, h, p, s: (b * nblk + s + (1 - p) * (nblk - 1 - 2 * s), off + h)

    st_spec = pl.BlockSpec((1, 1, hd, hd), lambda b, h, p, s: (b, h, 0, 0))
    kern = functools.partial(_retention_kernel, nchunk=nchunk, nblk=nblk)
    return pl.pallas_call(
        kern,
        out_shape=[jax.ShapeDtypeStruct((batch * seq, heads * hd), BF16),
                   jax.ShapeDtypeStruct((batch, heads, hd, hd), F32),
                   jax.ShapeDtypeStruct((batch, heads, hd, hd), F32)],
        grid=(batch, heads, 2, nblk),
        in_specs=[pl.BlockSpec((rblk, hd), qmap(0)),
                  pl.BlockSpec((rblk, hd), kvmap(heads)),
                  pl.BlockSpec((rblk, hd), kvmap(2 * heads)),
                  pl.BlockSpec((rblk, hd), qmap(3 * heads)),
                  pl.BlockSpec((1, 7, RET_CHUNK, RET_CHUNK), lambda b, h, p, s: (h, 0, 0, 0)),
                  st_spec, st_spec],
        out_specs=[pl.BlockSpec((rblk, hd), qmap(0)), st_spec, st_spec],
        scratch_shapes=[pltpu.VMEM((seq // RET_CHUNK, hd, hd), BF16), pltpu.VMEM((hd, hd), F32)],
        compiler_params=_cparams("parallel", "parallel", "arbitrary", "arbitrary"),
        name="retention",
    )(proj, proj, proj, proj, tabs, sf0, sb0)


NA_QROWS = 8
NA_WROWS = 16
NA_KBLK = 4
NA_QSPLIT = 2


def _na_kernel(q_ref, k0, k1, k2, k3, v0, v1, v2, v3, kc_ref, vc_ref, bias_ref, o_ref):
    kb = NA_KBLK * GRID_W
    nq = q_ref.shape[0] // NA_QSPLIT
    for part in range(NA_QSPLIT):
        rows = slice(part * nq, (part + 1) * nq)
        q = q_ref[rows, :]
        scores = []
        for i, kr in enumerate((k0, k1, k2, k3)):
            sc = lax.dot_general(q, kr[...], _NT, preferred_element_type=F32)
            scores.append(sc + bias_ref[0, 0, rows, i * kb:(i + 1) * kb])
        scores.append(lax.dot_general(q, kc_ref[...], _NT, preferred_element_type=F32))
        m = scores[0].max(axis=-1, keepdims=True)
        for sc in scores[1:]:
            m = jnp.maximum(m, sc.max(axis=-1, keepdims=True))
        acc = jnp.zeros((nq, HEAD_DIM), F32)
        l = jnp.zeros((nq, 1), F32)
        for sc, vr in zip(scores, (v0, v1, v2, v3, vc_ref)):
            pr = jnp.exp(sc - m)
            l = l + pr.sum(axis=-1, keepdims=True)
            acc = acc + jnp.dot(pr.astype(BF16), vr[...], preferred_element_type=F32)
        o_ref[rows, :] = (acc / l).astype(o_ref.dtype)


def _na_bias_tables(rpb, rows):
    nrb = rows // NA_QROWS
    nr, nc = 2 * NA_ROWS - 1, 2 * NA_COLS - 1
    hi = lax.Precision.HIGHEST
    qc = np.arange(GRID_W)[:, None]
    kc = np.arange(GRID_W)[None, :]
    c0 = np.clip(qc - NA_COLS // 2, 0, GRID_W - NA_COLS)
    col_valid = (kc >= c0) & (kc < c0 + NA_COLS)
    sel_c = (np.arange(nc)[:, None, None] == (kc - qc + NA_COLS - 1)[None]).astype(np.float32)
    toep = jnp.einsum('haj,jqk->haqk', rpb.astype(F32), sel_c, precision=hi)
    qr = np.arange(NA_QROWS)[:, None]
    kr = np.arange(NA_WROWS)[None, :]
    tabs = []
    for rb in (0, 1, nrb - 1):
        w0 = int(np.clip(rb * NA_QROWS - NA_ROWS // 2, 0, rows - NA_WROWS))
        r = rb * NA_QROWS + qr
        krow = w0 + kr
        r0 = np.clip(r - NA_ROWS // 2, 0, rows - NA_ROWS)
        row_valid = (krow >= r0) & (krow < r0 + NA_ROWS)
        sel_r = ((np.arange(nr)[None, None, :] == (krow - r + NA_ROWS - 1)[:, :, None])
                 & row_valid[:, :, None]).astype(np.float32)
        b = jnp.einsum('rka,haqc->hrqkc', sel_r, toep, precision=hi)
        valid = row_valid[:, None, :, None] & col_valid[None, :, None, :]
        b = jnp.where(valid[None], b, NEG_INF)
        tabs.append(b.reshape(rpb.shape[0], NA_QROWS * GRID_W, NA_WROWS * GRID_W))
    return jnp.stack(tabs, axis=1)


def _neighbourhood_attention(proj, proj_ctx, bias, *, batch, seq, ctx_len):
    heads = bias.shape[0]
    hd = HEAD_DIM
    qblk = NA_QROWS * GRID_W
    kblk = NA_KBLK * GRID_W
    nrb = seq // qblk
    nkb = seq // kblk
    assert seq // GRID_W >= NA_WROWS + NA_QROWS
    qoff, koff, voff = 4 * heads, 5 * heads, 6 * heads

    def kmap(off, i):
        def f(b, h, r):
            u0 = jnp.clip(2 * r - 1, 0, nkb - NA_WROWS // NA_KBLK)
            return (b * nkb + u0 + i, off + h)
        return f

    def bias_map(b, h, r):
        return (h, (r > 0).astype(jnp.int32) + (r == nrb - 1).astype(jnp.int32), 0, 0)

    kspecs = [pl.BlockSpec((kblk, hd), kmap(koff, i)) for i in range(4)]
    vspecs = [pl.BlockSpec((kblk, hd), kmap(voff, i)) for i in range(4)]
    return pl.pallas_call(
        _na_kernel,
        out_shape=jax.ShapeDtypeStruct((batch * seq, heads * hd), BF16),
        grid=(batch, heads, nrb),
        in_specs=[pl.BlockSpec((qblk, hd), lambda b, h, r: (b * nrb + r, qoff + h))] + kspecs + vspecs + [
            pl.BlockSpec((ctx_len, hd), lambda b, h, r: (b, koff + h)),
            pl.BlockSpec((ctx_len, hd), lambda b, h, r: (b, voff + h)),
            pl.BlockSpec((1, 1, qblk, NA_WROWS * GRID_W), bias_map)],
        out_specs=pl.BlockSpec((qblk, hd), lambda b, h, r: (b * nrb + r, h)),
        compiler_params=_cparams("parallel", "parallel", "arbitrary"),
        name="neighbourhood_attention",
    )(*([proj] * 9), proj_ctx, proj_ctx, bias)


def _ctx_attn_kernel(q_ref, k_ref, v_ref, o_ref):
    sc = lax.dot_general(q_ref[...], k_ref[...], _NT, preferred_element_type=F32)
    pr = jnp.exp(sc - sc.max(axis=-1, keepdims=True))
    acc = jnp.dot(pr.astype(BF16), v_ref[...], preferred_element_type=F32)
    o_ref[...] = (acc / pr.sum(axis=-1, keepdims=True)).astype(o_ref.dtype)


def _ctx_attention(proj_ctx, *, batch, ctx_len, heads):
    hd = HEAD_DIM
    return pl.pallas_call(
        _ctx_attn_kernel,
        out_shape=jax.ShapeDtypeStruct((batch * ctx_len, heads * hd), BF16),
        grid=(batch, heads),
        in_specs=[pl.BlockSpec((ctx_len, hd), lambda b, h: (b, 4 * heads + h)),
                  pl.BlockSpec((ctx_len, hd), lambda b, h: (b, 5 * heads + h)),
                  pl.BlockSpec((ctx_len, hd), lambda b, h: (b, 6 * heads + h))],
        out_specs=pl.BlockSpec((ctx_len, hd), lambda b, h: (b, h)),
        compiler_params=_cparams("parallel", "parallel"),
        name="context_attention",
    )(proj_ctx, proj_ctx, proj_ctx)


OUTPROJ_SPLIT = 2


def _split_bf16(x):
    hi = x.astype(BF16)
    return hi, (x - hi.astype(F32)).astype(BF16)


def _outproj_kernel(*refs, odd):
    it = iter(refs)
    if odd:
        y_ref, z_ref, nws_ref = next(it), next(it), next(it)
    else:
        mr_ref, mn_ref = next(it), next(it)
    w_ref, lat_ref, g1_ref, nwf_ref, sh_ref, sc_ref = (next(it) for _ in range(6))
    wrh_ref, wrl_ref = next(it), next(it)
    lat2_ref, h_ref, lg_ref = next(it), next(it), next(it)

    nsplit = 1 if odd else OUTPROJ_SPLIT
    nr = lat_ref.shape[0] // nsplit
    for part in range(nsplit):
        rows = slice(part * nr, (part + 1) * nr)
        if odd:
            u = y_ref[rows, :].astype(F32) * _silu(z_ref[rows, :].astype(F32))
            u = u * lax.rsqrt(jnp.mean(u * u, axis=-1, keepdims=True) + EPS) * nws_ref[...]
            o = jnp.dot(u.astype(BF16), w_ref[...], preferred_element_type=F32)
        else:
            half = mr_ref.shape[1]
            o = (jnp.dot(mr_ref[rows, :], w_ref[0:half, :], preferred_element_type=F32)
                 + jnp.dot(mn_ref[rows, :], w_ref[half:2 * half, :], preferred_element_type=F32))
        lat2 = lat_ref[rows, :] + g1_ref[0] * o
        lat2_ref[rows, :] = lat2
        h = _rms_modulate(lat2, nwf_ref[...], sh_ref[0], sc_ref[0])
        h_ref[rows, :] = h
        hh, hl = _split_bf16(h)
        wrh = wrh_ref[...]
        lg_ref[rows, :] = (jnp.dot(hh, wrh, preferred_element_type=F32)
                           + jnp.dot(hl, wrh, preferred_element_type=F32)
                           + jnp.dot(hh, wrl_ref[...], preferred_element_type=F32))


def _outproj(mix, w, lat, g1, nwf, sh, sc, wrh, wrl, *, seg_rows, tm, ssd=None):
    t, d = lat.shape
    tm = min(tm, t)
    assert t % tm == 0 and seg_rows % tm == 0
    odd = ssd is not None
    row = lambda i: (i, 0)
    seg = pl.BlockSpec((1, 1, d), lambda i: ((i * tm) // seg_rows, 0, 0))
    const = lambda shape: pl.BlockSpec(shape, lambda i: (0,) * len(shape))
    if odd:
        y, zx, nws = ssd
        kin = y.shape[1]
        args = [y, zx, nws.reshape(1, kin)]
        in_specs = [pl.BlockSpec((tm, kin), row), pl.BlockSpec((tm, kin), row), const((1, kin))]
    else:
        mr, mn = mix
        args = [mr, mn]
        in_specs = [pl.BlockSpec((tm, mr.shape[1]), row), pl.BlockSpec((tm, mn.shape[1]), row)]
    args += [w, lat, g1, nwf.reshape(1, d), sh, sc, wrh, wrl]
    in_specs += [pl.BlockSpec(w.shape, lambda i: (0, 0), pipeline_mode=pl.Buffered(1)),
                 pl.BlockSpec((tm, d), row), seg, const((1, d)), seg, seg,
                 const(wrh.shape), const(wrl.shape)]
    nl = wrh.shape[1]
    return pl.pallas_call(
        functools.partial(_outproj_kernel, odd=odd),
        out_shape=[jax.ShapeDtypeStruct((t, d), F32),
                   jax.ShapeDtypeStruct((t, d), F32),
                   jax.ShapeDtypeStruct((t, nl), F32)],
        grid=(t // tm,),
        in_specs=in_specs,
        out_specs=[pl.BlockSpec((tm, d), row), pl.BlockSpec((tm, d), row), pl.BlockSpec((tm, nl), row)],
        compiler_params=_cparams("parallel"),
        name="outproj_odd" if odd else "outproj_even",
    )(*args)


ROUTER_TILE = 256
META_E, META_RANK, META_GATE = 0, 2, 4


def _router_kernel(lg_ref, rb_ref, meta_ref, cnt_ref, run_scr):
    tm = lg_ref.shape[0]
    i = pl.program_id(0)

    @pl.when(i == 0)
    def _():
        run_scr[...] = jnp.zeros_like(run_scr)

    big = 1e9
    lane = lax.broadcasted_iota(jnp.int32, (tm, LANES), 1)
    lanef = lane.astype(F32)
    scores = 1.0 / (1.0 + jnp.exp(-lg_ref[...]))
    biased = jnp.where(lane < N_EXPERTS, scores + rb_ref[...], NEG_INF)
    member = lane % EXPERTS_PER_GROUP
    best = jnp.full((tm, LANES), NEG_INF, F32)
    for k in range(1, EXPERTS_PER_GROUP):
        best = jnp.maximum(best, jnp.where(member >= k, biased + pltpu.roll(biased, k, axis=1), NEG_INF))
    grp = (lane // EXPERTS_PER_GROUP).astype(F32)
    top = best.max(axis=-1, keepdims=True)
    g_sel = jnp.where(best == top, grp, big).min(axis=-1, keepdims=True)
    v = jnp.where(grp == g_sel, biased, NEG_INF)
    i1 = jnp.where(v == v.max(axis=-1, keepdims=True), lanef, big).min(axis=-1, keepdims=True)
    v = jnp.where(lanef == i1, NEG_INF, v)
    i2 = jnp.where(v == v.max(axis=-1, keepdims=True), lanef, big).min(axis=-1, keepdims=True)
    oh1, oh2 = lanef == i1, lanef == i2
    s1 = jnp.where(oh1, scores, 0.0).sum(axis=-1, keepdims=True)
    s2 = jnp.where(oh2, scores, 0.0).sum(axis=-1, keepdims=True)
    den = s1 + s2
    oh = oh1.astype(F32) + oh2.astype(F32)
    earlier = (lax.broadcasted_iota(jnp.int32, (tm, tm), 0)
               > lax.broadcasted_iota(jnp.int32, (tm, tm), 1)).astype(BF16)
    base = jnp.dot(earlier, oh.astype(BF16), preferred_element_type=F32) + run_scr[0:1, :]
    r1 = jnp.where(oh1, base, 0.0).sum(axis=-1, keepdims=True)
    r2 = jnp.where(oh2, base, 0.0).sum(axis=-1, keepdims=True)
    total = run_scr[0:1, :] + oh.sum(axis=0, keepdims=True)
    run_scr[0:1, :] = total
    cnt_ref[...] = jnp.broadcast_to(total, cnt_ref.shape)
    meta = jnp.zeros((tm, LANES), F32)
    for ln, val in ((META_E, i1), (META_E + 1, i2), (META_RANK, r1), (META_RANK + 1, r2),
                    (META_GATE, s1 / den), (META_GATE + 1, s2 / den)):
        meta = jnp.where(lane == ln, val, meta)
    meta_ref[...] = meta


def _router(logits, router_bias):
    t = logits.shape[0]
    tm = ROUTER_TILE
    rb = jnp.zeros((1, LANES), F32).at[0, :N_EXPERTS].set(router_bias.astype(F32))
    return pl.pallas_call(
        _router_kernel,
        out_shape=[jax.ShapeDtypeStruct((t, LANES), F32), jax.ShapeDtypeStruct((SUBLANES, LANES), F32)],
        grid=(t // tm,),
        in_specs=[pl.BlockSpec((tm, LANES), lambda i: (i, 0)), pl.BlockSpec((1, LANES), lambda i: (0, 0))],
        out_specs=[pl.BlockSpec((tm, LANES), lambda i: (i, 0)),
                   pl.BlockSpec((SUBLANES, LANES), lambda i: (0, 0))],
        scratch_shapes=[pltpu.VMEM((SUBLANES, LANES), F32)],
        compiler_params=_cparams("arbitrary"),
        name="moe_router",
    )(logits, rb)


def _moe_plan(meta, cnt):
    t = meta.shape[0]
    blk = MOE_BLOCK
    counts = cnt[0, :N_EXPERTS].astype(jnp.int32)
    ends = jnp.cumsum(counts)
    offsets = ends - counts
    e = meta[:, META_E:META_E + TOP_K].astype(jnp.int32)
    rank = meta[:, META_RANK:META_RANK + TOP_K].astype(jnp.int32)
    onehot = e[..., None] == jnp.arange(N_EXPERTS, dtype=jnp.int32)
    dest = jnp.sum(jnp.where(onehot, offsets, 0), axis=-1) + rank
    n_tiles = t * TOP_K // blk
    tile_start = jnp.arange(n_tiles, dtype=jnp.int32) * blk
    def count_le(sorted_vals, query):
        return jnp.sum(sorted_vals[None, :] <= query[:, None], axis=-1).astype(jnp.int32)

    first_e = count_le(ends, tile_start)
    last_e = count_le(ends, tile_start + blk - 1)
    nsteps = last_e - first_e + 1
    step_end = jnp.cumsum(nsteps)
    step_start = step_end - nsteps
    s = jnp.arange(n_tiles + N_EXPERTS - 1, dtype=jnp.int32)
    valid = s < step_end[-1]
    tile_of = jnp.minimum(count_le(step_end, s), n_tiles - 1)
    exp_of = jnp.where(valid, first_e[tile_of] + s - step_start[tile_of], last_e[-1])
    exp_of = jnp.clip(exp_of, 0, N_EXPERTS - 1).astype(jnp.int32)
    lo = jnp.clip(offsets[exp_of] - tile_of * blk, 0, blk)
    hi = jnp.clip(ends[exp_of] - tile_of * blk, 0, blk)
    lo = jnp.where(valid, lo, 0).astype(jnp.int32)
    hi = jnp.where(valid, hi, 0).astype(jnp.int32)
    first = (valid & (s == step_start[tile_of])).astype(jnp.int32)
    return dest.astype(jnp.int32), (tile_of, exp_of, lo, hi, first)


DMA_UNROLL = 8


def _dispatch_kernel(*refs, tm, src_tiles):
    dest_ref, srcs, xs_hbm, sem = refs[0], refs[1:-2], refs[-2], refs[-1]
    i = pl.program_id(0)

    def scatter_from(src):
        def body(r, c):
            for k in range(TOP_K):
                pltpu.make_async_copy(src.at[pl.ds(r, 1), :],
                                      xs_hbm.at[pl.ds(dest_ref[0, 0, TOP_K * r + k], 1), :], sem).start()
            return c
        lax.fori_loop(0, tm, body, 0, unroll=DMA_UNROLL)
        for _ in range(TOP_K):
            pltpu.make_async_copy(src, xs_hbm.at[pl.ds(0, tm), :], sem).wait()

    first_tile = 0
    for src, ntile in zip(srcs, src_tiles):
        @pl.when((i >= first_tile) & (i < first_tile + ntile))
        def _(src=src):
            scatter_from(src)
        first_tile += ntile


def _moe_dispatch(hs, dest):
    d = hs[0].shape[1]
    t = dest.shape[0]
    tm = 256
    nt = t // tm
    src_tiles = tuple(h.shape[0] // tm for h in hs)
    assert sum(src_tiles) == nt and all(h.shape[0] % tm == 0 for h in hs)
    starts = np.cumsum((0,) + src_tiles[:-1])
    src_specs = [pl.BlockSpec((tm, d), lambda i, s=int(s), n=n: (jnp.clip(i - s, 0, n - 1), 0))
                 for s, n in zip(starts, src_tiles)]
    kern = functools.partial(_dispatch_kernel, tm=tm, src_tiles=src_tiles)
    return pl.pallas_call(
        kern,
        out_shape=jax.ShapeDtypeStruct((TOP_K * t, d), F32),
        grid=(nt,),
        in_specs=[pl.BlockSpec((1, 1, TOP_K * tm), lambda i: (i, 0, 0), memory_space=pltpu.SMEM)]
        + src_specs,
        out_specs=pl.BlockSpec(memory_space=pl.ANY),
        scratch_shapes=[pltpu.SemaphoreType.DMA(())],
        compiler_params=_cparams("arbitrary"),
        name="moe_dispatch",
    )(dest.reshape(nt, 1, TOP_K * tm), *hs)


EXPERT_SPLIT = 1


def _expert_kernel(tile_ref, exp_ref, lo_ref, hi_ref, first_ref, x_ref, wg_ref, wu_ref, wd_ref,
                   o_ref, wg_s, wu_s, wd_s, cur_e):
    s = pl.program_id(0)
    lo, hi, e = lo_ref[s], hi_ref[s], exp_ref[s]

    @pl.when(s == 0)
    def _():
        cur_e[0] = -1

    @pl.when(hi > lo)
    def _():
        @pl.when(cur_e[0] != e)
        def _():
            wg_s[...] = wg_ref[0, 0].astype(BF16)
            wu_s[...] = wu_ref[0, 0].astype(BF16)
            wd_s[...] = wd_ref[0, 0].astype(BF16)
            cur_e[0] = e

        nr = x_ref.shape[0] // EXPERT_SPLIT
        ys = []
        for part in range(EXPERT_SPLIT):
            x = x_ref[part * nr:(part + 1) * nr, :].astype(BF16)
            a = jnp.dot(x, wg_s[...], preferred_element_type=F32)
            u = jnp.dot(x, wu_s[...], preferred_element_type=F32)
            hid = (_silu(a) * u).astype(BF16)
            ys.append(jnp.dot(hid, wd_s[...], preferred_element_type=F32))
        y = jnp.concatenate(ys, axis=0)
        row = lax.broadcasted_iota(jnp.int32, (y.shape[0], 1), 0)
        mine = (row >= lo) & (row < hi)

        @pl.when(first_ref[s] == 1)
        def _():
            o_ref[...] = jnp.where(mine, y, 0.0)

        @pl.when(first_ref[s] == 0)
        def _():
            o_ref[...] = jnp.where(mine, y, o_ref[...])


def _expert_ffn(xs, plan, w_gate, w_up, w_down, layer):
    n_rows, d = xs.shape
    hid = w_gate.shape[3]
    n_steps = plan[0].shape[0]
    xmap = lambda s, tile, exp, lo, hi, first: (tile[s], 0)
    wmap = lambda s, tile, exp, lo, hi, first: (layer, exp[s], 0, 0)
    return pl.pallas_call(
        _expert_kernel,
        out_shape=jax.ShapeDtypeStruct((n_rows, d), F32),
        grid_spec=pltpu.PrefetchScalarGridSpec(
            num_scalar_prefetch=5,
            grid=(n_steps,),
            in_specs=[pl.BlockSpec((MOE_BLOCK, d), xmap),
                      pl.BlockSpec((1, 1, d, hid), wmap),
                      pl.BlockSpec((1, 1, d, hid), wmap),
                      pl.BlockSpec((1, 1, hid, d), wmap)],
            out_specs=pl.BlockSpec((MOE_BLOCK, d), xmap),
            scratch_shapes=[pltpu.VMEM((d, hid), BF16), pltpu.VMEM((d, hid), BF16),
                            pltpu.VMEM((hid, d), BF16), pltpu.SMEM((1,), jnp.int32)]),
        compiler_params=_cparams("arbitrary"),
        name="moe_experts",
    )(*plan, xs, w_gate, w_up, w_down)


def _combine_kernel(pos_ref, posn_ref, ys_hbm, meta_ref, lat_ref, g2_ref, nw_ref, o_ref, buf, sem,
                    *, final, tc):
    i = pl.program_id(0)
    nt = pl.num_programs(0)
    slot = i % 2
    n = TOP_K * tc

    def issue(idx_ref, sl):
        def body(r, c):
            pltpu.make_async_copy(ys_hbm.at[pl.ds(idx_ref[0, 0, r], 1), :],
                                  buf.at[sl, pl.ds(r, 1), :], sem.at[sl]).start()
            return c
        lax.fori_loop(0, n, body, 0, unroll=DMA_UNROLL)

    @pl.when(i == 0)
    def _():
        issue(pos_ref, 0)

    @pl.when(i + 1 < nt)
    def _():
        issue(posn_ref, 1 - slot)

    pltpu.make_async_copy(ys_hbm.at[pl.ds(0, n), :], buf.at[slot], sem.at[slot]).wait()
    gk = meta_ref[:, META_GATE:META_GATE + TOP_K]
    y = gk[:, 0:1] * buf[slot, 0:tc, :] + gk[:, 1:2] * buf[slot, tc:2 * tc, :]
    x = lat_ref[...] + g2_ref[0] * y
    if final:
        x = x * lax.rsqrt(jnp.mean(x * x, axis=-1, keepdims=True) + EPS) * nw_ref[...]
    o_ref[...] = x


def _moe_combine(ys, pos, meta, lat, g2, nw, *, seg_rows, final):
    t, d = lat.shape
    tc = 256
    nt = t // tc
    n = TOP_K * tc
    kern = functools.partial(_combine_kernel, final=final, tc=tc)
    posb = pos.reshape(nt, tc, TOP_K).transpose(0, 2, 1).reshape(nt, 1, n)
    return pl.pallas_call(
        kern,
        out_shape=jax.ShapeDtypeStruct((t, d), F32),
        grid=(nt,),
        in_specs=[pl.BlockSpec((1, 1, n), lambda i: (i, 0, 0), memory_space=pltpu.SMEM),
                  pl.BlockSpec((1, 1, n), lambda i: (jnp.minimum(i + 1, nt - 1), 0, 0),
                               memory_space=pltpu.SMEM),
                  pl.BlockSpec(memory_space=pl.ANY),
                  pl.BlockSpec((tc, LANES), lambda i: (i, 0)),
                  pl.BlockSpec((tc, d), lambda i: (i, 0)),
                  pl.BlockSpec((1, 1, d), lambda i: ((i * tc) // seg_rows, 0, 0)),
                  pl.BlockSpec((1, d), lambda i: (0, 0))],
        out_specs=pl.BlockSpec((tc, d), lambda i: (i, 0)),
        scratch_shapes=[pltpu.VMEM((2, n, d), F32), pltpu.SemaphoreType.DMA((2,))],
        compiler_params=_cparams("arbitrary"),
        name="moe_combine",
    )(posb, posb, ys, meta, lat, g2, nw.reshape(1, d))


CONV_HALO = 16


def _conv_kernel(prev_ref, cur_ref, next_ref, w_ref, b_ref, o_ref, *, seq, rblk):
    i = pl.program_id(0)
    first = (i * rblk) % seq == 0
    last = ((i + 1) * rblk) % seq == 0
    half = CONV_HALO // 2
    pv = jnp.where(first, 0.0, prev_ref[...].astype(F32))[half:, :]
    nx = jnp.where(last, 0.0, next_ref[...].astype(F32))[:half, :]
    xm = jnp.concatenate([pv, cur_ref[...].astype(F32), nx], axis=0)
    n = rblk + 2 * half
    acc = jnp.zeros((rblk, xm.shape[1]), F32) + b_ref[...]
    for k in range(SSD_CONV):
        sh = (SSD_CONV // 2 - k) % n
        xs = xm if sh == 0 else pltpu.roll(xm, sh, axis=0)
        acc = acc + w_ref[k:k + 1, :] * xs[half:half + rblk, :]
    o_ref[...] = _silu(acc).astype(o_ref.dtype)


def _conv_silu(zx, conv_w, conv_b, *, seq, col_off):
    t = zx.shape[0]
    c = conv_w.shape[1]
    rblk, tn = min(512, seq), 1024
    assert seq % rblk == 0 and c % tn == 0 and col_off % tn == 0
    cb = col_off // tn
    hb = rblk // CONV_HALO
    nhb = t // CONV_HALO
    kern = functools.partial(_conv_kernel, seq=seq, rblk=rblk)
    wpad = jnp.zeros((SUBLANES, c), F32).at[:SSD_CONV].set(conv_w.astype(F32))
    return pl.pallas_call(
        kern,
        out_shape=jax.ShapeDtypeStruct((t, c), BF16),
        grid=(t // rblk, c // tn),
        in_specs=[pl.BlockSpec((CONV_HALO, tn), lambda i, j: (jnp.maximum(i * hb - 1, 0), cb + j)),
                  pl.BlockSpec((rblk, tn), lambda i, j: (i, cb + j)),
                  pl.BlockSpec((CONV_HALO, tn), lambda i, j: (jnp.minimum((i + 1) * hb, nhb - 1), cb + j)),
                  pl.BlockSpec((SUBLANES, tn), lambda i, j: (0, j)),
                  pl.BlockSpec((1, tn), lambda i, j: (0, j))],
        out_specs=pl.BlockSpec((rblk, tn), lambda i, j: (i, j)),
        compiler_params=_cparams("parallel", "parallel"),
        name="ssd_conv_silu",
    )(zx, zx, zx, wpad, conv_b.astype(F32).reshape(1, c))


LOG2E = 1.4426950408889634


def _softplus(x):
    return jnp.maximum(x, 0.0) + jnp.log1p(jnp.exp(-jnp.abs(x)))


def _ssd_kernel(*refs, reverse, second, nchunk, nblk, groups):
    it = iter(refs)
    x_ref, b_ref, c_ref, dt_ref, prm_ref, s0_ref = (next(it) for _ in range(6))
    if second:
        yprev_ref, dsk_ref = next(it), next(it)
    y_ref, sfin_ref, s_scr = next(it), next(it), next(it)
    acum_s, acumt_s, dtt_s, fwt_s, etot_s, cb_s, bgt_s, yint_s = (next(it) for _ in range(8))
    q = SSD_CHUNK
    nst = SSD_STATE
    gw = x_ref.shape[1] // groups
    pw = 2 * SSD_HEAD_DIM
    s = pl.program_id(1)

    @pl.when(s == 0)
    def _():
        s_scr[...] = s0_ref[0]

    ii = lax.broadcasted_iota(jnp.int32, (q, q), 0)
    jj = lax.broadcasted_iota(jnp.int32, (q, q), 1)
    tri = (jj >= ii) if reverse else (ii >= jj)
    tri_bf = tri.astype(BF16)
    lo_half = lax.broadcasted_iota(jnp.int32, (q, pw), 1) < SSD_HEAD_DIM
    a_neg = prm_ref[0:1, :]
    dtb = prm_ref[1:2, :]

    def chunk(ci, carry):
        c = (nchunk - 1 - ci) if reverse else ci
        rows = pl.ds(pl.multiple_of(c * q, q), q)
        dt = _softplus(dt_ref[rows, :] + dtb)
        da = dt * a_neg
        hi = da.astype(BF16)
        r1 = da - hi.astype(F32)
        mid = r1.astype(BF16)
        lo = (r1 - mid.astype(F32)).astype(BF16)
        acum = (jnp.dot(tri_bf, hi, preferred_element_type=F32)
                + jnp.dot(tri_bf, mid, preferred_element_type=F32)
                + jnp.dot(tri_bf, lo, preferred_element_type=F32))
        acum = acum * LOG2E
        total = acum[0:1, :] if reverse else acum[q - 1:q, :]
        acum_s[...] = acum
        acumt_s[...] = acum.T
        dtt_s[...] = dt.T
        fwt_s[...] = (jnp.exp2(total - acum) * dt).T
        etot_s[...] = jnp.broadcast_to(jnp.exp2(total), etot_s.shape)
        for g in range(groups):
            bg = b_ref[rows, g * nst:(g + 1) * nst]
            cg = c_ref[rows, g * nst:(g + 1) * nst]
            cb_s[...] = lax.dot_general(cg, bg, _NT, preferred_element_type=F32)
            bgt_s[...] = bg.astype(F32).T
            yint_s[...] = jnp.dot(cg, s_scr[g].astype(BF16), preferred_element_type=F32)
            for pr in range(gw // pw):
                h1 = g * (gw // SSD_HEAD_DIM) + 2 * pr
                h2 = h1 + 1
                cols = slice(g * gw + pr * pw, g * gw + (pr + 1) * pw)
                gcols = slice(pr * pw, (pr + 1) * pw)
                ws, bs, acols = [], [], []
                for h in (h1, h2):
                    acol = jnp.broadcast_to(acum_s[:, h:h + 1], (q, q))
                    lm = jnp.where(tri, jnp.exp2(acol - acumt_s[h:h + 1, :]), 0.0)
                    ws.append((cb_s[...] * lm * dtt_s[h:h + 1, :]).astype(BF16))
                    bs.append((bgt_s[...] * fwt_s[h:h + 1, :]).astype(BF16))
                    acols.append(acol)
                xp = x_ref[rows, cols].astype(F32)
                base = yint_s[:, gcols] * jnp.exp2(jnp.where(lo_half, acols[0], acols[1]))
                if second:
                    base = base + yprev_ref[rows, cols] + dsk_ref[:, cols] * xp
                lhs = jnp.concatenate([jnp.concatenate(ws, axis=1), jnp.concatenate(bs, axis=1)], axis=0)
                rhs = jnp.concatenate([jnp.where(lo_half, xp, 0.0), jnp.where(lo_half, 0.0, xp)],
                                      axis=0).astype(BF16)
                res = jnp.dot(lhs, rhs, preferred_element_type=F32)
                y_ref[rows, cols] = (res[0:q, :] + base).astype(y_ref.dtype)
                dec = jnp.where(lo_half[0:1, :], etot_s[0:1, h1:h1 + 1], etot_s[0:1, h2:h2 + 1])
                s_scr[g, :, gcols] = dec * s_scr[g, :, gcols] + res[q:q + nst, :]
        return carry

    lax.fori_loop(0, nchunk, chunk, 0)

    @pl.when(s == nblk - 1)
    def _():
        sfin_ref[0] = s_scr[...]


def _ssd_pass(act, dt, prm, s0, *, batch, seq, reverse, yprev=None, dskip=None):
    groups, nst = SSD_GROUPS, SSD_STATE
    inner = act.shape[1] - 2 * groups * nst
    rblk = min(256, seq)
    nblk = seq // rblk
    second = yprev is not None
    bc = groups * nst
    rmap = (lambda b, s: (b * nblk + nblk - 1 - s, 0)) if reverse else (lambda b, s: (b * nblk + s, 0))
    cmap = lambda k: ((lambda b, s: (b * nblk + nblk - 1 - s, k)) if reverse
                      else (lambda b, s: (b * nblk + s, k)))
    st_spec = pl.BlockSpec((1, groups, nst, inner // groups), lambda b, s: (b, 0, 0, 0))
    args = [act, act, act, dt, prm, s0]
    in_specs = [pl.BlockSpec((rblk, inner), rmap),
                pl.BlockSpec((rblk, bc), cmap(inner // bc)),
                pl.BlockSpec((rblk, bc), cmap(inner // bc + 1)),
                pl.BlockSpec((rblk, dt.shape[1]), rmap),
                pl.BlockSpec(prm.shape, lambda b, s: (0, 0)),
                st_spec]
    if second:
        args += [yprev, dskip]
        in_specs += [pl.BlockSpec((rblk, inner), rmap), pl.BlockSpec((1, inner), lambda b, s: (0, 0))]
    kern = functools.partial(_ssd_kernel, reverse=reverse, second=second, nchunk=rblk // SSD_CHUNK,
                             nblk=nblk, groups=groups)
    return pl.pallas_call(
        kern,
        out_shape=[jax.ShapeDtypeStruct((batch * seq, inner), BF16 if second else F32),
                   jax.ShapeDtypeStruct((batch, groups, nst, inner // groups), F32)],
        grid=(batch, nblk),
        in_specs=in_specs,
        out_specs=[pl.BlockSpec((rblk, inner), rmap), st_spec],
        scratch_shapes=[pltpu.VMEM((groups, nst, inner // groups), F32),
                        pltpu.VMEM((SSD_CHUNK, LANES), F32),
                        pltpu.VMEM((LANES, SSD_CHUNK), F32),
                        pltpu.VMEM((LANES, SSD_CHUNK), F32),
                        pltpu.VMEM((LANES, SSD_CHUNK), F32),
                        pltpu.VMEM((SUBLANES, LANES), F32),
                        pltpu.VMEM((SSD_CHUNK, SSD_CHUNK), F32),
                        pltpu.VMEM((nst, SSD_CHUNK), F32),
                        pltpu.VMEM((SSD_CHUNK, inner // groups), F32)],
        compiler_params=_cparams("parallel", "arbitrary"),
        name="ssd_scan_bwd" if reverse else "ssd_scan_fwd",
    )(*args)


def _rope_tables(seq):
    pos = jnp.arange(seq)
    rows = (pos // GRID_W).astype(F32)
    cols = (pos % GRID_W).astype(F32)
    half = HEAD_DIM // 2
    inv = ROPE_BASE ** (-jnp.arange(0, half, 2, dtype=F32) / half)
    ang = jnp.concatenate([rows[:, None] * inv, cols[:, None] * inv], axis=-1)
    cos, sin = jnp.cos(ang), jnp.sin(ang)
    return jnp.concatenate([cos, cos], axis=-1), jnp.concatenate([-sin, sin], axis=-1)


def _deinterleave_perm(n_cols, n_rot_cols):
    within = np.concatenate([np.arange(0, HEAD_DIM, 2), np.arange(1, HEAD_DIM, 2)])
    perm = np.arange(n_cols)
    for h in range(n_rot_cols // HEAD_DIM):
        perm[h * HEAD_DIM:(h + 1) * HEAD_DIM] = h * HEAD_DIM + within
    return perm


def _moe_layer(hs, logits, router_bias, w_gate, w_up, w_down, layer):
    meta, cnt = _router(logits, router_bias)
    dest, plan = _moe_plan(meta, cnt)
    xs = _moe_dispatch(hs, dest)
    ys = _expert_ffn(xs, plan, w_gate, w_up, w_down, layer)
    return ys, dest, meta


def kernel(x, c, ctx, c_ctx, w_mod, b_mod, norm_mix, norm_ffn, norm_final, w_in_even, w_out_even,
           ret_decay, na_rpb, w_in_odd, conv_w, conv_b, dt_bias, a_log, d_skip, ssd_norm, w_out_odd,
           w_router, router_bias, w_gate, w_up, w_down):
    batch, seq, d = x.shape
    ctx_len = ctx.shape[1]
    depth = w_mod.shape[0]
    assert batch + 1 <= SUBLANES
    tl, tc = batch * seq, batch * ctx_len

    cvec = jnp.zeros((SUBLANES, d), F32).at[0].set(c_ctx).at[1:1 + batch].set(c)
    mod = _modulation(cvec, w_mod, b_mod)

    def mod_vecs(i, k):
        m = mod[i, :, k * d:(k + 1) * d]
        return m[1:1 + batch].reshape(batch, 1, d), m[0:1].reshape(1, 1, d)

    wr = jnp.zeros((d, LANES), F32).at[:, :N_EXPERTS].set(w_router.astype(F32))
    wrh, wrl = _split_bf16(wr)

    lat = x.reshape(tl, d)
    cx = ctx.reshape(tc, d)
    out = None
    for i in range(depth):
        last = i == depth - 1
        j = i // 2
        (sh1, csh1), (sc1, csc1), (g1, cg1) = mod_vecs(i, 0), mod_vecs(i, 1), mod_vecs(i, 2)
        (sh2, csh2), (sc2, csc2), (g2, cg2) = mod_vecs(i, 3), mod_vecs(i, 4), mod_vecs(i, 5)
        if i % 2 == 0:
            ret_heads = w_out_even.shape[1] // (2 * HEAD_DIM)
            rot_cols = 2 * ret_heads * HEAD_DIM
            w_in = w_in_even[j][:, _deinterleave_perm(w_in_even.shape[2], rot_cols)].astype(BF16)
            cos, sin = _rope_tables(seq)
            ones, zeros = jnp.ones((tc, HEAD_DIM), F32), jnp.zeros((tc, HEAD_DIM), F32)
            proj_l, = _inproj(lat, norm_mix[i], sh1, sc1, w_in, seg_rows=seq, tm=1024, tn=1024,
                              even=True, cos=cos, sin=sin)
            proj_c, = _inproj(cx, norm_mix[i], csh1, csc1, w_in, seg_rows=tc, tm=tc, tn=1024,
                              even=True, cos=ones, sin=zeros)
            tabs = _retention_tables(ret_decay[j])
            zero = jnp.zeros((batch, ret_heads, HEAD_DIM, HEAD_DIM), F32)
            ret_c, s_f, s_b = _retention(proj_c, tabs, zero, zero, batch=batch, seq=ctx_len, rblk=ctx_len)
            ret_l, _, _ = _retention(proj_l, tabs, s_f, s_b, batch=batch, seq=seq, rblk=2048)
            bias = _na_bias_tables(na_rpb[j], seq // GRID_W)
            na_l = _neighbourhood_attention(proj_l, proj_c, bias, batch=batch, seq=seq, ctx_len=ctx_len)
            w_out = w_out_even[j].astype(BF16)
            mix_l, ssd_l, mix_c, ssd_c = (ret_l, na_l), None, None, None
            if not last:
                mix_c = (ret_c, _ctx_attention(proj_c, batch=batch, ctx_len=ctx_len, heads=ret_heads))
            tm_out = 512
        else:
            inner = w_out_odd.shape[1]
            n_main = inner + conv_w.shape[2]
            w_in = w_in_odd[j]
            w_main = w_in[:, :n_main].astype(BF16)
            heads = w_in.shape[1] - n_main
            w_dt = jnp.zeros((d, LANES), F32).at[:, :heads].set(w_in[:, n_main:]).astype(BF16)
            zx_l, dt_l = _inproj(lat, norm_mix[i], sh1, sc1, w_main, seg_rows=seq, tm=1024, tn=1024, wdt=w_dt)
            zx_c, dt_c = _inproj(cx, norm_mix[i], csh1, csc1, w_main, seg_rows=tc, tm=tc, tn=1024, wdt=w_dt)
            act_l = _conv_silu(zx_l, conv_w[j], conv_b[j], seq=seq, col_off=inner)
            act_c = _conv_silu(zx_c, conv_w[j], conv_b[j], seq=ctx_len, col_off=inner)
            a_neg = -jnp.exp(a_log[j].astype(F32))

            def prm(dr):
                return (jnp.zeros((SUBLANES, LANES), F32).at[0, :heads].set(a_neg[dr])
                        .at[1, :heads].set(dt_bias[j, dr].astype(F32)))

            dsk = jnp.repeat((d_skip[j, 0] + d_skip[j, 1]).astype(F32), inner // heads)[None, :]
            zero = jnp.zeros((batch, SSD_GROUPS, SSD_STATE, inner // SSD_GROUPS), F32)
            yf_c, s_f = _ssd_pass(act_c, dt_c, prm(0), zero, batch=batch, seq=ctx_len, reverse=False)
            y_c, s_b = _ssd_pass(act_c, dt_c, prm(1), zero, batch=batch, seq=ctx_len, reverse=True,
                                 yprev=yf_c, dskip=dsk)
            yf_l, _ = _ssd_pass(act_l, dt_l, prm(0), s_f, batch=batch, seq=seq, reverse=False)
            y_l, _ = _ssd_pass(act_l, dt_l, prm(1), s_b, batch=batch, seq=seq, reverse=True,
                               yprev=yf_l, dskip=dsk)
            w_out = w_out_odd[j].astype(BF16)
            mix_l, ssd_l = None, (y_l, zx_l, ssd_norm[j].astype(F32))
            mix_c, ssd_c = None, (None if last else (y_c, zx_c, ssd_norm[j].astype(F32)))
            tm_out = 256

        nwf = norm_ffn[i].astype(F32)
        lat2, h_l, lg_l = _outproj(mix_l, w_out, lat, g1, nwf, sh2, sc2, wrh, wrl, seg_rows=seq,
                                   tm=tm_out, ssd=ssd_l)
        if last:
            ys, pos, meta = _moe_layer([h_l], lg_l, router_bias, w_gate, w_up, w_down, i)
            out = _moe_combine(ys, pos, meta, lat2, g2, norm_final.astype(F32), seg_rows=seq, final=True)
        else:
            cx2, h_c, lg_c = _outproj(mix_c, w_out, cx, cg1, nwf, csh2, csc2, wrh, wrl, seg_rows=tc,
                                      tm=tm_out, ssd=ssd_c)
            ys, pos, meta = _moe_layer([h_c, h_l], jnp.concatenate([lg_c, lg_l], axis=0), router_bias,
                                       w_gate, w_up, w_down, i)
            ones = jnp.ones((d,), F32)
            cx = _moe_combine(ys, pos[:tc], meta[:tc], cx2, cg2, ones, seg_rows=tc, final=False)
            lat = _moe_combine(ys, pos[tc:], meta[tc:], lat2, g2, ones, seg_rows=seq, final=False)
    return out.reshape(batch, seq, d)
```

```python
import functools

import numpy as np
import jax
import jax.numpy as jnp
from jax import lax
from jax.experimental import pallas as pl
from jax.experimental.pallas import tpu as pltpu

GRID_W = 64
EPS = 1e-6
HEAD_DIM = 128
ROPE_BASE = 10000.0
RET_CHUNK = 128
NA_ROWS = 8
NA_COLS = 16
SSD_HEAD_DIM = 64
SSD_GROUPS = 8
SSD_STATE = 128
SSD_CONV = 5
SSD_CHUNK = 128
N_EXPERTS = 32
N_GROUPS = 8
EXPERTS_PER_GROUP = N_EXPERTS // N_GROUPS
TOP_K = 2
MOE_BLOCK = 512

LANES = 128
SUBLANES = 8
VMEM_LIMIT_BYTES = 56 * 1024 * 1024

NEG_INF = -1e30
F32 = jnp.float32
BF16 = jnp.bfloat16


def _cparams(*sem):
    return pltpu.CompilerParams(dimension_semantics=sem, vmem_limit_bytes=VMEM_LIMIT_BYTES)


def _silu(x):
    return x * (1.0 / (1.0 + jnp.exp(-x)))


def _mod_kernel(c_ref, w_ref, b_ref, o_ref):
    c = _silu(c_ref[...]).astype(BF16)
    o_ref[0] = jnp.dot(c, w_ref[0].astype(BF16), preferred_element_type=F32) + b_ref[0]


def _modulation(cvec, w_mod, b_mod):
    depth, d, n = w_mod.shape
    tn = 1024
    return pl.pallas_call(
        _mod_kernel,
        out_shape=jax.ShapeDtypeStruct((depth, SUBLANES, n), F32),
        grid=(depth, n // tn),
        in_specs=[
            pl.BlockSpec((SUBLANES, d), lambda i, j: (0, 0)),
            pl.BlockSpec((1, d, tn), lambda i, j: (i, 0, j)),
            pl.BlockSpec((1, 1, tn), lambda i, j: (i, 0, j)),
        ],
        out_specs=pl.BlockSpec((1, SUBLANES, tn), lambda i, j: (i, 0, j)),
        compiler_params=_cparams("parallel", "parallel"),
        name="adaln_modulation",
    )(cvec, w_mod, b_mod.reshape(depth, 1, n))


def _rms_modulate(x, nw, sh, sc):
    r = lax.rsqrt(jnp.mean(x * x, axis=-1, keepdims=True) + EPS)
    return (x * r * nw) * (1.0 + sc) + sh


def _rope(t, cos, sin):
    return t * cos + pltpu.roll(t, HEAD_DIM // 2, axis=1) * sin


def _inproj_kernel(*refs, even, has_dt, tn):
    it = iter(refs)
    x_ref, nw_ref, sh_ref, sc_ref, w_ref = (next(it) for _ in range(5))
    if even:
        cos_ref, sin_ref = next(it), next(it)
    if has_dt:
        wdt_ref = next(it)
    o_ref = next(it)
    if has_dt:
        dt_ref = next(it)
    h_scr = next(it)

    j = pl.program_id(1)

    @pl.when(j == 0)
    def _():
        h = _rms_modulate(x_ref[...], nw_ref[...], sh_ref[0], sc_ref[0]).astype(BF16)
        h_scr[...] = h
        if has_dt:
            dt_ref[...] = jnp.dot(h, wdt_ref[...], preferred_element_type=F32)

    acc = jnp.dot(h_scr[...], w_ref[...], preferred_element_type=F32)

    if not even:
        o_ref[...] = acc.astype(o_ref.dtype)
        return

    heads_per_tile = tn // HEAD_DIM
    tiles_per_kilo = 1024 // tn
    seg = j // tiles_per_kilo
    scale = HEAD_DIM ** -0.5

    @pl.when(seg <= 1)
    def _():
        mul = jnp.where(seg == 1, scale, 1.0).astype(F32)
        cos, sin = cos_ref[...], sin_ref[...]
        for hh in range(heads_per_tile):
            sl = slice(hh * HEAD_DIM, (hh + 1) * HEAD_DIM)
            o_ref[:, sl] = (_rope(acc[:, sl], cos, sin) * mul).astype(o_ref.dtype)

    @pl.when(seg == 4)
    def _():
        o_ref[...] = (acc * scale).astype(o_ref.dtype)

    @pl.when((seg == 2) | (seg == 3) | (seg >= 5))
    def _():
        o_ref[...] = acc.astype(o_ref.dtype)


def _seg_spec(d, seg_rows, tm):
    return pl.BlockSpec((1, 1, d), lambda i, j: ((i * tm) // seg_rows, 0, 0))


def _inproj(x, nw, sh, sc, w, *, seg_rows, tm, tn, even=False, cos=None, sin=None, wdt=None):
    t, d = x.shape
    n = w.shape[1]
    tm = min(tm, t)
    assert t % tm == 0 and n % tn == 0 and seg_rows % tm == 0
    args, in_specs = [x], [pl.BlockSpec((tm, d), lambda i, j: (i, 0))]
    args += [nw.reshape(1, d), sh, sc, w]
    in_specs += [pl.BlockSpec((1, d), lambda i, j: (0, 0)), _seg_spec(d, seg_rows, tm),
                 _seg_spec(d, seg_rows, tm), pl.BlockSpec((d, tn), lambda i, j: (0, j))]
    if even:
        lblk = cos.shape[0] // tm
        args += [cos, sin]
        in_specs += [pl.BlockSpec((tm, HEAD_DIM), lambda i, j: (i % lblk, 0))] * 2
    if wdt is not None:
        args.append(wdt)
        in_specs.append(pl.BlockSpec(wdt.shape, lambda i, j: (0, 0)))
    out_shape = [jax.ShapeDtypeStruct((t, n), BF16)]
    out_specs = [pl.BlockSpec((tm, tn), lambda i, j: (i, j))]
    if wdt is not None:
        out_shape.append(jax.ShapeDtypeStruct((t, wdt.shape[1]), F32))
        out_specs.append(pl.BlockSpec((tm, wdt.shape[1]), lambda i, j: (i, 0)))
    kern = functools.partial(_inproj_kernel, even=even, has_dt=wdt is not None, tn=tn)
    return pl.pallas_call(
        kern,
        out_shape=out_shape,
        grid=(t // tm, n // tn),
        in_specs=in_specs,
        out_specs=out_specs,
        scratch_shapes=[pltpu.VMEM((tm, d), BF16)],
        compiler_params=_cparams("parallel", "arbitrary"),
        name="inproj_even" if even else "inproj_odd",
    )(*args)


RET_UNROLL = 8
_TN = (((0,), (0,)), ((), ()))
_NT = (((1,), (1,)), ((), ()))


def _retention_kernel(q_ref, k_ref, v_ref, g_ref, tab_ref, sf0_ref, sb0_ref,
                      o_ref, sf_ref, sb_ref, sb_store, s_run, *, nchunk, nblk):
    p = pl.program_id(2)
    s = pl.program_id(3)
    q = RET_CHUNK
    dcomb = tab_ref[0, 0]

    @pl.when(p == 0)
    def _():
        @pl.when(s == 0)
        def _():
            s_run[...] = sb0_ref[0, 0]

        blk = nblk - 1 - s
        kb, cb = tab_ref[0, 4], tab_ref[0, 6]

        def body(ci, carry):
            c = nchunk - 1 - ci
            rows = pl.ds(pl.multiple_of(c * q, q), q)
            st = s_run[...]
            sb_store[blk * nchunk + c] = st.astype(BF16)
            kd = (k_ref[rows, :].astype(F32) * kb).astype(BF16)
            upd = lax.dot_general(kd, v_ref[rows, :], _TN, preferred_element_type=F32)
            s_run[...] = cb * st + upd
            return carry

        lax.fori_loop(0, nchunk, body, 0, unroll=min(nchunk, RET_UNROLL))

        @pl.when(s == nblk - 1)
        def _():
            sb_ref[0, 0] = s_run[...]

    @pl.when(p == 1)
    def _():
        @pl.when(s == 0)
        def _():
            s_run[...] = sf0_ref[0, 0]

        qf, qb, kf, cf = tab_ref[0, 1], tab_ref[0, 2], tab_ref[0, 3], tab_ref[0, 5]

        def body(c, carry):
            rows = pl.ds(pl.multiple_of(c * q, q), q)
            qc, kc, vc = q_ref[rows, :], k_ref[rows, :], v_ref[rows, :]
            qc32, kc32 = qc.astype(F32), kc.astype(F32)
            st = s_run[...]
            sc = lax.dot_general(qc, kc, _NT, preferred_element_type=F32) * dcomb
            o = jnp.dot(sc.astype(BF16), vc, preferred_element_type=F32)
            qcat = jnp.concatenate([(qc32 * qf).astype(BF16), (qc32 * qb).astype(BF16)], axis=1)
            scat = jnp.concatenate([st.astype(BF16), sb_store[s * nchunk + c]], axis=0)
            o = o + jnp.dot(qcat, scat, preferred_element_type=F32)
            upd = lax.dot_general((kc32 * kf).astype(BF16), vc, _TN, preferred_element_type=F32)
            s_run[...] = cf * st + upd
            o = o * lax.rsqrt(jnp.mean(o * o, axis=-1, keepdims=True) + EPS)
            o_ref[rows, :] = (o * _silu(g_ref[rows, :].astype(F32))).astype(o_ref.dtype)
            return carry

        lax.fori_loop(0, nchunk, body, 0, unroll=min(nchunk, RET_UNROLL))

        @pl.when(s == nblk - 1)
        def _():
            sf_ref[0, 0] = s_run[...]


def _retention_tables(ret_logit):
    q = RET_CHUNK
    lg = jax.nn.log_sigmoid(ret_logit.astype(F32))
    lf, lb = lg[0][:, None, None], lg[1][:, None, None]
    i = jnp.arange(q, dtype=F32)[None, :, None]
    j = jnp.arange(q, dtype=F32)[None, None, :]
    ones = jnp.ones((1, 1, q), F32)
    dcomb = (jnp.where(i >= j, jnp.exp(lf * jnp.maximum(i - j, 0.0)), 0.0)
             + jnp.where(j >= i, jnp.exp(lb * jnp.maximum(j - i, 0.0)), 0.0))
    qf = jnp.exp(lf * (i + 1.0)) * ones
    qb = jnp.exp(lb * (q - i)) * ones
    kf = jnp.exp(lf * (q - 1.0 - i)) * ones
    kb = jnp.exp(lb * i) * ones
    cf = jnp.exp(lf * q) * jnp.ones((1, q, q), F32)
    cb = jnp.exp(lb * q) * jnp.ones((1, q, q), F32)
    return jnp.stack([dcomb, qf, qb, kf, kb, cf, cb], axis=1)


def _retention(proj, tabs, sf0, sb0, *, batch, seq, rblk):
    heads = tabs.shape[0]
    rblk = min(rblk, seq)
    nblk = seq // rblk
    nchunk = rblk // RET_CHUNK
    hd = HEAD_DIM

    def qmap(off):
        return lambda b, h, p, s: (b * nblk + p * s, off + h)

    def kvmap(off):
        return lambda b, h, p, s: (b * nblk + s + (1 - p) * (nblk - 1 - 2 * s), off + h)

    st_spec = pl.BlockSpec((1, 1, hd, hd), lambda b, h, p, s: (b, h, 0, 0))
    kern = functools.partial(_retention_kernel, nchunk=nchunk, nblk=nblk)
    return pl.pallas_call(
        kern,
        out_shape=[jax.ShapeDtypeStruct((batch * seq, heads * hd), BF16),
                   jax.ShapeDtypeStruct((batch, heads, hd, hd), F32),
                   jax.ShapeDtypeStruct((batch, heads, hd, hd), F32)],
        grid=(batch, heads, 2, nblk),
        in_specs=[pl.BlockSpec((rblk, hd), qmap(0)),
                  pl.BlockSpec((rblk, hd), kvmap(heads)),
                  pl.BlockSpec((rblk, hd), kvmap(2 * heads)),
                  pl.BlockSpec((rblk, hd), qmap(3 * heads)),
                  pl.BlockSpec((1, 7, RET_CHUNK, RET_CHUNK), lambda b, h, p, s: (h, 0, 0, 0)),
                  st_spec, st_spec],
        out_specs=[pl.BlockSpec((rblk, hd), qmap(0)), st_spec, st_spec],
        scratch_shapes=[pltpu.VMEM((seq // RET_CHUNK, hd, hd), BF16), pltpu.VMEM((hd, hd), F32)],
        compiler_params=_cparams("parallel", "parallel", "arbitrary", "arbitrary"),
        name="retention",
    )(proj, proj, proj, proj, tabs, sf0, sb0)


NA_QROWS = 8
NA_WROWS = 16
NA_KBLK = 4
NA_QSPLIT = 2


def _na_kernel(q_ref, k0, k1, k2, k3, v0, v1, v2, v3, kc_ref, vc_ref, bias_ref, o_ref):
    kb = NA_KBLK * GRID_W
    nq = q_ref.shape[0] // NA_QSPLIT
    for part in range(NA_QSPLIT):
        rows = slice(part * nq, (part + 1) * nq)
        q = q_ref[rows, :]
        scores = []
        for i, kr in enumerate((k0, k1, k2, k3)):
            sc = lax.dot_general(q, kr[...], _NT, preferred_element_type=F32)
            scores.append(sc + bias_ref[0, 0, rows, i * kb:(i + 1) * kb])
        scores.append(lax.dot_general(q, kc_ref[...], _NT, preferred_element_type=F32))
        m = scores[0].max(axis=-1, keepdims=True)
        for sc in scores[1:]:
            m = jnp.maximum(m, sc.max(axis=-1, keepdims=True))
        acc = jnp.zeros((nq, HEAD_DIM), F32)
        l = jnp.zeros((nq, 1), F32)
        for sc, vr in zip(scores, (v0, v1, v2, v3, vc_ref)):
            pr = jnp.exp(sc - m)
            l = l + pr.sum(axis=-1, keepdims=True)
            acc = acc + jnp.dot(pr.astype(BF16), vr[...], preferred_element_type=F32)
        o_ref[rows, :] = (acc / l).astype(o_ref.dtype)


def _na_bias_tables(rpb, rows):
    nrb = rows // NA_QROWS
    nr, nc = 2 * NA_ROWS - 1, 2 * NA_COLS - 1
    hi = lax.Precision.HIGHEST
    qc = np.arange(GRID_W)[:, None]
    kc = np.arange(GRID_W)[None, :]
    c0 = np.clip(qc - NA_COLS // 2, 0, GRID_W - NA_COLS)
    col_valid = (kc >= c0) & (kc < c0 + NA_COLS)
    sel_c = (np.arange(nc)[:, None, None] == (kc - qc + NA_COLS - 1)[None]).astype(np.float32)
    toep = jnp.einsum('haj,jqk->haqk', rpb.astype(F32), sel_c, precision=hi)
    qr = np.arange(NA_QROWS)[:, None]
    kr = np.arange(NA_WROWS)[None, :]
    tabs = []
    for rb in (0, 1, nrb - 1):
        w0 = int(np.clip(rb * NA_QROWS - NA_ROWS // 2, 0, rows - NA_WROWS))
        r = rb * NA_QROWS + qr
        krow = w0 + kr
        r0 = np.clip(r - NA_ROWS // 2, 0, rows - NA_ROWS)
        row_valid = (krow >= r0) & (krow < r0 + NA_ROWS)
        sel_r = ((np.arange(nr)[None, None, :] == (krow - r + NA_ROWS - 1)[:, :, None])
                 & row_valid[:, :, None]).astype(np.float32)
        b = jnp.einsum('rka,haqc->hrqkc', sel_r, toep, precision=hi)
        valid = row_valid[:, None, :, None] & col_valid[None, :, None, :]
        b = jnp.where(valid[None], b, NEG_INF)
        tabs.append(b.reshape(rpb.shape[0], NA_QROWS * GRID_W, NA_WROWS * GRID_W))
    return jnp.stack(tabs, axis=1)


def _neighbourhood_attention(proj, proj_ctx, bias, *, batch, seq, ctx_len):
    heads = bias.shape[0]
    hd = HEAD_DIM
    qblk = NA_QROWS * GRID_W
    kblk = NA_KBLK * GRID_W
    nrb = seq // qblk
    nkb = seq // kblk
    assert seq // GRID_W >= NA_WROWS + NA_QROWS
    qoff, koff, voff = 4 * heads, 5 * heads, 6 * heads

    def kmap(off, i):
        def f(b, h, r):
            u0 = jnp.clip(2 * r - 1, 0, nkb - NA_WROWS // NA_KBLK)
            return (b * nkb + u0 + i, off + h)
        return f

    def bias_map(b, h, r):
        return (h, (r > 0).astype(jnp.int32) + (r == nrb - 1).astype(jnp.int32), 0, 0)

    kspecs = [pl.BlockSpec((kblk, hd), kmap(koff, i)) for i in range(4)]
    vspecs = [pl.BlockSpec((kblk, hd), kmap(voff, i)) for i in range(4)]
    return pl.pallas_call(
        _na_kernel,
        out_shape=jax.ShapeDtypeStruct((batch * seq, heads * hd), BF16),
        grid=(batch, heads, nrb),
        in_specs=[pl.BlockSpec((qblk, hd), lambda b, h, r: (b * nrb + r, qoff + h))] + kspecs + vspecs + [
            pl.BlockSpec((ctx_len, hd), lambda b, h, r: (b, koff + h)),
            pl.BlockSpec((ctx_len, hd), lambda b, h, r: (b, voff + h)),
            pl.BlockSpec((1, 1, qblk, NA_WROWS * GRID_W), bias_map)],
        out_specs=pl.BlockSpec((qblk, hd), lambda b, h, r: (b * nrb + r, h)),
        compiler_params=_cparams("parallel", "parallel", "arbitrary"),
        name="neighbourhood_attention",
    )(*([proj] * 9), proj_ctx, proj_ctx, bias)


def _ctx_attn_kernel(q_ref, k_ref, v_ref, o_ref):
    sc = lax.dot_general(q_ref[...], k_ref[...], _NT, preferred_element_type=F32)
    pr = jnp.exp(sc - sc.max(axis=-1, keepdims=True))
    acc = jnp.dot(pr.astype(BF16), v_ref[...], preferred_element_type=F32)
    o_ref[...] = (acc / pr.sum(axis=-1, keepdims=True)).astype(o_ref.dtype)


def _ctx_attention(proj_ctx, *, batch, ctx_len, heads):
    hd = HEAD_DIM
    return pl.pallas_call(
        _ctx_attn_kernel,
        out_shape=jax.ShapeDtypeStruct((batch * ctx_len, heads * hd), BF16),
        grid=(batch, heads),
        in_specs=[pl.BlockSpec((ctx_len, hd), lambda b, h: (b, 4 * heads + h)),
                  pl.BlockSpec((ctx_len, hd), lambda b, h: (b, 5 * heads + h)),
                  pl.BlockSpec((ctx_len, hd), lambda b, h: (b, 6 * heads + h))],
        out_specs=pl.BlockSpec((ctx_len, hd), lambda b, h: (b, h)),
        compiler_params=_cparams("parallel", "parallel"),
        name="context_attention",
    )(proj_ctx, proj_ctx, proj_ctx)


OUTPROJ_SPLIT = 2


def _split_bf16(x):
    hi = x.astype(BF16)
    return hi, (x - hi.astype(F32)).astype(BF16)


def _outproj_kernel(*refs, odd):
    it = iter(refs)
    if odd:
        y_ref, z_ref, nws_ref = next(it), next(it), next(it)
    else:
        mr_ref, mn_ref = next(it), next(it)
    w_ref, lat_ref, g1_ref, nwf_ref, sh_ref, sc_ref = (next(it) for _ in range(6))
    wrh_ref, wrl_ref = next(it), next(it)
    lat2_ref, h_ref, lg_ref = next(it), next(it), next(it)

    nsplit = 1 if odd else OUTPROJ_SPLIT
    nr = lat_ref.shape[0] // nsplit
    for part in range(nsplit):
        rows = slice(part * nr, (part + 1) * nr)
        if odd:
            u = y_ref[rows, :].astype(F32) * _silu(z_ref[rows, :].astype(F32))
            u = u * lax.rsqrt(jnp.mean(u * u, axis=-1, keepdims=True) + EPS) * nws_ref[...]
            o = jnp.dot(u.astype(BF16), w_ref[...], preferred_element_type=F32)
        else:
            half = mr_ref.shape[1]
            o = (jnp.dot(mr_ref[rows, :], w_ref[0:half, :], preferred_element_type=F32)
                 + jnp.dot(mn_ref[rows, :], w_ref[half:2 * half, :], preferred_element_type=F32))
        lat2 = lat_ref[rows, :] + g1_ref[0] * o
        lat2_ref[rows, :] = lat2
        h = _rms_modulate(lat2, nwf_ref[...], sh_ref[0], sc_ref[0])
        h_ref[rows, :] = h
        hh, hl = _split_bf16(h)
        wrh = wrh_ref[...]
        lg_ref[rows, :] = (jnp.dot(hh, wrh, preferred_element_type=F32)
                           + jnp.dot(hl, wrh, preferred_element_type=F32)
                           + jnp.dot(hh, wrl_ref[...], preferred_element_type=F32))


def _outproj(mix, w, lat, g1, nwf, sh, sc, wrh, wrl, *, seg_rows, tm, ssd=None):
    t, d = lat.shape
    tm = min(tm, t)
    assert t % tm == 0 and seg_rows % tm == 0
    odd = ssd is not None
    row = lambda i: (i, 0)
    seg = pl.BlockSpec((1, 1, d), lambda i: ((i * tm) // seg_rows, 0, 0))
    const = lambda shape: pl.BlockSpec(shape, lambda i: (0,) * len(shape))
    if odd:
        y, zx, nws = ssd
        kin = y.shape[1]
        args = [y, zx, nws.reshape(1, kin)]
        in_specs = [pl.BlockSpec((tm, kin), row), pl.BlockSpec((tm, kin), row), const((1, kin))]
    else:
        mr, mn = mix
        args = [mr, mn]
        in_specs = [pl.BlockSpec((tm, mr.shape[1]), row), pl.BlockSpec((tm, mn.shape[1]), row)]
    args += [w, lat, g1, nwf.reshape(1, d), sh, sc, wrh, wrl]
    in_specs += [pl.BlockSpec(w.shape, lambda i: (0, 0), pipeline_mode=pl.Buffered(1)),
                 pl.BlockSpec((tm, d), row), seg, const((1, d)), seg, seg,
                 const(wrh.shape), const(wrl.shape)]
    nl = wrh.shape[1]
    return pl.pallas_call(
        functools.partial(_outproj_kernel, odd=odd),
        out_shape=[jax.ShapeDtypeStruct((t, d), F32),
                   jax.ShapeDtypeStruct((t, d), F32),
                   jax.ShapeDtypeStruct((t, nl), F32)],
        grid=(t // tm,),
        in_specs=in_specs,
        out_specs=[pl.BlockSpec((tm, d), row), pl.BlockSpec((tm, d), row), pl.BlockSpec((tm, nl), row)],
        compiler_params=_cparams("parallel"),
        name="outproj_odd" if odd else "outproj_even",
    )(*args)


ROUTER_TILE = 256
META_E, META_RANK, META_GATE = 0, 2, 4


def _router_kernel(lg_ref, rb_ref, meta_ref, cnt_ref, run_scr):
    tm = lg_ref.shape[0]
    i = pl.program_id(0)

    @pl.when(i == 0)
    def _():
        run_scr[...] = jnp.zeros_like(run_scr)

    big = 1e9
    lane = lax.broadcasted_iota(jnp.int32, (tm, LANES), 1)
    lanef = lane.astype(F32)
    scores = 1.0 / (1.0 + jnp.exp(-lg_ref[...]))
    biased = jnp.where(lane < N_EXPERTS, scores + rb_ref[...], NEG_INF)
    member = lane % EXPERTS_PER_GROUP
    best = jnp.full((tm, LANES), NEG_INF, F32)
    for k in range(1, EXPERTS_PER_GROUP):
        best = jnp.maximum(best, jnp.where(member >= k, biased + pltpu.roll(biased, k, axis=1), NEG_INF))
    grp = (lane // EXPERTS_PER_GROUP).astype(F32)
    top = best.max(axis=-1, keepdims=True)
    g_sel = jnp.where(best == top, grp, big).min(axis=-1, keepdims=True)
    v = jnp.where(grp == g_sel, biased, NEG_INF)
    i1 = jnp.where(v == v.max(axis=-1, keepdims=True), lanef, big).min(axis=-1, keepdims=True)
    v = jnp.where(lanef == i1, NEG_INF, v)
    i2 = jnp.where(v == v.max(axis=-1, keepdims=True), lanef, big).min(axis=-1, keepdims=True)
    oh1, oh2 = lanef == i1, lanef == i2
    s1 = jnp.where(oh1, scores, 0.0).sum(axis=-1, keepdims=True)
    s2 = jnp.where(oh2, scores, 0.0).sum(axis=-1, keepdims=True)
    den = s1 + s2
    oh = oh1.astype(F32) + oh2.astype(F32)
    earlier = (lax.broadcasted_iota(jnp.int32, (tm, tm), 0)
               > lax.broadcasted_iota(jnp.int32, (tm, tm), 1)).astype(BF16)
    base = jnp.dot(earlier, oh.astype(BF16), preferred_element_type=F32) + run_scr[0:1, :]
    r1 = jnp.where(oh1, base, 0.0).sum(axis=-1, keepdims=True)
    r2 = jnp.where(oh2, base, 0.0).sum(axis=-1, keepdims=True)
    total = run_scr[0:1, :] + oh.sum(axis=0, keepdims=True)
    run_scr[0:1, :] = total
    cnt_ref[...] = jnp.broadcast_to(total, cnt_ref.shape)
    meta = jnp.zeros((tm, LANES), F32)
    for ln, val in ((META_E, i1), (META_E + 1, i2), (META_RANK, r1), (META_RANK + 1, r2),
                    (META_GATE, s1 / den), (META_GATE + 1, s2 / den)):
        meta = jnp.where(lane == ln, val, meta)
    meta_ref[...] = meta


def _router(logits, router_bias):
    t = logits.shape[0]
    tm = ROUTER_TILE
    rb = jnp.zeros((1, LANES), F32).at[0, :N_EXPERTS].set(router_bias.astype(F32))
    return pl.pallas_call(
        _router_kernel,
        out_shape=[jax.ShapeDtypeStruct((t, LANES), F32), jax.ShapeDtypeStruct((SUBLANES, LANES), F32)],
        grid=(t // tm,),
        in_specs=[pl.BlockSpec((tm, LANES), lambda i: (i, 0)), pl.BlockSpec((1, LANES), lambda i: (0, 0))],
        out_specs=[pl.BlockSpec((tm, LANES), lambda i: (i, 0)),
                   pl.BlockSpec((SUBLANES, LANES), lambda i: (0, 0))],
        scratch_shapes=[pltpu.VMEM((SUBLANES, LANES), F32)],
        compiler_params=_cparams("arbitrary"),
        name="moe_router",
    )(logits, rb)


def _moe_plan(meta, cnt):
    t = meta.shape[0]
    blk = MOE_BLOCK
    counts = cnt[0, :N_EXPERTS].astype(jnp.int32)
    ends = jnp.cumsum(counts)
    offsets = ends - counts
    e = meta[:, META_E:META_E + TOP_K].astype(jnp.int32)
    rank = meta[:, META_RANK:META_RANK + TOP_K].astype(jnp.int32)
    onehot = e[..., None] == jnp.arange(N_EXPERTS, dtype=jnp.int32)
    dest = jnp.sum(jnp.where(onehot, offsets, 0), axis=-1) + rank
    n_tiles = t * TOP_K // blk
    tile_start = jnp.arange(n_tiles, dtype=jnp.int32) * blk
    def count_le(sorted_vals, query):
        return jnp.sum(sorted_vals[None, :] <= query[:, None], axis=-1).astype(jnp.int32)

    first_e = count_le(ends, tile_start)
    last_e = count_le(ends, tile_start + blk - 1)
    nsteps = last_e - first_e + 1
    step_end = jnp.cumsum(nsteps)
    step_start = step_end - nsteps
    s = jnp.arange(n_tiles + N_EXPERTS - 1, dtype=jnp.int32)
    valid = s < step_end[-1]
    tile_of = jnp.minimum(count_le(step_end, s), n_tiles - 1)
    exp_of = jnp.where(valid, first_e[tile_of] + s - step_start[tile_of], last_e[-1])
    exp_of = jnp.clip(exp_of, 0, N_EXPERTS - 1).astype(jnp.int32)
    lo = jnp.clip(offsets[exp_of] - tile_of * blk, 0, blk)
    hi = jnp.clip(ends[exp_of] - tile_of * blk, 0, blk)
    lo = jnp.where(valid, lo, 0).astype(jnp.int32)
    hi = jnp.where(valid, hi, 0).astype(jnp.int32)
    first = (valid & (s == step_start[tile_of])).astype(jnp.int32)
    return dest.astype(jnp.int32), (tile_of, exp_of, lo, hi, first)


DMA_UNROLL = 8


def _dispatch_kernel(*refs, tm, src_tiles):
    dest_ref, srcs, xs_hbm, sem = refs[0], refs[1:-2], refs[-2], refs[-1]
    i = pl.program_id(0)

    def scatter_from(src):
        def body(r, c):
            for k in range(TOP_K):
                pltpu.make_async_copy(src.at[pl.ds(r, 1), :],
                                      xs_hbm.at[pl.ds(dest_ref[0, 0, TOP_K * r + k], 1), :], sem).start()
            return c
        lax.fori_loop(0, tm, body, 0, unroll=DMA_UNROLL)
        for _ in range(TOP_K):
            pltpu.make_async_copy(src, xs_hbm.at[pl.ds(0, tm), :], sem).wait()

    first_tile = 0
    for src, ntile in zip(srcs, src_tiles):
        @pl.when((i >= first_tile) & (i < first_tile + ntile))
        def _(src=src):
            scatter_from(src)
        first_tile += ntile


def _moe_dispatch(hs, dest):
    d = hs[0].shape[1]
    t = dest.shape[0]
    tm = 256
    nt = t // tm
    src_tiles = tuple(h.shape[0] // tm for h in hs)
    assert sum(src_tiles) == nt and all(h.shape[0] % tm == 0 for h in hs)
    starts = np.cumsum((0,) + src_tiles[:-1])
    src_specs = [pl.BlockSpec((tm, d), lambda i, s=int(s), n=n: (jnp.clip(i - s, 0, n - 1), 0))
                 for s, n in zip(starts, src_tiles)]
    kern = functools.partial(_dispatch_kernel, tm=tm, src_tiles=src_tiles)
    return pl.pallas_call(
        kern,
        out_shape=jax.ShapeDtypeStruct((TOP_K * t, d), F32),
        grid=(nt,),
        in_specs=[pl.BlockSpec((1, 1, TOP_K * tm), lambda i: (i, 0, 0), memory_space=pltpu.SMEM)]
        + src_specs,
        out_specs=pl.BlockSpec(memory_space=pl.ANY),
        scratch_shapes=[pltpu.SemaphoreType.DMA(())],
        compiler_params=_cparams("arbitrary"),
        name="moe_dispatch",
    )(dest.reshape(nt, 1, TOP_K * tm), *hs)


EXPERT_SPLIT = 1


def _expert_kernel(tile_ref, exp_ref, lo_ref, hi_ref, first_ref, x_ref, wg_ref, wu_ref, wd_ref,
                   o_ref, wg_s, wu_s, wd_s, cur_e):
    s = pl.program_id(0)
    lo, hi, e = lo_ref[s], hi_ref[s], exp_ref[s]

    @pl.when(s == 0)
    def _():
        cur_e[0] = -1

    @pl.when(hi > lo)
    def _():
        @pl.when(cur_e[0] != e)
        def _():
            wg_s[...] = wg_ref[0, 0].astype(BF16)
            wu_s[...] = wu_ref[0, 0].astype(BF16)
            wd_s[...] = wd_ref[0, 0].astype(BF16)
            cur_e[0] = e

        nr = x_ref.shape[0] // EXPERT_SPLIT
        ys = []
        for part in range(EXPERT_SPLIT):
            x = x_ref[part * nr:(part + 1) * nr, :].astype(BF16)
            a = jnp.dot(x, wg_s[...], preferred_element_type=F32)
            u = jnp.dot(x, wu_s[...], preferred_element_type=F32)
            hid = (_silu(a) * u).astype(BF16)
            ys.append(jnp.dot(hid, wd_s[...], preferred_element_type=F32))
        y = jnp.concatenate(ys, axis=0)
        row = lax.broadcasted_iota(jnp.int32, (y.shape[0], 1), 0)
        mine = (row >= lo) & (row < hi)

        @pl.when(first_ref[s] == 1)
        def _():
            o_ref[...] = jnp.where(mine, y, 0.0)

        @pl.when(first_ref[s] == 0)
        def _():
            o_ref[...] = jnp.where(mine, y, o_ref[...])


def _expert_ffn(xs, plan, w_gate, w_up, w_down, layer):
    n_rows, d = xs.shape
    hid = w_gate.shape[3]
    n_steps = plan[0].shape[0]
    xmap = lambda s, tile, exp, lo, hi, first: (tile[s], 0)
    wmap = lambda s, tile, exp, lo, hi, first: (layer, exp[s], 0, 0)
    return pl.pallas_call(
        _expert_kernel,
        out_shape=jax.ShapeDtypeStruct((n_rows, d), F32),
        grid_spec=pltpu.PrefetchScalarGridSpec(
            num_scalar_prefetch=5,
            grid=(n_steps,),
            in_specs=[pl.BlockSpec((MOE_BLOCK, d), xmap),
                      pl.BlockSpec((1, 1, d, hid), wmap),
                      pl.BlockSpec((1, 1, d, hid), wmap),
                      pl.BlockSpec((1, 1, hid, d), wmap)],
            out_specs=pl.BlockSpec((MOE_BLOCK, d), xmap),
            scratch_shapes=[pltpu.VMEM((d, hid), BF16), pltpu.VMEM((d, hid), BF16),
                            pltpu.VMEM((hid, d), BF16), pltpu.SMEM((1,), jnp.int32)]),
        compiler_params=_cparams("arbitrary"),
        name="moe_experts",
    )(*plan, xs, w_gate, w_up, w_down)


def _combine_kernel(pos_ref, posn_ref, ys_hbm, meta_ref, lat_ref, g2_ref, nw_ref, o_ref, buf, sem,
                    *, final, tc):
    i = pl.program_id(0)
    nt = pl.num_programs(0)
    slot = i % 2
    n = TOP_K * tc

    def issue(idx_ref, sl):
        def body(r, c):
            pltpu.make_async_copy(ys_hbm.at[pl.ds(idx_ref[0, 0, r], 1), :],
                                  buf.at[sl, pl.ds(r, 1), :], sem.at[sl]).start()
            return c
        lax.fori_loop(0, n, body, 0, unroll=DMA_UNROLL)

    @pl.when(i == 0)
    def _():
        issue(pos_ref, 0)

    @pl.when(i + 1 < nt)
    def _():
        issue(posn_ref, 1 - slot)

    pltpu.make_async_copy(ys_hbm.at[pl.ds(0, n), :], buf.at[slot], sem.at[slot]).wait()
    gk = meta_ref[:, META_GATE:META_GATE + TOP_K]
    y = gk[:, 0:1] * buf[slot, 0:tc, :] + gk[:, 1:2] * buf[slot, tc:2 * tc, :]
    x = lat_ref[...] + g2_ref[0] * y
    if final:
        x = x * lax.rsqrt(jnp.mean(x * x, axis=-1, keepdims=True) + EPS) * nw_ref[...]
    o_ref[...] = x


def _moe_combine(ys, pos, meta, lat, g2, nw, *, seg_rows, final):
    t, d = lat.shape
    tc = 256
    nt = t // tc
    n = TOP_K * tc
    kern = functools.partial(_combine_kernel, final=final, tc=tc)
    posb = pos.reshape(nt, tc, TOP_K).transpose(0, 2, 1).reshape(nt, 1, n)
    return pl.pallas_call(
        kern,
        out_shape=jax.ShapeDtypeStruct((t, d), F32),
        grid=(nt,),
        in_specs=[pl.BlockSpec((1, 1, n), lambda i: (i, 0, 0), memory_space=pltpu.SMEM),
                  pl.BlockSpec((1, 1, n), lambda i: (jnp.minimum(i + 1, nt - 1), 0, 0),
                               memory_space=pltpu.SMEM),
                  pl.BlockSpec(memory_space=pl.ANY),
                  pl.BlockSpec((tc, LANES), lambda i: (i, 0)),
                  pl.BlockSpec((tc, d), lambda i: (i, 0)),
                  pl.BlockSpec((1, 1, d), lambda i: ((i * tc) // seg_rows, 0, 0)),
                  pl.BlockSpec((1, d), lambda i: (0, 0))],
        out_specs=pl.BlockSpec((tc, d), lambda i: (i, 0)),
        scratch_shapes=[pltpu.VMEM((2, n, d), F32), pltpu.SemaphoreType.DMA((2,))],
        compiler_params=_cparams("arbitrary"),
        name="moe_combine",
    )(posb, posb, ys, meta, lat, g2, nw.reshape(1, d))


CONV_HALO = 16


def _conv_kernel(prev_ref, cur_ref, next_ref, w_ref, b_ref, o_ref, *, seq, rblk):
    i = pl.program_id(0)
    first = (i * rblk) % seq == 0
    last = ((i + 1) * rblk) % seq == 0
    half = CONV_HALO // 2
    pv = jnp.where(first, 0.0, prev_ref[...].astype(F32))[half:, :]
    nx = jnp.where(last, 0.0, next_ref[...].astype(F32))[:half, :]
    xm = jnp.concatenate([pv, cur_ref[...].astype(F32), nx], axis=0)
    n = rblk + 2 * half
    acc = jnp.zeros((rblk, xm.shape[1]), F32) + b_ref[...]
    for k in range(SSD_CONV):
        sh = (SSD_CONV // 2 - k) % n
        xs = xm if sh == 0 else pltpu.roll(xm, sh, axis=0)
        acc = acc + w_ref[k:k + 1, :] * xs[half:half + rblk, :]
    o_ref[...] = _silu(acc).astype(o_ref.dtype)


def _conv_silu(zx, conv_w, conv_b, *, seq, col_off):
    t = zx.shape[0]
    c = conv_w.shape[1]
    rblk, tn = min(512, seq), 1024
    assert seq % rblk == 0 and c % tn == 0 and col_off % tn == 0
    cb = col_off // tn
    hb = rblk // CONV_HALO
    nhb = t // CONV_HALO
    kern = functools.partial(_conv_kernel, seq=seq, rblk=rblk)
    wpad = jnp.zeros((SUBLANES, c), F32).at[:SSD_CONV].set(conv_w.astype(F32))
    return pl.pallas_call(
        kern,
        out_shape=jax.ShapeDtypeStruct((t, c), BF16),
        grid=(t // rblk, c // tn),
        in_specs=[pl.BlockSpec((CONV_HALO, tn), lambda i, j: (jnp.maximum(i * hb - 1, 0), cb + j)),
                  pl.BlockSpec((rblk, tn), lambda i, j: (i, cb + j)),
                  pl.BlockSpec((CONV_HALO, tn), lambda i, j: (jnp.minimum((i + 1) * hb, nhb - 1), cb + j)),
                  pl.BlockSpec((SUBLANES, tn), lambda i, j: (0, j)),
                  pl.BlockSpec((1, tn), lambda i, j: (0, j))],
        out_specs=pl.BlockSpec((rblk, tn), lambda i, j: (i, j)),
        compiler_params=_cparams("parallel", "parallel"),
        name="ssd_conv_silu",
    )(zx, zx, zx, wpad, conv_b.astype(F32).reshape(1, c))


LOG2E = 1.4426950408889634


def _softplus(x):
    return jnp.maximum(x, 0.0) + jnp.log1p(jnp.exp(-jnp.abs(x)))


def _ssd_kernel(*refs, reverse, second, nchunk, nblk, groups):
    it = iter(refs)
    x_ref, b_ref, c_ref, dt_ref, prm_ref, s0_ref = (next(it) for _ in range(6))
    if second:
        yprev_ref, dsk_ref = next(it), next(it)
    y_ref, sfin_ref, s_scr = next(it), next(it), next(it)
    acum_s, acumt_s, dtt_s, fwt_s, etot_s, cb_s, bgt_s, yint_s = (next(it) for _ in range(8))
    q = SSD_CHUNK
    nst = SSD_STATE
    gw = x_ref.shape[1] // groups
    pw = 2 * SSD_HEAD_DIM
    s = pl.program_id(1)

    @pl.when(s == 0)
    def _():
        s_scr[...] = s0_ref[0]

    ii = lax.broadcasted_iota(jnp.int32, (q, q), 0)
    jj = lax.broadcasted_iota(jnp.int32, (q, q), 1)
    tri = (jj >= ii) if reverse else (ii >= jj)
    tri_bf = tri.astype(BF16)
    lo_half = lax.broadcasted_iota(jnp.int32, (q, pw), 1) < SSD_HEAD_DIM
    a_neg = prm_ref[0:1, :]
    dtb = prm_ref[1:2, :]

    def chunk(ci, carry):
        c = (nchunk - 1 - ci) if reverse else ci
        rows = pl.ds(pl.multiple_of(c * q, q), q)
        dt = _softplus(dt_ref[rows, :] + dtb)
        da = dt * a_neg
        hi = da.astype(BF16)
        r1 = da - hi.astype(F32)
        mid = r1.astype(BF16)
        lo = (r1 - mid.astype(F32)).astype(BF16)
        acum = (jnp.dot(tri_bf, hi, preferred_element_type=F32)
                + jnp.dot(tri_bf, mid, preferred_element_type=F32)
                + jnp.dot(tri_bf, lo, preferred_element_type=F32))
        acum = acum * LOG2E
        total = acum[0:1, :] if reverse else acum[q - 1:q, :]
        acum_s[...] = acum
        acumt_s[...] = acum.T
        dtt_s[...] = dt.T
        fwt_s[...] = (jnp.exp2(total - acum) * dt).T
        etot_s[...] = jnp.broadcast_to(jnp.exp2(total), etot_s.shape)
        for g in range(groups):
            bg = b_ref[rows, g * nst:(g + 1) * nst]
            cg = c_ref[rows, g * nst:(g + 1) * nst]
            cb_s[...] = lax.dot_general(cg, bg, _NT, preferred_element_type=F32)
            bgt_s[...] = bg.astype(F32).T
            yint_s[...] = jnp.dot(cg, s_scr[g].astype(BF16), preferred_element_type=F32)
            for pr in range(gw // pw):
                h1 = g * (gw // SSD_HEAD_DIM) + 2 * pr
                h2 = h1 + 1
                cols = slice(g * gw + pr * pw, g * gw + (pr + 1) * pw)
                gcols = slice(pr * pw, (pr + 1) * pw)
                ws, bs, acols = [], [], []
                for h in (h1, h2):
                    acol = jnp.broadcast_to(acum_s[:, h:h + 1], (q, q))
                    lm = jnp.where(tri, jnp.exp2(acol - acumt_s[h:h + 1, :]), 0.0)
                    ws.append((cb_s[...] * lm * dtt_s[h:h + 1, :]).astype(BF16))
                    bs.append((bgt_s[...] * fwt_s[h:h + 1, :]).astype(BF16))
                    acols.append(acol)
                xp = x_ref[rows, cols].astype(F32)
                base = yint_s[:, gcols] * jnp.exp2(jnp.where(lo_half, acols[0], acols[1]))
                if second:
                    base = base + yprev_ref[rows, cols] + dsk_ref[:, cols] * xp
                lhs = jnp.concatenate([jnp.concatenate(ws, axis=1), jnp.concatenate(bs, axis=1)], axis=0)
                rhs = jnp.concatenate([jnp.where(lo_half, xp, 0.0), jnp.where(lo_half, 0.0, xp)],
                                      axis=0).astype(BF16)
                res = jnp.dot(lhs, rhs, preferred_element_type=F32)
                y_ref[rows, cols] = (res[0:q, :] + base).astype(y_ref.dtype)
                dec = jnp.where(lo_half[0:1, :], etot_s[0:1, h1:h1 + 1], etot_s[0:1, h2:h2 + 1])
                s_scr[g, :, gcols] = dec * s_scr[g, :, gcols] + res[q:q + nst, :]
        return carry

    lax.fori_loop(0, nchunk, chunk, 0)

    @pl.when(s == nblk - 1)
    def _():
        sfin_ref[0] = s_scr[...]


def _ssd_pass(act, dt, prm, s0, *, batch, seq, reverse, yprev=None, dskip=None):
    groups, nst = SSD_GROUPS, SSD_STATE
    inner = act.shape[1] - 2 * groups * nst
    rblk = min(256, seq)
    nblk = seq // rblk
    second = yprev is not None
    bc = groups * nst
    rmap = (lambda b, s: (b * nblk + nblk - 1 - s, 0)) if reverse else (lambda b, s: (b * nblk + s, 0))
    cmap = lambda k: ((lambda b, s: (b * nblk + nblk - 1 - s, k)) if reverse
                      else (lambda b, s: (b * nblk + s, k)))
    st_spec = pl.BlockSpec((1, groups, nst, inner // groups), lambda b, s: (b, 0, 0, 0))
    args = [act, act, act, dt, prm, s0]
    in_specs = [pl.BlockSpec((rblk, inner), rmap),
                pl.BlockSpec((rblk, bc), cmap(inner // bc)),
                pl.BlockSpec((rblk, bc), cmap(inner // bc + 1)),
                pl.BlockSpec((rblk, dt.shape[1]), rmap),
                pl.BlockSpec(prm.shape, lambda b, s: (0, 0)),
                st_spec]
    if second:
        args += [yprev, dskip]
        in_specs += [pl.BlockSpec((rblk, inner), rmap), pl.BlockSpec((1, inner), lambda b, s: (0, 0))]
    kern = functools.partial(_ssd_kernel, reverse=reverse, second=second, nchunk=rblk // SSD_CHUNK,
                             nblk=nblk, groups=groups)
    return pl.pallas_call(
        kern,
        out_shape=[jax.ShapeDtypeStruct((batch * seq, inner), BF16 if second else F32),
                   jax.ShapeDtypeStruct((batch, groups, nst, inner // groups), F32)],
        grid=(batch, nblk),
        in_specs=in_specs,
        out_specs=[pl.BlockSpec((rblk, inner), rmap), st_spec],
        scratch_shapes=[pltpu.VMEM((groups, nst, inner // groups), F32),
                        pltpu.VMEM((SSD_CHUNK, LANES), F32),
                        pltpu.VMEM((LANES, SSD_CHUNK), F32),
                        pltpu.VMEM((LANES, SSD_CHUNK), F32),
                        pltpu.VMEM((LANES, SSD_CHUNK), F32),
                        pltpu.VMEM((SUBLANES, LANES), F32),
                        pltpu.VMEM((SSD_CHUNK, SSD_CHUNK), F32),
                        pltpu.VMEM((nst, SSD_CHUNK), F32),
                        pltpu.VMEM((SSD_CHUNK, inner // groups), F32)],
        compiler_params=_cparams("parallel", "arbitrary"),
        name="ssd_scan_bwd" if reverse else "ssd_scan_fwd",
    )(*args)


def _rope_tables(seq):
    pos = jnp.arange(seq)
    rows = (pos // GRID_W).astype(F32)
    cols = (pos % GRID_W).astype(F32)
    half = HEAD_DIM // 2
    inv = ROPE_BASE ** (-jnp.arange(0, half, 2, dtype=F32) / half)
    ang = jnp.concatenate([rows[:, None] * inv, cols[:, None] * inv], axis=-1)
    cos, sin = jnp.cos(ang), jnp.sin(ang)
    return jnp.concatenate([cos, cos], axis=-1), jnp.concatenate([-sin, sin], axis=-1)


def _deinterleave_perm(n_cols, n_rot_cols):
    within = np.concatenate([np.arange(0, HEAD_DIM, 2), np.arange(1, HEAD_DIM, 2)])
    perm = np.arange(n_cols)
    for h in range(n_rot_cols // HEAD_DIM):
        perm[h * HEAD_DIM:(h + 1) * HEAD_DIM] = h * HEAD_DIM + within
    return perm


def _moe_layer(hs, logits, router_bias, w_gate, w_up, w_down, layer):
    meta, cnt = _router(logits, router_bias)
    dest, plan = _moe_plan(meta, cnt)
    xs = _moe_dispatch(hs, dest)
    ys = _expert_ffn(xs, plan, w_gate, w_up, w_down, layer)
    return ys, dest, meta


def kernel(x, c, ctx, c_ctx, w_mod, b_mod, norm_mix, norm_ffn, norm_final, w_in_even, w_out_even,
           ret_decay, na_rpb, w_in_odd, conv_w, conv_b, dt_bias, a_log, d_skip, ssd_norm, w_out_odd,
           w_router, router_bias, w_gate, w_up, w_down):
    batch, seq, d = x.shape
    ctx_len = ctx.shape[1]
    depth = w_mod.shape[0]
    assert batch + 1 <= SUBLANES
    tl, tc = batch * seq, batch * ctx_len

    cvec = jnp.zeros((SUBLANES, d), F32).at[0].set(c_ctx).at[1:1 + batch].set(c)
    mod = _modulation(cvec, w_mod, b_mod)

    def mod_vecs(i, k):
        m = mod[i, :, k * d:(k + 1) * d]
        return m[1:1 + batch].reshape(batch, 1, d), m[0:1].reshape(1, 1, d)

    wr = jnp.zeros((d, LANES), F32).at[:, :N_EXPERTS].set(w_router.astype(F32))
    wrh, wrl = _split_bf16(wr)

    lat = x.reshape(tl, d)
    cx = ctx.reshape(tc, d)
    out = None
    for i in range(depth):
        last = i == depth - 1
        j = i // 2
        (sh1, csh1), (sc1, csc1), (g1, cg1) = mod_vecs(i, 0), mod_vecs(i, 1), mod_vecs(i, 2)
        (sh2, csh2), (sc2, csc2), (g2, cg2) = mod_vecs(i, 3), mod_vecs(i, 4), mod_vecs(i, 5)
        if i % 2 == 0:
            ret_heads = w_out_even.shape[1] // (2 * HEAD_DIM)
            rot_cols = 2 * ret_heads * HEAD_DIM
            w_in = w_in_even[j][:, _deinterleave_perm(w_in_even.shape[2], rot_cols)].astype(BF16)
            cos, sin = _rope_tables(seq)
            ones, zeros = jnp.ones((tc, HEAD_DIM), F32), jnp.zeros((tc, HEAD_DIM), F32)
            proj_l, = _inproj(lat, norm_mix[i], sh1, sc1, w_in, seg_rows=seq, tm=1024, tn=1024,
                              even=True, cos=cos, sin=sin)
            proj_c, = _inproj(cx, norm_mix[i], csh1, csc1, w_in, seg_rows=tc, tm=tc, tn=1024,
                              even=True, cos=ones, sin=zeros)
            tabs = _retention_tables(ret_decay[j])
            zero = jnp.zeros((batch, ret_heads, HEAD_DIM, HEAD_DIM), F32)
            ret_c, s_f, s_b = _retention(proj_c, tabs, zero, zero, batch=batch, seq=ctx_len, rblk=ctx_len)
            ret_l, _, _ = _retention(proj_l, tabs, s_f, s_b, batch=batch, seq=seq, rblk=2048)
            bias = _na_bias_tables(na_rpb[j], seq // GRID_W)
            na_l = _neighbourhood_attention(proj_l, proj_c, bias, batch=batch, seq=seq, ctx_len=ctx_len)
            w_out = w_out_even[j].astype(BF16)
            mix_l, ssd_l, mix_c, ssd_c = (ret_l, na_l), None, None, None
            if not last:
                mix_c = (ret_c, _ctx_attention(proj_c, batch=batch, ctx_len=ctx_len, heads=ret_heads))
            tm_out = 512
        else:
            inner = w_out_odd.shape[1]
            n_main = inner + conv_w.shape[2]
            w_in = w_in_odd[j]
            w_main = w_in[:, :n_main].astype(BF16)
            heads = w_in.shape[1] - n_main
            w_dt = jnp.zeros((d, LANES), F32).at[:, :heads].set(w_in[:, n_main:]).astype(BF16)
            zx_l, dt_l = _inproj(lat, norm_mix[i], sh1, sc1, w_main, seg_rows=seq, tm=1024, tn=1024, wdt=w_dt)
            zx_c, dt_c = _inproj(cx, norm_mix[i], csh1, csc1, w_main, seg_rows=tc, tm=tc, tn=1024, wdt=w_dt)
            act_l = _conv_silu(zx_l, conv_w[j], conv_b[j], seq=seq, col_off=inner)
            act_c = _conv_silu(zx_c, conv_w[j], conv_b[j], seq=ctx_len, col_off=inner)
            a_neg = -jnp.exp(a_log[j].astype(F32))

            def prm(dr):
                return (jnp.zeros((SUBLANES, LANES), F32).at[0, :heads].set(a_neg[dr])
                        .at[1, :heads].set(dt_bias[j, dr].astype(F32)))

            dsk = jnp.repeat((d_skip[j, 0] + d_skip[j, 1]).astype(F32), inner // heads)[None, :]
            zero = jnp.zeros((batch, SSD_GROUPS, SSD_STATE, inner // SSD_GROUPS), F32)
            yf_c, s_f = _ssd_pass(act_c, dt_c, prm(0), zero, batch=batch, seq=ctx_len, reverse=False)
            y_c, s_b = _ssd_pass(act_c, dt_c, prm(1), zero, batch=batch, seq=ctx_len, reverse=True,
                                 yprev=yf_c, dskip=dsk)
            yf_l, _ = _ssd_pass(act_l, dt_l, prm(0), s_f, batch=batch, seq=seq, reverse=False)
            y_l, _ = _ssd_pass(act_l, dt_l, prm(1), s_b, batch=batch, seq=seq, reverse=True,
                               yprev=yf_l, dskip=dsk)
            w_out = w_out_odd[j].astype(BF16)
            mix_l, ssd_l = None, (y_l, zx_l, ssd_norm[j].astype(F32))
            mix_c, ssd_c = None, (None if last else (y_c, zx_c, ssd_norm[j].astype(F32)))
            tm_out = 256

        nwf = norm_ffn[i].astype(F32)
        lat2, h_l, lg_l = _outproj(mix_l, w_out, lat, g1, nwf, sh2, sc2, wrh, wrl, seg_rows=seq,
                                   tm=tm_out, ssd=ssd_l)
        if last:
            ys, pos, meta = _moe_layer([h_l], lg_l, router_bias, w_gate, w_up, w_down, i)
            out = _moe_combine(ys, pos, meta, lat2, g2, norm_final.astype(F32), seg_rows=seq, final=True)
        else:
            cx2, h_c, lg_c = _outproj(mix_c, w_out, cx, cg1, nwf, csh2, csc2, wrh, wrl, seg_rows=tc,
                                      tm=tm_out, ssd=ssd_c)
            ys, pos, meta = _moe_layer([h_c, h_l], jnp.concatenate([lg_c, lg_l], axis=0), router_bias,
                                       w_gate, w_up, w_down, i)
            ones = jnp.ones((d,), F32)
            cx = _moe_combine(ys, pos[:tc], meta[:tc], cx2, cg2, ones, seg_rows=tc, final=False)
            lat = _moe_combine(ys, pos[tc:], meta[tc:], lat2, g2, ones, seg_rows=seq, final=False)
    return out.reshape(batch, seq, d)
```
